```python
import jax, jax.numpy as jnp
from jax import lax
import numpy as np

D_MODEL = 2048
BATCH = 1
SEQ = 8192
DEPTH = 1

EPS = 1e-6
HG_HEADS = 16
HG_DK = 128
HG_DV = 128
HG_FDIM = HG_HEADS * HG_DK
HG_VDIM = HG_HEADS * HG_DV
HG_CHUNK = 64
SSM_DINNER = 2 * D_MODEL
SSM_HEADDIM = 64
SSM_HEADS = SSM_DINNER // SSM_HEADDIM
SSM_GROUPS = 8
SSM_HPG = SSM_HEADS // SSM_GROUPS
SSM_DSTATE = 128
SSM_CONV = 4
SSM_CHUNK = 256
SSM_CONV_DIM = SSM_DINNER + 2 * SSM_GROUPS * SSM_DSTATE
D_FF = 5632
FFN_CONV = 3
IN_SIZES = (HG_FDIM, HG_FDIM, HG_VDIM, HG_VDIM,
            SSM_DINNER, SSM_CONV_DIM, SSM_HEADS,
            D_MODEL, D_MODEL)
IN_TOTAL = sum(IN_SIZES)

kernel_name = "hgrn2_mamba2_gated_hybrid_block"


def rmsnorm(x, w):
    xf = x.astype(jnp.float32)
    y = xf * lax.rsqrt(jnp.mean(xf * xf, axis=-1, keepdims=True) + EPS)
    return (y * w.astype(jnp.float32)).astype(x.dtype)


def causal_dwconv(x, w, b):
    k = w.shape[0]
    c = x.shape[-1]
    y = lax.conv_general_dilated(x, w[:, None, :].astype(x.dtype), window_strides=(1,),
                                 padding=[(k - 1, 0)], dimension_numbers=('NWC', 'WIO', 'NWC'),
                                 feature_group_count=c)
    return y + b.astype(x.dtype)


def _to_chunks(a, c):
    b, t = a.shape[:2]
    n = -(-t // c)
    a = jnp.pad(a, ((0, 0), (0, n * c - t)) + ((0, 0),) * (a.ndim - 2))
    return jnp.moveaxis(a.reshape((b, n, c) + a.shape[2:]), 1, 0)


def _from_chunks(y, t):
    n, b, c = y.shape[:3]
    return jnp.moveaxis(y, 0, 1).reshape((b, n * c) + y.shape[3:])[:, :t]


def hgrn2_scan(q, k, v, logf):
    bsz, t, h, dk = q.shape
    dv = v.shape[-1]
    causal = jnp.tril(jnp.ones((HG_CHUNK, HG_CHUNK), dtype=bool))[:, :, None, None]

    def step(S, inp):
        qc, kc, vc, gc = inp
        bcum = jnp.cumsum(gc, axis=1)
        rel = jnp.exp(jnp.where(causal, bcum[:, :, None] - bcum[:, None, :], -jnp.inf))
        scores = jnp.einsum('bthk,bshk,btshk->bhts', qc, kc, rel)
        o = jnp.einsum('bhts,bshv->bthv', scores, vc)
        o = o + jnp.einsum('bthk,bhkv->bthv', qc * jnp.exp(bcum), S)
        b_last = bcum[:, -1]
        S = jnp.exp(b_last)[..., None] * S + jnp.einsum('bshk,bshv->bhkv', kc * jnp.exp(b_last[:, None] - bcum), vc)
        return S, o

    S0 = jnp.zeros((bsz, h, dk, dv), jnp.float32)
    xs = tuple(_to_chunks(a.astype(jnp.float32), HG_CHUNK) for a in (q, k, v, logf))
    _, o = lax.scan(step, S0, xs)
    return _from_chunks(o, t)


def ssd_scan(x, dA, Bm, Cm):
    bsz, t, g, r, p = x.shape
    n = Bm.shape[-1]
    causal = jnp.tril(jnp.ones((SSM_CHUNK, SSM_CHUNK), dtype=bool))[:, :, None, None]

    def step(S, inp):
        xc, ac, bc, cc = inp
        acum = jnp.cumsum(ac, axis=1)
        L = jnp.exp(jnp.where(causal, acum[:, :, None] - acum[:, None, :], -jnp.inf))
        cb = jnp.einsum('btgn,bsgn->bgts', cc, bc)
        y = jnp.einsum('bgts,btsgr,bsgrp->btgrp', cb, L, xc)
        y = y + jnp.einsum('btgn,bgrpn,btgr->btgrp', cc, S, jnp.exp(acum))
        a_last = acum[:, -1]
        S = jnp.exp(a_last)[..., None, None] * S + jnp.einsum(
            'bsgn,bsgr,bsgrp->bgrpn', bc, jnp.exp(a_last[:, None] - acum), xc)
        return S, y

    S0 = jnp.zeros((bsz, g, r, p, n), jnp.float32)
    xs = tuple(_to_chunks(a.astype(jnp.float32), SSM_CHUNK) for a in (x, dA, Bm, Cm))
    _, y = lax.scan(step, S0, xs)
    return _from_chunks(y, t)


def hgrn2_mixer(q, f_raw, i, g, lb, norm_w):
    bsz, t = q.shape[:2]
    f = lb + (1.0 - lb) * jax.nn.sigmoid(f_raw.astype(jnp.float32))
    qh = (jax.nn.silu(q.astype(jnp.float32)) * HG_DK ** -0.5).reshape(bsz, t, HG_HEADS, HG_DK)
    kh = (1.0 - f).reshape(bsz, t, HG_HEADS, HG_DK)
    lfh = jnp.log(f).reshape(bsz, t, HG_HEADS, HG_DK)
    vh = i.reshape(bsz, t, HG_HEADS, HG_DV)
    o = hgrn2_scan(qh, kh, vh, lfh)
    o = rmsnorm(o, norm_w) * jax.nn.silu(g.reshape(bsz, t, HG_HEADS, HG_DV).astype(jnp.float32))
    return o.reshape(bsz, t, HG_VDIM).astype(q.dtype)


def mamba2_mixer(z, xbc, dt_raw, conv_w, conv_b, dt_bias, A_log, D_skip, norm_w):
    bsz, t = z.shape[:2]
    xbc = jax.nn.silu(causal_dwconv(xbc, conv_w, conv_b))
    xs = xbc[..., :SSM_DINNER]
    Bm = xbc[..., SSM_DINNER:SSM_DINNER + SSM_GROUPS * SSM_DSTATE].reshape(bsz, t, SSM_GROUPS, SSM_DSTATE)
    Cm = xbc[..., SSM_DINNER + SSM_GROUPS * SSM_DSTATE:].reshape(bsz, t, SSM_GROUPS, SSM_DSTATE)
    dt = jax.nn.softplus(dt_raw.astype(jnp.float32) + dt_bias.astype(jnp.float32))
    A = -jnp.exp(A_log.astype(jnp.float32))
    xh = xs.astype(jnp.float32).reshape(bsz, t, SSM_GROUPS, SSM_HPG, SSM_HEADDIM)
    dtg = dt.reshape(bsz, t, SSM_GROUPS, SSM_HPG)
    y = ssd_scan(xh * dtg[..., None], dtg * A.reshape(SSM_GROUPS, SSM_HPG), Bm, Cm)
    y = y + D_skip.astype(jnp.float32).reshape(SSM_GROUPS, SSM_HPG, 1) * xh
    y = y.reshape(bsz, t, SSM_DINNER) * jax.nn.silu(z.astype(jnp.float32))
    y = rmsnorm(y.reshape(bsz, t, SSM_GROUPS, SSM_DINNER // SSM_GROUPS),
                norm_w.reshape(SSM_GROUPS, SSM_DINNER // SSM_GROUPS))
    return y.reshape(bsz, t, SSM_DINNER).astype(z.dtype)


def conv_ffn(h, w_up, conv_w, conv_b, w_down):
    gu = jnp.einsum('btd,df->btf', h, w_up)
    gate, up = gu[..., :D_FF], gu[..., D_FF:]
    gate = causal_dwconv(gate, conv_w, conv_b)
    return jnp.einsum('btf,fd->btd', jax.nn.gelu(gate, approximate=True) * up, w_down)


def setup_inputs(seed: int = 0) -> dict:
    key = jax.random.key(seed)
    ks = jax.random.split(key, 24)
    nrm = lambda k, shape, s: jax.random.normal(k, shape, jnp.float32) * s
    gain = lambda k, shape: 1.0 + 0.02 * jax.random.normal(k, shape, jnp.float32)
    dt0 = jnp.exp(jax.random.uniform(ks[8], (DEPTH, SSM_HEADS), jnp.float32,
                                     np.log(1e-3), np.log(1e-1)))
    return {
        'x': jax.random.normal(ks[0], (BATCH, SEQ, D_MODEL), jnp.float32),
        'w_in': nrm(ks[1], (DEPTH, D_MODEL, IN_TOTAL), D_MODEL ** -0.5),
        'mix_pre_norm': gain(ks[2], (DEPTH, D_MODEL)),
        'mix_post_norm': gain(ks[3], (DEPTH, D_MODEL)),
        'hg_lb_table': nrm(ks[4], (DEPTH + 1, HG_FDIM), 0.5),
        'hg_out_norm': gain(ks[5], (DEPTH, HG_DV)),
        'ssm_conv_w': nrm(ks[6], (DEPTH, SSM_CONV, SSM_CONV_DIM), SSM_CONV ** -0.5),
        'ssm_conv_b': nrm(ks[7], (DEPTH, SSM_CONV_DIM), 0.01),
        'ssm_dt_bias': dt0 + jnp.log(-jnp.expm1(-dt0)),
        'ssm_A_log': jnp.log(jax.random.uniform(ks[9], (DEPTH, SSM_HEADS), jnp.float32, 1.0, 16.0)),
        'ssm_D': gain(ks[10], (DEPTH, SSM_HEADS)),
        'ssm_out_norm': gain(ks[11], (DEPTH, SSM_DINNER)),
        'w_branch_hg': nrm(ks[12], (DEPTH, HG_VDIM, D_MODEL), HG_VDIM ** -0.5),
        'w_branch_ssm': nrm(ks[13], (DEPTH, SSM_DINNER, D_MODEL), SSM_DINNER ** -0.5),
        'w_out': nrm(ks[14], (DEPTH, D_MODEL, D_MODEL), D_MODEL ** -0.5),
        'ffn_pre_norm': gain(ks[15], (DEPTH, D_MODEL)),
        'ffn_post_norm': gain(ks[16], (DEPTH, D_MODEL)),
        'ffn_w_up': nrm(ks[17], (DEPTH, D_MODEL, 2 * D_FF), D_MODEL ** -0.5),
        'ffn_conv_w': nrm(ks[18], (DEPTH, FFN_CONV, D_FF), FFN_CONV ** -0.5),
        'ffn_conv_b': nrm(ks[19], (DEPTH, D_FF), 0.01),
        'ffn_w_down': nrm(ks[20], (DEPTH, D_FF, D_MODEL), D_FF ** -0.5),
    }


def reference(x, w_in, mix_pre_norm, mix_post_norm, hg_lb_table, hg_out_norm, ssm_conv_w, ssm_conv_b,
              ssm_dt_bias, ssm_A_log, ssm_D, ssm_out_norm, w_branch_hg, w_branch_ssm, w_out,
              ffn_pre_norm, ffn_post_norm, ffn_w_up, ffn_conv_w, ffn_conv_b, ffn_w_down):
    lower_bounds = jnp.cumsum(jax.nn.softmax(hg_lb_table.astype(jnp.float32), axis=0), axis=0)
    splits = []
    acc = 0
    for s in IN_SIZES[:-1]:
        acc += s
        splits.append(acc)
    for l in range(DEPTH):
        h = rmsnorm(x, mix_pre_norm[l])
        proj = jnp.einsum('btd,de->bte', h, w_in[l])
        q, f_raw, i, g, z, xbc, dt_raw, gate_hg, gate_ssm = jnp.split(proj, splits, axis=-1)
        y_hg = hgrn2_mixer(q, f_raw, i, g, lower_bounds[l], hg_out_norm[l])
        y_ssm = mamba2_mixer(z, xbc, dt_raw, ssm_conv_w[l], ssm_conv_b[l], ssm_dt_bias[l],
                             ssm_A_log[l], ssm_D[l], ssm_out_norm[l])
        mixed = (jax.nn.sigmoid(gate_hg) * jnp.einsum('btv,vd->btd', y_hg, w_branch_hg[l])
                 + jax.nn.sigmoid(gate_ssm) * jnp.einsum('bte,ed->btd', y_ssm, w_branch_ssm[l]))
        x = x + rmsnorm(jnp.einsum('btd,de->bte', mixed, w_out[l]), mix_post_norm[l])
        h = rmsnorm(x, ffn_pre_norm[l])
        x = x + rmsnorm(conv_ffn(h, ffn_w_up[l], ffn_conv_w[l], ffn_conv_b[l], ffn_w_down[l]), ffn_post_norm[l])
    return x
```

```python
import functools

import numpy as np
import jax
import jax.numpy as jnp
from jax import lax
from jax.experimental import pallas as pl
from jax.experimental.pallas import tpu as pltpu

F32 = jnp.float32
BF16 = jnp.bfloat16
EPS = 1e-6

HG_DK = 128
HG_DV = 128
SSM_HEADDIM = 64
SSM_GROUPS = 8
SSM_DSTATE = 128
SSM_CONV = 4
FFN_CONV = 3

HG_CHUNK = 128
HG_BLOCK = 1024
SSD_CHUNK = 256
CARRY = 8

VMEM_LIMIT = 48 * 1024 * 1024


def _params(sem):
    return pltpu.CompilerParams(dimension_semantics=sem, vmem_limit_bytes=VMEM_LIMIT)


def _dot(a, b):
    return jnp.dot(a, b, preferred_element_type=F32)


def _dot_nt(a, b):
    return lax.dot_general(a, b, (((1,), (1,)), ((), ())), preferred_element_type=F32)


def _dot_tn(a, b):
    return lax.dot_general(a, b, (((0,), (0,)), ((), ())), preferred_element_type=F32)


def _split_bf16(a):
    hi = a.astype(BF16)
    lo = (a - hi.astype(F32)).astype(BF16)
    return hi, lo


def _rms(x, w):
    return x * lax.rsqrt(jnp.mean(x * x, axis=-1, keepdims=True) + EPS) * w


def _norm_matmul_kernel(x_ref, nw_ref, w_ref, o_ref, h_ref):
    @pl.when(pl.program_id(1) == 0)
    def _():
        h_ref[...] = _rms(x_ref[...], nw_ref[...]).astype(BF16)

    o_ref[...] = _dot(h_ref[...], w_ref[...]).astype(o_ref.dtype)


def _norm_matmul(x, nw, w, out_dtype, tm, tn):
    t, d = x.shape
    n = w.shape[1]
    return pl.pallas_call(
        _norm_matmul_kernel,
        grid=(t // tm, n // tn),
        in_specs=[
            pl.BlockSpec((tm, d), lambda m, j: (m, 0)),
            pl.BlockSpec((1, d), lambda m, j: (0, 0)),
            pl.BlockSpec((d, tn), lambda m, j: (0, j)),
        ],
        out_specs=pl.BlockSpec((tm, tn), lambda m, j: (m, j)),
        out_shape=jax.ShapeDtypeStruct((t, n), out_dtype),
        scratch_shapes=[pltpu.VMEM((tm, d), BF16)],
        compiler_params=_params(("parallel", "arbitrary")),
    )(x, nw, w)


def _hg_constants(c):
    levels = []
    h = c // 2
    while h >= 1:
        levels.append(h)
        h //= 2
    t = np.arange(c)[:, None]
    u = np.arange(c)[None, :]
    mats = [(u <= t)]
    masks = []
    for h in levels:
        blk = 2 * h
        mid = (t // blk) * blk + h
        second = (t % blk) >= h
        mats.append(np.where(second, (u >= mid) & (u <= t), (u > t) & (u < mid)))
        s = u
        masks.append(((t // blk) == (s // blk)) & second & ((s % blk) < h))
    masks.append(t == u)
    pm = np.concatenate(mats, axis=0).astype(np.float32)
    mk = np.stack(masks, axis=0).astype(np.float32)
    return len(levels), pm, mk


def _hgrn2_kernel(q_ref, f_ref, i_ref, g_ref, lbt_ref, nw_ref, pm_ref, mk_ref, o_ref, st_ref,
                  *, chunk, nlev):
    c = chunk

    @pl.when(pl.program_id(1) == 0)
    def _():
        st_ref[...] = jnp.zeros_like(st_ref)

    tab = lbt_ref[...]
    te = jnp.exp(tab - jnp.max(tab, axis=0, keepdims=True))
    lb = te[0:1, :] / jnp.sum(te, axis=0, keepdims=True)
    nw = nw_ref[...]
    pm = pm_ref[...]
    scale = HG_DK ** -0.5

    def body(ci, carry):
        r0 = pl.multiple_of(ci * c, c)
        rows = pl.ds(r0, c)
        q_raw = q_ref[rows, :].astype(F32)
        f_raw = f_ref[rows, :].astype(F32)
        v = i_ref[rows, :]
        g_raw = g_ref[rows, :].astype(F32)

        f = lb + (1.0 - lb) * jax.nn.sigmoid(f_raw)
        kk = 1.0 - f
        lf = jnp.log(f)
        qh = q_raw * jax.nn.sigmoid(q_raw) * scale

        lf_hi, lf_lo = _split_bf16(lf)
        ex = _dot(pm, lf_hi) + _dot(pm, lf_lo)
        b = ex[0:c]

        st = st_ref[...]
        o = _dot_nt((qh * jnp.exp(b)).astype(BF16), st.astype(BF16))

        scores = mk_ref[nlev] * _dot_nt(qh.astype(BF16), kk.astype(BF16))
        for l in range(nlev):
            e = jnp.exp(ex[(l + 1) * c:(l + 2) * c])
            s = _dot_nt((qh * e).astype(BF16), (kk * e).astype(BF16))
            scores = scores + mk_ref[l] * s
        o = o + _dot(scores.astype(BF16), v)

        b_last = b[c - 1:c, :]
        kdec = kk * jnp.exp(b_last - b)
        st_ref[...] = st * jnp.exp(b_last) + _dot_tn(v, kdec.astype(BF16))

        y = _rms(o, nw) * (g_raw * jax.nn.sigmoid(g_raw))
        o_ref[rows, :] = y.astype(o_ref.dtype)
        return carry

    lax.fori_loop(0, q_ref.shape[0] // c, body, 0)


def _hgrn2(proj, lb_table, norm_w, n_heads, tb, chunk):
    t = proj.shape[0]
    nlev, pm, mk = _hg_constants(chunk)
    pm = jnp.asarray(pm, BF16)
    mk = jnp.asarray(mk, F32)
    col = lambda seg: (lambda h, i: (i, seg * n_heads + h))
    const2 = lambda h, i: (0, 0)
    return pl.pallas_call(
        functools.partial(_hgrn2_kernel, chunk=chunk, nlev=nlev),
        grid=(n_heads, t // tb),
        in_specs=[
            pl.BlockSpec((tb, HG_DK), col(0)),
            pl.BlockSpec((tb, HG_DK), col(1)),
            pl.BlockSpec((tb, HG_DV), col(2)),
            pl.BlockSpec((tb, HG_DV), col(3)),
            pl.BlockSpec((lb_table.shape[0], HG_DK), lambda h, i: (0, h)),
            pl.BlockSpec((1, HG_DV), const2),
            pl.BlockSpec(pm.shape, const2),
            pl.BlockSpec(mk.shape, lambda h, i: (0, 0, 0)),
        ],
        out_specs=pl.BlockSpec((tb, HG_DV), lambda h, i: (i, h)),
        out_shape=jax.ShapeDtypeStruct((t, n_heads * HG_DV), BF16),
        scratch_shapes=[pltpu.VMEM((HG_DV, HG_DK), F32)],
        compiler_params=_params(("parallel", "arbitrary")),
    )(proj, proj, proj, proj, lb_table, norm_w, pm, mk)


def _causal_conv(ext_ref, cur, w, b, ksize, first):
    q = cur.shape[0]

    @pl.when(first)
    def _():
        ext_ref[0:CARRY, :] = jnp.zeros((CARRY, cur.shape[1]), F32)

    ext_ref[CARRY:CARRY + q, :] = cur
    acc = cur * w[ksize - 1:ksize, :] + b
    for j in range(1, ksize):
        acc = acc + ext_ref[CARRY - j:CARRY - j + q, :] * w[ksize - 1 - j:ksize - j, :]
    ext_ref[0:CARRY, :] = ext_ref[q:q + CARRY, :]
    return acc


def _silu(a):
    return a * jax.nn.sigmoid(a)


def _ssd_kernel(z_ref, x_ref, b_ref, c_ref, dt_ref, wx_ref, wb_ref, wc_ref, bx_ref, bb_ref, bc_ref,
                dtb_ref, alog_ref, dskip_ref, nw_ref, triu_ref, o_ref,
                s_ref, ex_ref, eb_ref, ec_ref):
    q = x_ref.shape[0]
    hpg = dt_ref.shape[1]
    first = pl.program_id(1) == 0

    @pl.when(first)
    def _():
        s_ref[...] = jnp.zeros_like(s_ref)

    xs = _silu(_causal_conv(ex_ref, x_ref[...].astype(F32), wx_ref[...], bx_ref[...], SSM_CONV, first))
    bm = _silu(_causal_conv(eb_ref, b_ref[...].astype(F32), wb_ref[...], bb_ref[...], SSM_CONV, first))
    cm = _silu(_causal_conv(ec_ref, c_ref[...].astype(F32), wc_ref[...], bc_ref[...], SSM_CONV, first))
    bm16 = bm.astype(BF16)
    cm16 = cm.astype(BF16)

    dt_r = jax.nn.softplus(dt_ref[0] + dtb_ref[0])
    da_r = dt_r * (-jnp.exp(alog_ref[0]))
    da_hi, da_lo = _split_bf16(da_r)
    triu = triu_ref[...]
    acum_r = _dot(da_hi, triu) + _dot(da_lo, triu)
    pad = jnp.zeros((128 - 2 * hpg, q), F32)
    cols = jnp.concatenate([acum_r, dt_r, pad], axis=0).T

    a_last = cols[q - 1:q, :]
    ea = jnp.exp(cols)
    wd = jnp.exp(a_last - cols)
    sd = jnp.exp(a_last)

    cb = _dot_nt(cm16, bm16)
    cs = _dot(cm16, s_ref[...].astype(BF16))
    row = lax.broadcasted_iota(jnp.int32, (q, q), 0)
    colid = lax.broadcasted_iota(jnp.int32, (q, q), 1)
    causal = row >= colid
    lane = lax.broadcasted_iota(jnp.int32, (q, 128), 1)
    lo_half = lane < SSM_HEADDIM
    lane1 = lax.broadcasted_iota(jnp.int32, (1, 128), 1)
    lo_half1 = lane1 < SSM_HEADDIM

    def bc(a, j):
        return jnp.broadcast_to(a[:, j:j + 1], (a.shape[0], 128))

    y_parts = []
    xw_parts = []
    sd_parts = []
    for p in range(hpg // 2):
        h0, h1 = 2 * p, 2 * p + 1
        xp = xs[:, p * 128:(p + 1) * 128]
        xdt = xp * jnp.where(lo_half, bc(cols, hpg + h0), bc(cols, hpg + h1))
        xdt16 = xdt.astype(BF16)
        ys = []
        for h in (h0, h1):
            dm = cols[:, h:h + 1] - acum_r[h:h + 1, :]
            lm = jnp.where(causal, jnp.exp(jnp.minimum(dm, 0.0)), 0.0)
            ys.append(_dot((cb * lm).astype(BF16), xdt16))
        y = jnp.where(lo_half, ys[0], ys[1])
        y = y + cs[:, p * 128:(p + 1) * 128] * jnp.where(lo_half, bc(ea, h0), bc(ea, h1))
        y_parts.append(y)
        xw_parts.append(xdt * jnp.where(lo_half, bc(wd, h0), bc(wd, h1)))
        sd_parts.append(jnp.where(lo_half1, bc(sd, h0), bc(sd, h1)))

    y = jnp.concatenate(y_parts, axis=1)
    xw = jnp.concatenate(xw_parts, axis=1)
    sdec = jnp.concatenate(sd_parts, axis=1)
    s_ref[...] = s_ref[...] * sdec + _dot_tn(bm16, xw.astype(BF16))

    y = y + dskip_ref[0] * xs
    y = y * _silu(z_ref[...].astype(F32))
    o_ref[...] = _rms(y, nw_ref[0]).astype(o_ref.dtype)


def _ssd(proj, dt_hm, conv_w, conv_b, dt_bias, a_log, d_skip, norm_w, z_off, xbc_off, d_inner):
    t = proj.shape[0]
    q = SSD_CHUNK
    g = SSM_GROUPS
    n = SSM_DSTATE
    gw = d_inner // g
    hpg = gw // SSM_HEADDIM
    triu = jnp.asarray(np.triu(np.ones((q, q), np.float32)), BF16)
    k = conv_w.shape[0]
    c2 = lambda gi, i: (0, 0)
    return pl.pallas_call(
        _ssd_kernel,
        grid=(g, t // q),
        in_specs=[
            pl.BlockSpec((q, gw), lambda gi, i: (i, z_off // gw + gi)),
            pl.BlockSpec((q, gw), lambda gi, i: (i, xbc_off // gw + gi)),
            pl.BlockSpec((q, n), lambda gi, i: (i, (xbc_off + d_inner) // n + gi)),
            pl.BlockSpec((q, n), lambda gi, i: (i, (xbc_off + d_inner) // n + g + gi)),
            pl.BlockSpec((1, hpg, q), lambda gi, i: (gi, 0, i)),
            pl.BlockSpec((k, gw), lambda gi, i: (0, gi)),
            pl.BlockSpec((k, n), lambda gi, i: (0, d_inner // n + gi)),
            pl.BlockSpec((k, n), lambda gi, i: (0, d_inner // n + g + gi)),
            pl.BlockSpec((1, gw), lambda gi, i: (0, gi)),
            pl.BlockSpec((1, n), lambda gi, i: (0, d_inner // n + gi)),
            pl.BlockSpec((1, n), lambda gi, i: (0, d_inner // n + g + gi)),
            pl.BlockSpec((1, hpg, 1), lambda gi, i: (gi, 0, 0)),
            pl.BlockSpec((1, hpg, 1), lambda gi, i: (gi, 0, 0)),
            pl.BlockSpec((1, 1, gw), lambda gi, i: (gi, 0, 0)),
            pl.BlockSpec((1, 1, gw), lambda gi, i: (gi, 0, 0)),
            pl.BlockSpec((q, q), c2),
        ],
        out_specs=pl.BlockSpec((q, gw), lambda gi, i: (i, gi)),
        out_shape=jax.ShapeDtypeStruct((t, d_inner), BF16),
        scratch_shapes=[
            pltpu.VMEM((n, gw), F32),
            pltpu.VMEM((CARRY + q, gw), F32),
            pltpu.VMEM((CARRY + q, n), F32),
            pltpu.VMEM((CARRY + q, n), F32),
        ],
        compiler_params=_params(("parallel", "arbitrary")),
    )(proj, proj, proj, proj, dt_hm, conv_w, conv_w, conv_w, conv_b, conv_b, conv_b,
      dt_bias, a_log, d_skip, norm_w, triu)


def _merge_kernel(yh_ref, ys_ref, wh_ref, ws_ref, gh_ref, gs_ref, o_ref):
    a = _dot(yh_ref[...], wh_ref[...])
    b = _dot(ys_ref[...], ws_ref[...])
    o_ref[...] = (jax.nn.sigmoid(gh_ref[...]) * a + jax.nn.sigmoid(gs_ref[...]) * b).astype(o_ref.dtype)


def _merge(y_hg, y_ssm, w_hg, w_ssm, tail, tm, tn):
    t = y_hg.shape[0]
    d = w_hg.shape[1]
    return pl.pallas_call(
        _merge_kernel,
        grid=(t // tm, d // tn),
        in_specs=[
            pl.BlockSpec((tm, y_hg.shape[1]), lambda m, j: (m, 0)),
            pl.BlockSpec((tm, y_ssm.shape[1]), lambda m, j: (m, 0)),
            pl.BlockSpec((w_hg.shape[0], tn), lambda m, j: (0, j)),
            pl.BlockSpec((w_ssm.shape[0], tn), lambda m, j: (0, j)),
            pl.BlockSpec((tm, tn), lambda m, j: (m, j)),
            pl.BlockSpec((tm, tn), lambda m, j: (m, d // tn + j)),
        ],
        out_specs=pl.BlockSpec((tm, tn), lambda m, j: (m, j)),
        out_shape=jax.ShapeDtypeStruct((t, d), BF16),
        compiler_params=_params(("parallel", "arbitrary")),
    )(y_hg, y_ssm, w_hg, w_ssm, tail, tail)


def _out_kernel(a_ref, w_ref, x_ref, post_ref, pre_ref, x1_ref, h_ref):
    r = _dot(a_ref[...], w_ref[...])
    x1 = x_ref[...] + _rms(r, post_ref[...])
    x1_ref[...] = x1
    h_ref[...] = _rms(x1, pre_ref[...]).astype(BF16)


def _out_proj(mixed, w_out, x, post_w, pre_w, tm):
    t, d = x.shape
    c2 = lambda m: (0, 0)
    row = lambda m: (m, 0)
    return pl.pallas_call(
        _out_kernel,
        grid=(t // tm,),
        in_specs=[
            pl.BlockSpec((tm, d), row),
            pl.BlockSpec((d, d), c2),
            pl.BlockSpec((tm, d), row),
            pl.BlockSpec((1, d), c2),
            pl.BlockSpec((1, d), c2),
        ],
        out_specs=[pl.BlockSpec((tm, d), row), pl.BlockSpec((tm, d), row)],
        out_shape=[jax.ShapeDtypeStruct((t, d), F32), jax.ShapeDtypeStruct((t, d), BF16)],
        compiler_params=_params(("parallel",)),
    )(mixed, w_out, x, post_w, pre_w)


def _ffn_up_kernel(h_ref, wg_ref, wu_ref, cw_ref, cb_ref, o_ref, ext_ref):
    h = h_ref[...]
    gate = _dot(h, wg_ref[...])
    up = _dot(h, wu_ref[...])
    conv = _causal_conv(ext_ref, gate, cw_ref[...], cb_ref[...], FFN_CONV, pl.program_id(1) == 0)
    o_ref[...] = (jax.nn.gelu(conv, approximate=True) * up).astype(o_ref.dtype)


def _ffn_up(h, w_up, conv_w, conv_b, d_ff, tm, tn):
    t, d = h.shape
    k = conv_w.shape[0]
    return pl.pallas_call(
        _ffn_up_kernel,
        grid=(d_ff // tn, t // tm),
        in_specs=[
            pl.BlockSpec((tm, d), lambda j, m: (m, 0)),
            pl.BlockSpec((d, tn), lambda j, m: (0, j)),
            pl.BlockSpec((d, tn), lambda j, m: (0, d_ff // tn + j)),
            pl.BlockSpec((k, tn), lambda j, m: (0, j)),
            pl.BlockSpec((1, tn), lambda j, m: (0, j)),
        ],
        out_specs=pl.BlockSpec((tm, tn), lambda j, m: (m, j)),
        out_shape=jax.ShapeDtypeStruct((t, d_ff), BF16),
        scratch_shapes=[pltpu.VMEM((CARRY + tm, tn), F32)],
        compiler_params=_params(("parallel", "arbitrary")),
    )(h, w_up, w_up, conv_w, conv_b)


def _ffn_down_kernel(a_ref, w_ref, x_ref, post_ref, o_ref, acc_ref):
    kstep = pl.program_id(1)

    @pl.when(kstep == 0)
    def _():
        acc_ref[...] = jnp.zeros_like(acc_ref)

    acc_ref[...] += _dot(a_ref[...], w_ref[...])

    @pl.when(kstep == pl.num_programs(1) - 1)
    def _():
        o_ref[...] = x_ref[...] + _rms(acc_ref[...], post_ref[...])


def _ffn_down(act, w_down, x1, post_w, tm, tk):
    t, d = x1.shape
    d_ff = act.shape[1]
    return pl.pallas_call(
        _ffn_down_kernel,
        grid=(t // tm, d_ff // tk),
        in_specs=[
            pl.BlockSpec((tm, tk), lambda m, k: (m, k)),
            pl.BlockSpec((tk, d), lambda m, k: (k, 0)),
            pl.BlockSpec((tm, d), lambda m, k: (m, 0)),
            pl.BlockSpec((1, d), lambda m, k: (0, 0)),
        ],
        out_specs=pl.BlockSpec((tm, d), lambda m, k: (m, 0)),
        out_shape=jax.ShapeDtypeStruct((t, d), F32),
        scratch_shapes=[pltpu.VMEM((tm, d), F32)],
        compiler_params=_params(("parallel", "arbitrary")),
    )(act, w_down, x1, post_w)


def _layer(x, w_in, mix_pre, mix_post, lb_table, hg_norm, conv_w, conv_b, dt_bias, a_log, d_skip,
           ssm_norm, w_hg, w_ssm, w_out, ffn_pre, ffn_post, w_up, ffn_cw, ffn_cb, w_down):
    t, d = x.shape
    hg_v = w_hg.shape[0]
    hg_heads = hg_v // HG_DV
    d_inner = w_ssm.shape[0]
    ssm_heads = a_log.shape[0]
    hpg = ssm_heads // SSM_GROUPS
    gw = d_inner // SSM_GROUPS
    d_ff = w_down.shape[0]
    conv_dim = conv_w.shape[1]

    main_n = 4 * hg_v + d_inner + conv_dim
    dt_off = main_n
    gate_off = dt_off + ssm_heads
    z_off = 4 * hg_v
    xbc_off = z_off + d_inner

    w_main = w_in[:, :main_n].astype(BF16)
    tail_pad = (-(2 * d + ssm_heads)) % 128
    w_tail = jnp.concatenate(
        [w_in[:, gate_off:], w_in[:, dt_off:gate_off], jnp.zeros((d, tail_pad), w_in.dtype)], axis=1
    ).astype(BF16)

    nw = mix_pre.reshape(1, d)
    proj = _norm_matmul(x, nw, w_main, BF16, tm=1024, tn=1024)
    tail = _norm_matmul(x, nw, w_tail, F32, tm=1024, tn=w_tail.shape[1] // 3)

    y_hg = _hgrn2(proj, lb_table, hg_norm.reshape(1, HG_DV), hg_heads, HG_BLOCK, HG_CHUNK)

    dt_hm = tail[:, 2 * d:2 * d + ssm_heads].T.reshape(SSM_GROUPS, hpg, t)
    y_ssm = _ssd(
        proj, dt_hm, conv_w, conv_b.reshape(1, conv_dim),
        dt_bias.reshape(SSM_GROUPS, hpg, 1), a_log.reshape(SSM_GROUPS, hpg, 1),
        jnp.repeat(d_skip, SSM_HEADDIM).reshape(SSM_GROUPS, 1, gw),
        ssm_norm.reshape(SSM_GROUPS, 1, gw), z_off, xbc_off, d_inner)

    mixed = _merge(y_hg, y_ssm, w_hg.astype(BF16), w_ssm.astype(BF16), tail, tm=512, tn=512)
    x1, h2 = _out_proj(mixed, w_out.astype(BF16), x, mix_post.reshape(1, d), ffn_pre.reshape(1, d), tm=256)
    act = _ffn_up(h2, w_up.astype(BF16), ffn_cw, ffn_cb.reshape(1, d_ff), d_ff, tm=1024, tn=512)
    return _ffn_down(act, w_down.astype(BF16), x1, ffn_post.reshape(1, d), tm=512, tk=512)


def kernel(x, w_in, mix_pre_norm, mix_post_norm, hg_lb_table, hg_out_norm, ssm_conv_w, ssm_conv_b,
           ssm_dt_bias, ssm_A_log, ssm_D, ssm_out_norm, w_branch_hg, w_branch_ssm, w_out,
           ffn_pre_norm, ffn_post_norm, ffn_w_up, ffn_conv_w, ffn_conv_b, ffn_w_down):
    bsz, t, d = x.shape
    depth = w_in.shape[0]
    assert depth == 1 and hg_lb_table.shape[0] == 2, "forget-gate lower bound is computed for one layer"
    outs = []
    for b in range(bsz):
        xb = x[b]
        for l in range(depth):
            xb = _layer(xb, w_in[l], mix_pre_norm[l], mix_post_norm[l], hg_lb_table, hg_out_norm[l],
                        ssm_conv_w[l], ssm_conv_b[l], ssm_dt_bias[l], ssm_A_log[l], ssm_D[l],
                        ssm_out_norm[l], w_branch_hg[l], w_branch_ssm[l], w_out[l],
                        ffn_pre_norm[l], ffn_post_norm[l], ffn_w_up[l], ffn_conv_w[l], ffn_conv_b[l],
                        ffn_w_down[l])
        outs.append(xb)
    return jnp.stack(outs, axis=0)
```

```python
import functools

import numpy as np
import jax
import jax.numpy as jnp
from jax import lax
from jax.experimental import pallas as pl
from jax.experimental.pallas import tpu as pltpu

F32 = jnp.float32
BF16 = jnp.bfloat16
EPS = 1e-6

HG_DK = 128
HG_DV = 128
SSM_HEADDIM = 64
SSM_GROUPS = 8
SSM_DSTATE = 128
SSM_CONV = 4
FFN_CONV = 3

HG_CHUNK = 128
HG_BLOCK = 1024
HG_HEADS_PER_STEP = 4
SSD_CHUNK = 256
CARRY = 8

VMEM_LIMIT = 48 * 1024 * 1024


def _params(sem):
    return pltpu.CompilerParams(dimension_semantics=sem, vmem_limit_bytes=VMEM_LIMIT)


def _dot(a, b):
    return jnp.dot(a, b, preferred_element_type=F32)


def _dot_nt(a, b):
    return lax.dot_general(a, b, (((1,), (1,)), ((), ())), preferred_element_type=F32)


def _dot_tn(a, b):
    return lax.dot_general(a, b, (((0,), (0,)), ((), ())), preferred_element_type=F32)


def _split_bf16(a):
    hi = a.astype(BF16)
    lo = (a - hi.astype(F32)).astype(BF16)
    return hi, lo


def _sigmoid(a):
    return 0.5 * jnp.tanh(0.5 * a) + 0.5


def _silu(a):
    return a * _sigmoid(a)


def _rms(x, w):
    return x * lax.rsqrt(jnp.mean(x * x, axis=-1, keepdims=True) + EPS) * w


def _norm_tail_kernel(x_ref, nw_ref, w_ref, o_ref, h_ref):
    @pl.when(pl.program_id(1) == 0)
    def _():
        h_ref[...] = _rms(x_ref[...], nw_ref[...]).astype(BF16)

    o_ref[...] = _dot(h_ref[...], w_ref[...])


def _norm_tail(x, nw, w, tm, tn):
    t, d = x.shape
    n = w.shape[1]
    return pl.pallas_call(
        _norm_tail_kernel,
        grid=(t // tm, n // tn),
        in_specs=[
            pl.BlockSpec((tm, d), lambda m, j: (m, 0)),
            pl.BlockSpec((1, d), lambda m, j: (0, 0)),
            pl.BlockSpec((d, tn), lambda m, j: (0, j)),
        ],
        out_specs=[pl.BlockSpec((tm, tn), lambda m, j: (m, j)),
                   pl.BlockSpec((tm, d), lambda m, j: (m, 0))],
        out_shape=[jax.ShapeDtypeStruct((t, n), F32), jax.ShapeDtypeStruct((t, d), BF16)],
        compiler_params=_params(("parallel", "arbitrary")),
        name="inproj_tail",
    )(x, nw, w)


def _proj_main_kernel(h_ref, w_ref, o_ref, wb_ref):
    @pl.when(pl.program_id(1) == 0)
    def _():
        wb_ref[...] = w_ref[...].astype(BF16)

    o_ref[...] = _dot(h_ref[...], wb_ref[...]).astype(o_ref.dtype)


def _proj_main(h, w, n, tm, tn):
    t, d = h.shape
    return pl.pallas_call(
        _proj_main_kernel,
        grid=(n // tn, t // tm),
        in_specs=[
            pl.BlockSpec((tm, d), lambda j, m: (m, 0)),
            pl.BlockSpec((d, tn), lambda j, m: (0, j)),
        ],
        out_specs=pl.BlockSpec((tm, tn), lambda j, m: (m, j)),
        out_shape=jax.ShapeDtypeStruct((t, n), BF16),
        scratch_shapes=[pltpu.VMEM((d, tn), BF16)],
        compiler_params=_params(("parallel", "arbitrary")),
        name="inproj_main",
    )(h, w)


def _hg_constants(c):
    levels = []
    h = c // 2
    while h >= 1:
        levels.append(h)
        h //= 2
    t = np.arange(c)[:, None]
    u = np.arange(c)[None, :]
    mats = [(u <= t)]
    masks = []
    for h in levels:
        blk = 2 * h
        mid = (t // blk) * blk + h
        second = (t % blk) >= h
        mats.append(np.where(second, (u >= mid) & (u <= t), (u > t) & (u < mid)))
        s = u
        masks.append(((t // blk) == (s // blk)) & second & ((s % blk) < h))
    masks.append(t == u)
    pm = np.concatenate(mats, axis=0).astype(np.float32)
    mk = np.stack(masks, axis=0).astype(np.float32)
    return len(levels), pm, mk


def _hgrn2_kernel(q_ref, f_ref, i_ref, g_ref, lbt_ref, nw_ref, pm_ref, mk_ref, o_ref, st_ref,
                  *, chunk, nlev):
    c = chunk

    @pl.when(pl.program_id(1) == 0)
    def _():
        st_ref[...] = jnp.zeros_like(st_ref)

    tab = lbt_ref[...]
    te = jnp.exp(tab - jnp.max(tab, axis=0, keepdims=True))
    lb = te[0:1, :] / jnp.sum(te, axis=0, keepdims=True)
    nw = nw_ref[...]
    pm = pm_ref[...]
    scale = HG_DK ** -0.5
    nh = st_ref.shape[0]

    def body(ci, carry):
        r0 = pl.multiple_of(ci * c, c)
        rows = pl.ds(r0, c)
        q_raw = q_ref[rows, :].astype(F32)
        f_raw = f_ref[rows, :].astype(F32)
        v = i_ref[rows, :]
        g_raw = g_ref[rows, :].astype(F32)

        f = lb + (1.0 - lb) * _sigmoid(f_raw)
        kk = 1.0 - f
        lf = jnp.log(f)
        qh = _silu(q_raw) * scale
        gate = _silu(g_raw)

        lf_hi, lf_lo = _split_bf16(lf)
        ex = _dot(pm, lf_hi) + _dot(pm, lf_lo)
        b = ex[0:c]
        b_last = b[c - 1:c, :]
        qe = (qh * jnp.exp(b)).astype(BF16)
        kdec = (kk * jnp.exp(b_last - b)).astype(BF16)
        sdec = jnp.exp(b_last)
        qh16 = qh.astype(BF16)
        kk16 = kk.astype(BF16)

        for hh in range(nh):
            sl = slice(hh * HG_DK, (hh + 1) * HG_DK)
            st = st_ref[hh]
            o = _dot_nt(qe[:, sl], st.astype(BF16))
            scores = mk_ref[nlev] * _dot_nt(qh16[:, sl], kk16[:, sl])
            for l in range(nlev):
                e = jnp.exp(ex[(l + 1) * c:(l + 2) * c, sl])
                s = _dot_nt((qh[:, sl] * e).astype(BF16), (kk[:, sl] * e).astype(BF16))
                scores = scores + mk_ref[l] * s
            o = o + _dot(scores.astype(BF16), v[:, sl])
            st_ref[hh] = st * sdec[:, sl] + _dot_tn(v[:, sl], kdec[:, sl])
            o_ref[rows, sl] = (_rms(o, nw) * gate[:, sl]).astype(o_ref.dtype)
        return carry

    lax.fori_loop(0, q_ref.shape[0] // c, body, 0)


def _hgrn2(proj, lb_table, norm_w, n_heads, tb, chunk, hps):
    t = proj.shape[0]
    nlev, pm, mk = _hg_constants(chunk)
    pm = jnp.asarray(pm, BF16)
    mk = jnp.asarray(mk, F32)
    w = hps * HG_DK
    nblk = n_heads // hps
    col = lambda seg: (lambda h, i: (i, seg * nblk + h))
    const2 = lambda h, i: (0, 0)
    return pl.pallas_call(
        functools.partial(_hgrn2_kernel, chunk=chunk, nlev=nlev),
        grid=(nblk, t // tb),
        in_specs=[
            pl.BlockSpec((tb, w), col(0)),
            pl.BlockSpec((tb, w), col(1)),
            pl.BlockSpec((tb, w), col(2)),
            pl.BlockSpec((tb, w), col(3)),
            pl.BlockSpec((lb_table.shape[0], w), lambda h, i: (0, h)),
            pl.BlockSpec((1, HG_DV), const2),
            pl.BlockSpec(pm.shape, const2),
            pl.BlockSpec(mk.shape, lambda h, i: (0, 0, 0)),
        ],
        out_specs=pl.BlockSpec((tb, w), lambda h, i: (i, h)),
        out_shape=jax.ShapeDtypeStruct((t, n_heads * HG_DV), BF16),
        scratch_shapes=[pltpu.VMEM((hps, HG_DV, HG_DK), F32)],
        compiler_params=_params(("parallel", "arbitrary")),
        name="hgrn2_scan",
    )(proj, proj, proj, proj, lb_table, norm_w, pm, mk)


def _causal_conv(ext_ref, cur, w, b, ksize, first):
    q = cur.shape[0]

    @pl.when(first)
    def _():
        ext_ref[0:CARRY, :] = jnp.zeros((CARRY, cur.shape[1]), F32)

    ext_ref[CARRY:CARRY + q, :] = cur
    acc = cur * w[ksize - 1:ksize, :] + b
    for j in range(1, ksize):
        acc = acc + ext_ref[CARRY - j:CARRY - j + q, :] * w[ksize - 1 - j:ksize - j, :]
    ext_ref[0:CARRY, :] = ext_ref[q:q + CARRY, :]
    return acc


def _ssd_kernel(z_ref, x_ref, b_ref, c_ref, dt_ref, wx_ref, wb_ref, wc_ref, bx_ref, bb_ref, bc_ref,
                dtb_ref, alog_ref, dskip_ref, nw_ref, triu_ref, o_ref,
                s_ref, ex_ref, eb_ref, ec_ref):
    q = x_ref.shape[0]
    hpg = dt_ref.shape[1]
    first = pl.program_id(1) == 0

    @pl.when(first)
    def _():
        s_ref[...] = jnp.zeros_like(s_ref)

    xs = _silu(_causal_conv(ex_ref, x_ref[...].astype(F32), wx_ref[...], bx_ref[...], SSM_CONV, first))
    bm = _silu(_causal_conv(eb_ref, b_ref[...].astype(F32), wb_ref[...], bb_ref[...], SSM_CONV, first))
    cm = _silu(_causal_conv(ec_ref, c_ref[...].astype(F32), wc_ref[...], bc_ref[...], SSM_CONV, first))
    bm16 = bm.astype(BF16)
    cm16 = cm.astype(BF16)

    dt_r = jax.nn.softplus(dt_ref[0] + dtb_ref[0])
    da_r = dt_r * (-jnp.exp(alog_ref[0]))
    da_hi, da_lo = _split_bf16(da_r)
    triu = triu_ref[...]
    acum_r = _dot(da_hi, triu) + _dot(da_lo, triu)
    pad = jnp.zeros((128 - 2 * hpg, q), F32)
    cols = jnp.concatenate([acum_r, dt_r, pad], axis=0).T

    a_last = cols[q - 1:q, :]
    ea = jnp.exp(cols)
    wd = jnp.exp(a_last - cols)
    sd = jnp.exp(a_last)

    cb = _dot_nt(cm16, bm16)
    cs = _dot(cm16, s_ref[...].astype(BF16))
    row = lax.broadcasted_iota(jnp.int32, (q, q), 0)
    colid = lax.broadcasted_iota(jnp.int32, (q, q), 1)
    causal = row >= colid
    lane = lax.broadcasted_iota(jnp.int32, (q, 128), 1)
    lo_half = lane < SSM_HEADDIM
    lane1 = lax.broadcasted_iota(jnp.int32, (1, 128), 1)
    lo_half1 = lane1 < SSM_HEADDIM

    def bc(a, j):
        return jnp.broadcast_to(a[:, j:j + 1], (a.shape[0], 128))

    y_parts = []
    xw_parts = []
    sd_parts = []
    for p in range(hpg // 2):
        h0, h1 = 2 * p, 2 * p + 1
        xp = xs[:, p * 128:(p + 1) * 128]
        xdt = xp * jnp.where(lo_half, bc(cols, hpg + h0), bc(cols, hpg + h1))
        xdt16 = xdt.astype(BF16)
        ys = []
        for h in (h0, h1):
            dm = cols[:, h:h + 1] - acum_r[h:h + 1, :]
            lm = jnp.where(causal, jnp.exp(jnp.minimum(dm, 0.0)), 0.0)
            ys.append(_dot((cb * lm).astype(BF16), xdt16))
        y = jnp.where(lo_half, ys[0], ys[1])
        y = y + cs[:, p * 128:(p + 1) * 128] * jnp.where(lo_half, bc(ea, h0), bc(ea, h1))
        y_parts.append(y)
        xw_parts.append(xdt * jnp.where(lo_half, bc(wd, h0), bc(wd, h1)))
        sd_parts.append(jnp.where(lo_half1, bc(sd, h0), bc(sd, h1)))

    y = jnp.concatenate(y_parts, axis=1)
    xw = jnp.concatenate(xw_parts, axis=1)
    sdec = jnp.concatenate(sd_parts, axis=1)
    s_ref[...] = s_ref[...] * sdec + _dot_tn(bm16, xw.astype(BF16))

    y = y + dskip_ref[0] * xs
    y = y * _silu(z_ref[...].astype(F32))
    o_ref[...] = _rms(y, nw_ref[0]).astype(o_ref.dtype)


def _ssd(proj, dt_hm, conv_w, conv_b, dt_bias, a_log, d_skip, norm_w, z_off, xbc_off, d_inner):
    t = proj.shape[0]
    q = SSD_CHUNK
    g = SSM_GROUPS
    n = SSM_DSTATE
    gw = d_inner // g
    hpg = gw // SSM_HEADDIM
    triu = jnp.asarray(np.triu(np.ones((q, q), np.float32)), BF16)
    k = conv_w.shape[0]
    c2 = lambda gi, i: (0, 0)
    return pl.pallas_call(
        _ssd_kernel,
        grid=(g, t // q),
        in_specs=[
            pl.BlockSpec((q, gw), lambda gi, i: (i, z_off // gw + gi)),
            pl.BlockSpec((q, gw), lambda gi, i: (i, xbc_off // gw + gi)),
            pl.BlockSpec((q, n), lambda gi, i: (i, (xbc_off + d_inner) // n + gi)),
            pl.BlockSpec((q, n), lambda gi, i: (i, (xbc_off + d_inner) // n + g + gi)),
            pl.BlockSpec((1, hpg, q), lambda gi, i: (gi, 0, i)),
            pl.BlockSpec((k, gw), lambda gi, i: (0, gi)),
            pl.BlockSpec((k, n), lambda gi, i: (0, d_inner // n + gi)),
            pl.BlockSpec((k, n), lambda gi, i: (0, d_inner // n + g + gi)),
            pl.BlockSpec((1, gw), lambda gi, i: (0, gi)),
            pl.BlockSpec((1, n), lambda gi, i: (0, d_inner // n + gi)),
            pl.BlockSpec((1, n), lambda gi, i: (0, d_inner // n + g + gi)),
            pl.BlockSpec((1, hpg, 1), lambda gi, i: (gi, 0, 0)),
            pl.BlockSpec((1, hpg, 1), lambda gi, i: (gi, 0, 0)),
            pl.BlockSpec((1, 1, gw), lambda gi, i: (gi, 0, 0)),
            pl.BlockSpec((1, 1, gw), lambda gi, i: (gi, 0, 0)),
            pl.BlockSpec((q, q), c2),
        ],
        out_specs=pl.BlockSpec((q, gw), lambda gi, i: (i, gi)),
        out_shape=jax.ShapeDtypeStruct((t, d_inner), BF16),
        scratch_shapes=[
            pltpu.VMEM((n, gw), F32),
            pltpu.VMEM((CARRY + q, gw), F32),
            pltpu.VMEM((CARRY + q, n), F32),
            pltpu.VMEM((CARRY + q, n), F32),
        ],
        compiler_params=_params(("parallel", "arbitrary")),
        name="ssd_scan",
    )(proj, proj, proj, proj, dt_hm, conv_w, conv_w, conv_w, conv_b, conv_b, conv_b,
      dt_bias, a_log, d_skip, norm_w, triu)


def _merge_kernel(yh_ref, ys_ref, wh_ref, ws_ref, gh_ref, gs_ref, o_ref):
    a = _dot(yh_ref[...], wh_ref[...])
    b = _dot(ys_ref[...], ws_ref[...])
    o_ref[...] = (_sigmoid(gh_ref[...]) * a + _sigmoid(gs_ref[...]) * b).astype(o_ref.dtype)


def _merge(y_hg, y_ssm, w_hg, w_ssm, tail, tm, tn):
    t = y_hg.shape[0]
    d = w_hg.shape[1]
    return pl.pallas_call(
        _merge_kernel,
        grid=(t // tm, d // tn),
        in_specs=[
            pl.BlockSpec((tm, y_hg.shape[1]), lambda m, j: (m, 0)),
            pl.BlockSpec((tm, y_ssm.shape[1]), lambda m, j: (m, 0)),
            pl.BlockSpec((w_hg.shape[0], tn), lambda m, j: (0, j)),
            pl.BlockSpec((w_ssm.shape[0], tn), lambda m, j: (0, j)),
            pl.BlockSpec((tm, tn), lambda m, j: (m, j)),
            pl.BlockSpec((tm, tn), lambda m, j: (m, d // tn + j)),
        ],
        out_specs=pl.BlockSpec((tm, tn), lambda m, j: (m, j)),
        out_shape=jax.ShapeDtypeStruct((t, d), BF16),
        compiler_params=_params(("parallel", "arbitrary")),
        name="branch_merge",
    )(y_hg, y_ssm, w_hg, w_ssm, tail, tail)


def _out_kernel(a_ref, w_ref, x_ref, post_ref, pre_ref, x1_ref, h_ref):
    r = _dot(a_ref[...], w_ref[...])
    x1 = x_ref[...] + _rms(r, post_ref[...])
    x1_ref[...] = x1
    h_ref[...] = _rms(x1, pre_ref[...]).astype(BF16)


def _out_proj(mixed, w_out, x, post_w, pre_w, tm):
    t, d = x.shape
    c2 = lambda m: (0, 0)
    row = lambda m: (m, 0)
    return pl.pallas_call(
        _out_kernel,
        grid=(t // tm,),
        in_specs=[
            pl.BlockSpec((tm, d), row),
            pl.BlockSpec((d, d), c2),
            pl.BlockSpec((tm, d), row),
            pl.BlockSpec((1, d), c2),
            pl.BlockSpec((1, d), c2),
        ],
        out_specs=[pl.BlockSpec((tm, d), row), pl.BlockSpec((tm, d), row)],
        out_shape=[jax.ShapeDtypeStruct((t, d), F32), jax.ShapeDtypeStruct((t, d), BF16)],
        compiler_params=_params(("parallel",)),
        name="out_proj",
    )(mixed, w_out, x, post_w, pre_w)


def _ffn_up_kernel(h_ref, wg_ref, wu_ref, cw_ref, cb_ref, o_ref, ext_ref, wgb_ref, wub_ref):
    @pl.when(pl.program_id(1) == 0)
    def _():
        wgb_ref[...] = wg_ref[...].astype(BF16)
        wub_ref[...] = wu_ref[...].astype(BF16)

    h = h_ref[...]
    gate = _dot(h, wgb_ref[...])
    up = _dot(h, wub_ref[...])
    conv = _causal_conv(ext_ref, gate, cw_ref[...], cb_ref[...], FFN_CONV, pl.program_id(1) == 0)
    o_ref[...] = (jax.nn.gelu(conv, approximate=True) * up).astype(o_ref.dtype)


def _ffn_up(h, w_up, conv_w, conv_b, d_ff, tm, tn):
    t, d = h.shape
    k = conv_w.shape[0]
    return pl.pallas_call(
        _ffn_up_kernel,
        grid=(d_ff // tn, t // tm),
        in_specs=[
            pl.BlockSpec((tm, d), lambda j, m: (m, 0)),
            pl.BlockSpec((d, tn), lambda j, m: (0, j)),
            pl.BlockSpec((d, tn), lambda j, m: (0, d_ff // tn + j)),
            pl.BlockSpec((k, tn), lambda j, m: (0, j)),
            pl.BlockSpec((1, tn), lambda j, m: (0, j)),
        ],
        out_specs=pl.BlockSpec((tm, tn), lambda j, m: (m, j)),
        out_shape=jax.ShapeDtypeStruct((t, d_ff), BF16),
        scratch_shapes=[pltpu.VMEM((CARRY + tm, tn), F32),
                        pltpu.VMEM((d, tn), BF16), pltpu.VMEM((d, tn), BF16)],
        compiler_params=_params(("parallel", "arbitrary")),
        name="ffn_up",
    )(h, w_up, w_up, conv_w, conv_b)


def _ffn_down_kernel(a_ref, w_ref, x_ref, post_ref, o_ref, acc_ref):
    kstep = pl.program_id(1)

    @pl.when(kstep == 0)
    def _():
        acc_ref[...] = jnp.zeros_like(acc_ref)

    acc_ref[...] += _dot(a_ref[...], w_ref[...])

    @pl.when(kstep == pl.num_programs(1) - 1)
    def _():
        o_ref[...] = x_ref[...] + _rms(acc_ref[...], post_ref[...])


def _ffn_down(act, w_down, x1, post_w, tm, tk):
    t, d = x1.shape
    d_ff = act.shape[1]
    return pl.pallas_call(
        _ffn_down_kernel,
        grid=(t // tm, d_ff // tk),
        in_specs=[
            pl.BlockSpec((tm, tk), lambda m, k: (m, k)),
            pl.BlockSpec((tk, d), lambda m, k: (k, 0)),
            pl.BlockSpec((tm, d), lambda m, k: (m, 0)),
            pl.BlockSpec((1, d), lambda m, k: (0, 0)),
        ],
        out_specs=pl.BlockSpec((tm, d), lambda m, k: (m, 0)),
        out_shape=jax.ShapeDtypeStruct((t, d), F32),
        scratch_shapes=[pltpu.VMEM((tm, d), F32)],
        compiler_params=_params(("parallel", "arbitrary")),
        name="ffn_down",
    )(act, w_down, x1, post_w)


def _layer(x, w_in, mix_pre, mix_post, lb_table, hg_norm, conv_w, conv_b, dt_bias, a_log, d_skip,
           ssm_norm, w_hg, w_ssm, w_out, ffn_pre, ffn_post, w_up, ffn_cw, ffn_cb, w_down):
    t, d = x.shape
    hg_v = w_hg.shape[0]
    hg_heads = hg_v // HG_DV
    d_inner = w_ssm.shape[0]
    ssm_heads = a_log.shape[0]
    hpg = ssm_heads // SSM_GROUPS
    gw = d_inner // SSM_GROUPS
    d_ff = w_down.shape[0]
    conv_dim = conv_w.shape[1]

    main_n = 4 * hg_v + d_inner + conv_dim
    dt_off = main_n
    gate_off = dt_off + ssm_heads
    z_off = 4 * hg_v
    xbc_off = z_off + d_inner

    tail_pad = (-(2 * d + ssm_heads)) % 128
    w_tail = jnp.concatenate(
        [w_in[:, gate_off:].astype(BF16), w_in[:, dt_off:gate_off].astype(BF16),
         jnp.zeros((d, tail_pad), BF16)], axis=1)

    tail, h = _norm_tail(x, mix_pre.reshape(1, d), w_tail, tm=512, tn=w_tail.shape[1] // 3)
    proj = _proj_main(h, w_in, main_n, tm=1024, tn=1024)

    y_hg = _hgrn2(proj, lb_table, hg_norm.reshape(1, HG_DV), hg_heads, HG_BLOCK, HG_CHUNK, HG_HEADS_PER_STEP)

    dt_hm = tail[:, 2 * d:2 * d + ssm_heads].T.reshape(SSM_GROUPS, hpg, t)
    y_ssm = _ssd(
        proj, dt_hm, conv_w, conv_b.reshape(1, conv_dim),
        dt_bias.reshape(SSM_GROUPS, hpg, 1), a_log.reshape(SSM_GROUPS, hpg, 1),
        jnp.repeat(d_skip, SSM_HEADDIM).reshape(SSM_GROUPS, 1, gw),
        ssm_norm.reshape(SSM_GROUPS, 1, gw), z_off, xbc_off, d_inner)

    mixed = _merge(y_hg, y_ssm, w_hg.astype(BF16), w_ssm.astype(BF16), tail, tm=512, tn=512)
    x1, h2 = _out_proj(mixed, w_out.astype(BF16), x, mix_post.reshape(1, d), ffn_pre.reshape(1, d), tm=256)
    act = _ffn_up(h2, w_up, ffn_cw, ffn_cb.reshape(1, d_ff), d_ff, tm=1024, tn=512)
    return _ffn_down(act, w_down.astype(BF16), x1, ffn_post.reshape(1, d), tm=512, tk=512)


def kernel(x, w_in, mix_pre_norm, mix_post_norm, hg_lb_table, hg_out_norm, ssm_conv_w, ssm_conv_b,
           ssm_dt_bias, ssm_A_log, ssm_D, ssm_out_norm, w_branch_hg, w_branch_ssm, w_out,
           ffn_pre_norm, ffn_post_norm, ffn_w_up, ffn_conv_w, ffn_conv_b, ffn_w_down):
    bsz, t, d = x.shape
    depth = w_in.shape[0]
    assert depth == 1 and hg_lb_table.shape[0] == 2, "forget-gate lower bound is computed for one layer"
    outs = []
    for b in range(bsz):
        xb = x[b]
        for l in range(depth):
            xb = _layer(xb, w_in[l], mix_pre_norm[l], mix_post_norm[l], hg_lb_table, hg_out_norm[l],
                        ssm_conv_w[l], ssm_conv_b[l], ssm_dt_bias[l], ssm_A_log[l], ssm_D[l],
                        ssm_out_norm[l], w_branch_hg[l], w_branch_ssm[l], w_out[l],
                        ffn_pre_norm[l], ffn_post_norm[l], ffn_w_up[l], ffn_conv_w[l], ffn_conv_b[l],
                        ffn_w_down[l])
        outs.append(xb)
    return jnp.stack(outs, axis=0)
```

```python
import functools

import numpy as np
import jax
import jax.numpy as jnp
from jax import lax
from jax.experimental import pallas as pl
from jax.experimental.pallas import tpu as pltpu

F32 = jnp.float32
BF16 = jnp.bfloat16
EPS = 1e-6

HG_DK = 128
HG_DV = 128
SSM_HEADDIM = 64
SSM_GROUPS = 8
SSM_DSTATE = 128
SSM_CONV = 4
FFN_CONV = 3

INPROJ_TILE = 1024
INPROJ_ROWS = 1024
HG_CHUNK = 128
HG_BLOCK = 1024
HG_HEADS_PER_STEP = 4
SSD_CHUNK = 256
CARRY = 8

VMEM_LIMIT = 48 * 1024 * 1024


def _params(sem):
    return pltpu.CompilerParams(dimension_semantics=sem, vmem_limit_bytes=VMEM_LIMIT)


def _dot(a, b):
    return jnp.dot(a, b, preferred_element_type=F32)


def _dot_nt(a, b):
    return lax.dot_general(a, b, (((1,), (1,)), ((), ())), preferred_element_type=F32)


def _dot_tn(a, b):
    return lax.dot_general(a, b, (((0,), (0,)), ((), ())), preferred_element_type=F32)


def _split_bf16(a):
    hi = a.astype(BF16)
    lo = (a - hi.astype(F32)).astype(BF16)
    return hi, lo


def _sigmoid(a):
    return 0.5 * jnp.tanh(0.5 * a) + 0.5


def _silu(a):
    return a * _sigmoid(a)


def _rms(x, w):
    return x * lax.rsqrt(jnp.mean(x * x, axis=-1, keepdims=True) + EPS) * w


def _norm_dt_kernel(x_ref, nw_ref, wdt_ref, h_ref, dt_ref):
    h = _rms(x_ref[...], nw_ref[...]).astype(BF16)
    h_ref[...] = h
    dt_ref[...] = _dot_nt(wdt_ref[...].astype(BF16), h)


def _norm_dt(x, nw, wdt_t, tm):
    t, d = x.shape
    nh = wdt_t.shape[0]
    return pl.pallas_call(
        _norm_dt_kernel,
        grid=(t // tm,),
        in_specs=[
            pl.BlockSpec((tm, d), lambda m: (m, 0)),
            pl.BlockSpec((1, d), lambda m: (0, 0)),
            pl.BlockSpec((nh, d), lambda m: (0, 0)),
        ],
        out_specs=[pl.BlockSpec((tm, d), lambda m: (m, 0)),
                   pl.BlockSpec((nh, tm), lambda m: (0, m))],
        out_shape=[jax.ShapeDtypeStruct((t, d), BF16), jax.ShapeDtypeStruct((nh, t), F32)],
        compiler_params=_params(("parallel",)),
        name="norm_dt",
    )(x, nw, wdt_t)


def _proj_nt_kernel(h_ref, w_ref, o_ref, wb_ref):
    @pl.when(pl.program_id(1) == 0)
    def _():
        wb_ref[...] = w_ref[...].astype(BF16)

    o_ref[...] = _dot_nt(h_ref[...], wb_ref[...]).astype(o_ref.dtype)


def _proj_nt(h, w_t, row_off, n, tm, tn, name):
    t, d = h.shape
    if row_off % tn == 0:
        w_spec = pl.BlockSpec((tn, d), lambda j, m: (row_off // tn + j, 0))
    else:
        assert row_off % 8 == 0 and tn % 8 == 0
        w_spec = pl.BlockSpec((pl.Element(tn), pl.Element(d)),
                              lambda j, m: (pl.multiple_of(row_off + j * tn, 8), 0))
    return pl.pallas_call(
        _proj_nt_kernel,
        grid=(n // tn, t // tm),
        in_specs=[pl.BlockSpec((tm, d), lambda j, m: (m, 0)), w_spec],
        out_specs=pl.BlockSpec((tm, tn), lambda j, m: (m, j)),
        out_shape=jax.ShapeDtypeStruct((t, n), BF16),
        scratch_shapes=[pltpu.VMEM((tn, d), BF16)],
        compiler_params=_params(("parallel", "arbitrary")),
        name=name,
    )(h, w_t)


def _inproj_main_kernel(h_ref, w_ref, lbt_ref, cw_ref, cb_ref, o_ref, wb_ref, ext_ref, *, bounds):
    j = pl.program_id(0)
    first = pl.program_id(1) == 0
    q_end, f_end, i_end, z_end = bounds

    @pl.when(first)
    def _():
        wb_ref[...] = w_ref[...].astype(BF16)

    acc = _dot_nt(h_ref[...], wb_ref[...])

    @pl.when(j < q_end)
    def _():
        o_ref[...] = (_silu(acc) * HG_DK ** -0.5).astype(o_ref.dtype)

    @pl.when((j >= q_end) & (j < f_end))
    def _():
        tab = lbt_ref[...]
        te = jnp.exp(tab - jnp.max(tab, axis=0, keepdims=True))
        lb = te[0:1, :] / jnp.sum(te, axis=0, keepdims=True)
        o_ref[...] = jnp.log(lb + (1.0 - lb) * _sigmoid(acc)).astype(o_ref.dtype)

    @pl.when((j >= f_end) & (j < i_end))
    def _():
        o_ref[...] = acc.astype(o_ref.dtype)

    @pl.when((j >= i_end) & (j < z_end))
    def _():
        o_ref[...] = _silu(acc).astype(o_ref.dtype)

    @pl.when(j >= z_end)
    def _():
        conv = _causal_conv(ext_ref, acc, cw_ref[...], cb_ref[...], SSM_CONV, first)
        o_ref[...] = _silu(conv).astype(o_ref.dtype)


def _inproj_main(h, w_t, lb_table, conv_w, conv_b, n, bounds, tm, tn):
    t, d = h.shape
    q_end, f_end, i_end, z_end = bounds
    nconv = n // tn - z_end
    clamp = lambda v, hi: jnp.clip(v, 0, hi)
    return pl.pallas_call(
        functools.partial(_inproj_main_kernel, bounds=bounds),
        grid=(n // tn, t // tm),
        in_specs=[
            pl.BlockSpec((tm, d), lambda j, m: (m, 0)),
            pl.BlockSpec((tn, d), lambda j, m: (j, 0)),
            pl.BlockSpec((lb_table.shape[0], tn), lambda j, m: (0, clamp(j - q_end, f_end - q_end - 1))),
            pl.BlockSpec((conv_w.shape[0], tn), lambda j, m: (0, clamp(j - z_end, nconv - 1))),
            pl.BlockSpec((1, tn), lambda j, m: (0, clamp(j - z_end, nconv - 1))),
        ],
        out_specs=pl.BlockSpec((tm, tn), lambda j, m: (m, j)),
        out_shape=jax.ShapeDtypeStruct((t, n), BF16),
        scratch_shapes=[pltpu.VMEM((tn, d), BF16), pltpu.VMEM((CARRY + tm, tn), F32)],
        compiler_params=_params(("parallel", "arbitrary")),
        name="inproj_main",
    )(h, w_t, lb_table, conv_w, conv_b)


def _hg_constants(c):
    levels = []
    h = c // 2
    while h >= 1:
        levels.append(h)
        h //= 2
    t = np.arange(c)[:, None]
    u = np.arange(c)[None, :]
    mats = [(u <= t)]
    masks = []
    for h in levels:
        blk = 2 * h
        mid = (t // blk) * blk + h
        second = (t % blk) >= h
        mats.append(np.where(second, (u >= mid) & (u <= t), (u > t) & (u < mid)))
        s = u
        masks.append(((t // blk) == (s // blk)) & second & ((s % blk) < h))
    masks.append(t == u)
    pm = np.concatenate(mats, axis=0).astype(np.float32)
    mk = np.stack(masks, axis=0).astype(np.float32)
    return len(levels), pm, mk


def _hgrn2_kernel(q_ref, lf_ref, i_ref, g_ref, nw_ref, pm_ref, mk_ref, o_ref, st_ref, *, chunk, nlev):
    c = chunk

    @pl.when(pl.program_id(1) == 0)
    def _():
        st_ref[...] = jnp.zeros_like(st_ref)

    nw = nw_ref[...]
    pm = pm_ref[...]
    nh = st_ref.shape[0]

    def body(ci, carry):
        r0 = pl.multiple_of(ci * c, c)
        rows = pl.ds(r0, c)
        qh16 = q_ref[rows, :]
        lf16 = lf_ref[rows, :]
        v = i_ref[rows, :]
        gate = g_ref[rows, :].astype(F32)

        kk16 = (1.0 - jnp.exp(lf16.astype(F32))).astype(BF16)
        ex = _dot(pm, lf16)
        b = ex[0:c]
        b_last = b[c - 1:c, :]
        qe = qh16 * jnp.exp(b).astype(BF16)
        kdec = kk16 * jnp.exp(b_last - b).astype(BF16)
        sdec = jnp.exp(b_last)

        for hh in range(nh):
            sl = slice(hh * HG_DK, (hh + 1) * HG_DK)
            st = st_ref[hh]
            o = _dot_nt(qe[:, sl], st.astype(BF16))
            scores = mk_ref[nlev] * _dot_nt(qh16[:, sl], kk16[:, sl])
            for l in range(nlev):
                e = jnp.exp(ex[(l + 1) * c:(l + 2) * c, sl]).astype(BF16)
                s = _dot_nt(qh16[:, sl] * e, kk16[:, sl] * e)
                scores = scores + mk_ref[l] * s
            o = o + _dot(scores.astype(BF16), v[:, sl])
            st_ref[hh] = st * sdec[:, sl] + _dot_tn(v[:, sl], kdec[:, sl])
            o_ref[rows, sl] = (_rms(o, nw) * gate[:, sl]).astype(o_ref.dtype)
        return carry

    lax.fori_loop(0, q_ref.shape[0] // c, body, 0)


def _hgrn2(proj, norm_w, n_heads, tb, chunk, hps):
    t = proj.shape[0]
    nlev, pm, mk = _hg_constants(chunk)
    pm = jnp.asarray(pm, BF16)
    mk = jnp.asarray(mk, F32)
    w = hps * HG_DK
    nblk = n_heads // hps
    col = lambda seg: (lambda h, i: (i, seg * nblk + h))
    const2 = lambda h, i: (0, 0)
    return pl.pallas_call(
        functools.partial(_hgrn2_kernel, chunk=chunk, nlev=nlev),
        grid=(nblk, t // tb),
        in_specs=[
            pl.BlockSpec((tb, w), col(0)),
            pl.BlockSpec((tb, w), col(1)),
            pl.BlockSpec((tb, w), col(2)),
            pl.BlockSpec((tb, w), col(3)),
            pl.BlockSpec((1, HG_DV), const2),
            pl.BlockSpec(pm.shape, const2),
            pl.BlockSpec(mk.shape, lambda h, i: (0, 0, 0)),
        ],
        out_specs=pl.BlockSpec((tb, w), lambda h, i: (i, h)),
        out_shape=jax.ShapeDtypeStruct((t, n_heads * HG_DV), BF16),
        scratch_shapes=[pltpu.VMEM((hps, HG_DV, HG_DK), F32)],
        compiler_params=_params(("parallel", "arbitrary")),
        name="hgrn2_scan",
    )(proj, proj, proj, proj, norm_w, pm, mk)


def _causal_conv(ext_ref, cur, w, b, ksize, first):
    q = cur.shape[0]

    @pl.when(first)
    def _():
        ext_ref[0:CARRY, :] = jnp.zeros((CARRY, cur.shape[1]), F32)

    ext_ref[CARRY:CARRY + q, :] = cur
    acc = cur * w[ksize - 1:ksize, :] + b
    for j in range(1, ksize):
        acc = acc + ext_ref[CARRY - j:CARRY - j + q, :] * w[ksize - 1 - j:ksize - j, :]
    ext_ref[0:CARRY, :] = ext_ref[q:q + CARRY, :]
    return acc


def _ssd_expand_mats(hpg):
    ea = np.zeros((128, hpg * SSM_HEADDIM), np.float32)
    ed = np.zeros((128, hpg * SSM_HEADDIM), np.float32)
    for h in range(hpg):
        ea[h, h * SSM_HEADDIM:(h + 1) * SSM_HEADDIM] = 1.0
        ed[hpg + h, h * SSM_HEADDIM:(h + 1) * SSM_HEADDIM] = 1.0
    return ea, ed


def _ssd_kernel(z_ref, x_ref, b_ref, c_ref, dt_ref, dtb_ref, alog_ref, dskip_ref, nw_ref,
                triu_ref, ea_ref, ed_ref, o_ref, s_ref):
    q = x_ref.shape[0]
    hpg = dt_ref.shape[1]

    @pl.when(pl.program_id(1) == 0)
    def _():
        s_ref[...] = jnp.zeros_like(s_ref)

    xs = x_ref[...].astype(F32)
    bm16 = b_ref[...]
    cm16 = c_ref[...]

    dt_r = jax.nn.softplus(dt_ref[0] + dtb_ref[0])
    da_r = dt_r * (-jnp.exp(alog_ref[0]))
    da_hi, da_lo = _split_bf16(da_r)
    triu = triu_ref[...]
    acum_r = _dot(da_hi, triu) + _dot(da_lo, triu)
    pad = jnp.zeros((128 - 2 * hpg, q), F32)
    cols = jnp.concatenate([acum_r, dt_r, pad], axis=0).T

    lane = lax.broadcasted_iota(jnp.int32, (q, 128), 1)
    acol = jnp.where(lane < hpg, cols, 0.0)
    a_last = acol[q - 1:q, :]
    dt_x = _dot(cols.astype(BF16), ed_ref[...])
    ea_x = _dot(jnp.exp(acol).astype(BF16), ea_ref[...])
    wd_x = _dot(jnp.exp(a_last - acol).astype(BF16), ea_ref[...])
    sd_x = _dot(jnp.broadcast_to(jnp.exp(a_last), (8, 128)).astype(BF16), ea_ref[...])[0:1, :]

    xdt = xs * dt_x
    xdt16 = xdt.astype(BF16)
    cb16 = _dot_nt(cm16, bm16).astype(BF16)
    cs = _dot(cm16, s_ref[...].astype(BF16))
    row = lax.broadcasted_iota(jnp.int32, (q, q), 0)
    colid = lax.broadcasted_iota(jnp.int32, (q, q), 1)
    causal = row >= colid
    lo_half = lane < SSM_HEADDIM

    y_parts = []
    for p in range(hpg // 2):
        ys = []
        for h in (2 * p, 2 * p + 1):
            dm = cols[:, h:h + 1] - acum_r[h:h + 1, :]
            lm16 = jnp.exp(jnp.where(causal, dm, -1e30)).astype(BF16)
            ys.append(_dot(cb16 * lm16, xdt16[:, p * 128:(p + 1) * 128]))
        y_parts.append(jnp.where(lo_half, ys[0], ys[1]))

    y = jnp.concatenate(y_parts, axis=1) + cs * ea_x
    s_ref[...] = s_ref[...] * sd_x + _dot_tn(bm16, (xdt * wd_x).astype(BF16))

    y = y + dskip_ref[0] * xs
    y = y * z_ref[...].astype(F32)
    o_ref[...] = _rms(y, nw_ref[0]).astype(o_ref.dtype)


def _ssd(proj, dt_hm, dt_bias, a_log, d_skip, norm_w, z_off, xbc_off, d_inner):
    t = proj.shape[0]
    q = SSD_CHUNK
    g = SSM_GROUPS
    n = SSM_DSTATE
    gw = d_inner // g
    hpg = gw // SSM_HEADDIM
    triu = jnp.asarray(np.triu(np.ones((q, q), np.float32)), BF16)
    ea, ed = (jnp.asarray(m, BF16) for m in _ssd_expand_mats(hpg))
    c2 = lambda gi, i: (0, 0)
    return pl.pallas_call(
        _ssd_kernel,
        grid=(g, t // q),
        in_specs=[
            pl.BlockSpec((q, gw), lambda gi, i: (i, z_off // gw + gi)),
            pl.BlockSpec((q, gw), lambda gi, i: (i, xbc_off // gw + gi)),
            pl.BlockSpec((q, n), lambda gi, i: (i, (xbc_off + d_inner) // n + gi)),
            pl.BlockSpec((q, n), lambda gi, i: (i, (xbc_off + d_inner) // n + g + gi)),
            pl.BlockSpec((1, hpg, q), lambda gi, i: (gi, 0, i)),
            pl.BlockSpec((1, hpg, 1), lambda gi, i: (gi, 0, 0)),
            pl.BlockSpec((1, hpg, 1), lambda gi, i: (gi, 0, 0)),
            pl.BlockSpec((1, 1, gw), lambda gi, i: (gi, 0, 0)),
            pl.BlockSpec((1, 1, gw), lambda gi, i: (gi, 0, 0)),
            pl.BlockSpec((q, q), c2),
            pl.BlockSpec(ea.shape, c2),
            pl.BlockSpec(ed.shape, c2),
        ],
        out_specs=pl.BlockSpec((q, gw), lambda gi, i: (i, gi)),
        out_shape=jax.ShapeDtypeStruct((t, d_inner), BF16),
        scratch_shapes=[pltpu.VMEM((n, gw), F32)],
        compiler_params=_params(("parallel", "arbitrary")),
        name="ssd_scan",
    )(proj, proj, proj, proj, dt_hm, dt_bias, a_log, d_skip, norm_w, triu, ea, ed)


def _merge_kernel(yh_ref, ys_ref, wh_ref, ws_ref, gh_ref, gs_ref, o_ref):
    a = _dot(yh_ref[...], wh_ref[...])
    b = _dot(ys_ref[...], ws_ref[...])
    gh = _sigmoid(gh_ref[...].astype(F32))
    gs = _sigmoid(gs_ref[...].astype(F32))
    o_ref[...] = (gh * a + gs * b).astype(o_ref.dtype)


def _merge(y_hg, y_ssm, w_hg, w_ssm, gates, tm, tn):
    t = y_hg.shape[0]
    d = w_hg.shape[1]
    return pl.pallas_call(
        _merge_kernel,
        grid=(t // tm, d // tn),
        in_specs=[
            pl.BlockSpec((tm, y_hg.shape[1]), lambda m, j: (m, 0)),
            pl.BlockSpec((tm, y_ssm.shape[1]), lambda m, j: (m, 0)),
            pl.BlockSpec((w_hg.shape[0], tn), lambda m, j: (0, j)),
            pl.BlockSpec((w_ssm.shape[0], tn), lambda m, j: (0, j)),
            pl.BlockSpec((tm, tn), lambda m, j: (m, j)),
            pl.BlockSpec((tm, tn), lambda m, j: (m, d // tn + j)),
        ],
        out_specs=pl.BlockSpec((tm, tn), lambda m, j: (m, j)),
        out_shape=jax.ShapeDtypeStruct((t, d), BF16),
        compiler_params=_params(("parallel", "arbitrary")),
        name="branch_merge",
    )(y_hg, y_ssm, w_hg, w_ssm, gates, gates)


def _out_kernel(a_ref, w_ref, x_ref, post_ref, pre_ref, x1_ref, h_ref):
    r = _dot(a_ref[...], w_ref[...])
    x1 = x_ref[...] + _rms(r, post_ref[...])
    x1_ref[...] = x1
    h_ref[...] = _rms(x1, pre_ref[...]).astype(BF16)


def _out_proj(mixed, w_out, x, post_w, pre_w, tm):
    t, d = x.shape
    c2 = lambda m: (0, 0)
    row = lambda m: (m, 0)
    return pl.pallas_call(
        _out_kernel,
        grid=(t // tm,),
        in_specs=[
            pl.BlockSpec((tm, d), row),
            pl.BlockSpec((d, d), c2),
            pl.BlockSpec((tm, d), row),
            pl.BlockSpec((1, d), c2),
            pl.BlockSpec((1, d), c2),
        ],
        out_specs=[pl.BlockSpec((tm, d), row), pl.BlockSpec((tm, d), row)],
        out_shape=[jax.ShapeDtypeStruct((t, d), F32), jax.ShapeDtypeStruct((t, d), BF16)],
        compiler_params=_params(("parallel",)),
        name="out_proj",
    )(mixed, w_out, x, post_w, pre_w)


def _ffn_up_kernel(h_ref, wg_ref, wu_ref, cw_ref, cb_ref, o_ref, ext_ref, wgb_ref, wub_ref):
    @pl.when(pl.program_id(1) == 0)
    def _():
        wgb_ref[...] = wg_ref[...].astype(BF16)
        wub_ref[...] = wu_ref[...].astype(BF16)

    h = h_ref[...]
    gate = _dot(h, wgb_ref[...])
    up = _dot(h, wub_ref[...])
    conv = _causal_conv(ext_ref, gate, cw_ref[...], cb_ref[...], FFN_CONV, pl.program_id(1) == 0)
    o_ref[...] = (jax.nn.gelu(conv, approximate=True) * up).astype(o_ref.dtype)


def _ffn_up(h, w_up, conv_w, conv_b, d_ff, tm, tn):
    t, d = h.shape
    k = conv_w.shape[0]
    return pl.pallas_call(
        _ffn_up_kernel,
        grid=(d_ff // tn, t // tm),
        in_specs=[
            pl.BlockSpec((tm, d), lambda j, m: (m, 0)),
            pl.BlockSpec((d, tn), lambda j, m: (0, j)),
            pl.BlockSpec((d, tn), lambda j, m: (0, d_ff // tn + j)),
            pl.BlockSpec((k, tn), lambda j, m: (0, j)),
            pl.BlockSpec((1, tn), lambda j, m: (0, j)),
        ],
        out_specs=pl.BlockSpec((tm, tn), lambda j, m: (m, j)),
        out_shape=jax.ShapeDtypeStruct((t, d_ff), BF16),
        scratch_shapes=[pltpu.VMEM((CARRY + tm, tn), F32),
                        pltpu.VMEM((d, tn), BF16), pltpu.VMEM((d, tn), BF16)],
        compiler_params=_params(("parallel", "arbitrary")),
        name="ffn_up",
    )(h, w_up, w_up, conv_w, conv_b)


def _ffn_down_kernel(a_ref, w_ref, x_ref, post_ref, o_ref, acc_ref):
    kstep = pl.program_id(1)

    @pl.when(kstep == 0)
    def _():
        acc_ref[...] = jnp.zeros_like(acc_ref)

    acc_ref[...] += _dot(a_ref[...], w_ref[...])

    @pl.when(kstep == pl.num_programs(1) - 1)
    def _():
        o_ref[...] = x_ref[...] + _rms(acc_ref[...], post_ref[...])


def _ffn_down(act, w_down, x1, post_w, tm, tk):
    t, d = x1.shape
    d_ff = act.shape[1]
    return pl.pallas_call(
        _ffn_down_kernel,
        grid=(t // tm, d_ff // tk),
        in_specs=[
            pl.BlockSpec((tm, tk), lambda m, k: (m, k)),
            pl.BlockSpec((tk, d), lambda m, k: (k, 0)),
            pl.BlockSpec((tm, d), lambda m, k: (m, 0)),
            pl.BlockSpec((1, d), lambda m, k: (0, 0)),
        ],
        out_specs=pl.BlockSpec((tm, d), lambda m, k: (m, 0)),
        out_shape=jax.ShapeDtypeStruct((t, d), F32),
        scratch_shapes=[pltpu.VMEM((tm, d), F32)],
        compiler_params=_params(("parallel", "arbitrary")),
        name="ffn_down",
    )(act, w_down, x1, post_w)


def _layer(x, w_in, mix_pre, mix_post, lb_table, hg_norm, conv_w, conv_b, dt_bias, a_log, d_skip,
           ssm_norm, w_hg, w_ssm, w_out, ffn_pre, ffn_post, w_up, ffn_cw, ffn_cb, w_down):
    t, d = x.shape
    hg_v = w_hg.shape[0]
    hg_heads = hg_v // HG_DV
    d_inner = w_ssm.shape[0]
    ssm_heads = a_log.shape[0]
    hpg = ssm_heads // SSM_GROUPS
    gw = d_inner // SSM_GROUPS
    d_ff = w_down.shape[0]
    conv_dim = conv_w.shape[1]

    main_n = 4 * hg_v + d_inner + conv_dim
    dt_off = main_n
    gate_off = dt_off + ssm_heads
    z_off = 4 * hg_v
    xbc_off = z_off + d_inner

    w_t = w_in.T
    h, dt_t = _norm_dt(x, mix_pre.reshape(1, d), w_t[dt_off:gate_off], tm=512)
    tn = INPROJ_TILE
    bounds = (hg_v // tn, 2 * hg_v // tn, 3 * hg_v // tn, xbc_off // tn)
    proj = _inproj_main(h, w_t, lb_table, conv_w, conv_b.reshape(1, conv_dim), main_n, bounds,
                        tm=INPROJ_ROWS, tn=tn)
    gates = _proj_nt(h, w_t, gate_off, 2 * d, 1024, 1024, "inproj_gates")

    y_hg = _hgrn2(proj, hg_norm.reshape(1, HG_DV), hg_heads, HG_BLOCK, HG_CHUNK, HG_HEADS_PER_STEP)

    dt_hm = dt_t.reshape(SSM_GROUPS, hpg, t)
    y_ssm = _ssd(
        proj, dt_hm,
        dt_bias.reshape(SSM_GROUPS, hpg, 1), a_log.reshape(SSM_GROUPS, hpg, 1),
        jnp.repeat(d_skip, SSM_HEADDIM).reshape(SSM_GROUPS, 1, gw),
        ssm_norm.reshape(SSM_GROUPS, 1, gw), z_off, xbc_off, d_inner)

    mixed = _merge(y_hg, y_ssm, w_hg.astype(BF16), w_ssm.astype(BF16), gates, tm=512, tn=512)
    x1, h2 = _out_proj(mixed, w_out.astype(BF16), x, mix_post.reshape(1, d), ffn_pre.reshape(1, d), tm=256)
    act = _ffn_up(h2, w_up, ffn_cw, ffn_cb.reshape(1, d_ff), d_ff, tm=1024, tn=512)
    return _ffn_down(act, w_down.astype(BF16), x1, ffn_post.reshape(1, d), tm=512, tk=512)


def kernel(x, w_in, mix_pre_norm, mix_post_norm, hg_lb_table, hg_out_norm, ssm_conv_w, ssm_conv_b,
           ssm_dt_bias, ssm_A_log, ssm_D, ssm_out_norm, w_branch_hg, w_branch_ssm, w_out,
           ffn_pre_norm, ffn_post_norm, ffn_w_up, ffn_conv_w, ffn_conv_b, ffn_w_down):
    bsz, t, d = x.shape
    depth = w_in.shape[0]
    assert depth == 1 and hg_lb_table.shape[0] == 2, "forget-gate lower bound is computed for one layer"
    outs = []
    for b in range(bsz):
        xb = x[b]
        for l in range(depth):
            xb = _layer(xb, w_in[l], mix_pre_norm[l], mix_post_norm[l], hg_lb_table, hg_out_norm[l],
                        ssm_conv_w[l], ssm_conv_b[l], ssm_dt_bias[l], ssm_A_log[l], ssm_D[l],
                        ssm_out_norm[l], w_branch_hg[l], w_branch_ssm[l], w_out[l],
                        ffn_pre_norm[l], ffn_post_norm[l], ffn_w_up[l], ffn_conv_w[l], ffn_conv_b[l],
                        ffn_w_down[l])
        outs.append(xb)
    return jnp.stack(outs, axis=0)
```

```python
import functools

import numpy as np
import jax
import jax.numpy as jnp
from jax import lax
from jax.experimental import pallas as pl
from jax.experimental.pallas import tpu as pltpu

F32 = jnp.float32
BF16 = jnp.bfloat16
EPS = 1e-6

HG_DK = 128
HG_DV = 128
SSM_HEADDIM = 64
SSM_GROUPS = 8
SSM_DSTATE = 128
SSM_CONV = 4
FFN_CONV = 3

INPROJ_TILE = 1024
INPROJ_ROWS = 1024
HG_CHUNK = 128
HG_BLOCK = 1024
HG_HEADS_PER_STEP = 8
SSD_CHUNK = 256
CARRY = 8
FFN_DOWN_TK = 1408

VMEM_LIMIT = 48 * 1024 * 1024
LOG2E = 1.4426950408889634


def _params(sem):
    return pltpu.CompilerParams(dimension_semantics=sem, vmem_limit_bytes=VMEM_LIMIT)


def _dot(a, b):
    return jnp.dot(a, b, preferred_element_type=F32)


def _dot_nt(a, b):
    return lax.dot_general(a, b, (((1,), (1,)), ((), ())), preferred_element_type=F32)


def _dot_tn(a, b):
    return lax.dot_general(a, b, (((0,), (0,)), ((), ())), preferred_element_type=F32)


def _split_bf16(a):
    hi = a.astype(BF16)
    lo = (a - hi.astype(F32)).astype(BF16)
    return hi, lo


def _sigmoid(a):
    return 0.5 * jnp.tanh(0.5 * a) + 0.5


def _silu(a):
    return a * _sigmoid(a)


def _rms(x, w):
    return x * lax.rsqrt(jnp.mean(x * x, axis=-1, keepdims=True) + EPS) * w


def _norm_dt_kernel(x_ref, nw_ref, wdt_ref, h_ref, dt_ref):
    h = _rms(x_ref[...], nw_ref[...]).astype(BF16)
    h_ref[...] = h
    dt_ref[...] = _dot_nt(wdt_ref[...].astype(BF16), h)


def _norm_dt(x, nw, wdt_t, tm):
    t, d = x.shape
    nh = wdt_t.shape[0]
    return pl.pallas_call(
        _norm_dt_kernel,
        grid=(t // tm,),
        in_specs=[
            pl.BlockSpec((tm, d), lambda m: (m, 0)),
            pl.BlockSpec((1, d), lambda m: (0, 0)),
            pl.BlockSpec((nh, d), lambda m: (0, 0)),
        ],
        out_specs=[pl.BlockSpec((tm, d), lambda m: (m, 0)),
                   pl.BlockSpec((nh, tm), lambda m: (0, m))],
        out_shape=[jax.ShapeDtypeStruct((t, d), BF16), jax.ShapeDtypeStruct((nh, t), F32)],
        compiler_params=_params(("parallel",)),
        name="norm_dt",
    )(x, nw, wdt_t)


def _proj_nt_kernel(h_ref, w_ref, o_ref, wb_ref):
    @pl.when(pl.program_id(1) == 0)
    def _():
        wb_ref[...] = w_ref[...].astype(BF16)

    o_ref[...] = _dot_nt(h_ref[...], wb_ref[...]).astype(o_ref.dtype)


def _proj_nt(h, w_t, row_off, n, tm, tn, name):
    t, d = h.shape
    if row_off % tn == 0:
        w_spec = pl.BlockSpec((tn, d), lambda j, m: (row_off // tn + j, 0))
    else:
        assert row_off % 8 == 0 and tn % 8 == 0
        w_spec = pl.BlockSpec((pl.Element(tn), pl.Element(d)),
                              lambda j, m: (pl.multiple_of(row_off + j * tn, 8), 0))
    return pl.pallas_call(
        _proj_nt_kernel,
        grid=(n // tn, t // tm),
        in_specs=[pl.BlockSpec((tm, d), lambda j, m: (m, 0)), w_spec],
        out_specs=pl.BlockSpec((tm, tn), lambda j, m: (m, j)),
        out_shape=jax.ShapeDtypeStruct((t, n), BF16),
        scratch_shapes=[pltpu.VMEM((tn, d), BF16)],
        compiler_params=_params(("parallel", "arbitrary")),
        name=name,
    )(h, w_t)


def _conv_taps(ext_ref, cur, w, b, ksize):
    q = cur.shape[0]
    ext_ref[CARRY:CARRY + q, :] = cur
    acc = cur * w[ksize - 1:ksize, :] + b
    for j in range(1, ksize):
        acc = acc + ext_ref[CARRY - j:CARRY - j + q, :] * w[ksize - 1 - j:ksize - j, :]
    ext_ref[0:CARRY, :] = ext_ref[q:q + CARRY, :]
    return acc


_STAGE_EXTRAS = {"qscale": 0, "silu": 0, "logf": 1, "conv": 2}


def _seg_proj_kernel(h_ref, w_ref, *rest, kind):
    nx = _STAGE_EXTRAS[kind]
    extra = rest[:nx]
    o_ref, wb_ref = rest[nx:nx + 2]
    first = pl.program_id(1) == 0

    @pl.when(first)
    def _():
        wb_ref[...] = w_ref[...].astype(BF16)

    if kind == "conv":
        ext_ref = rest[nx + 2]

        @pl.when(first)
        def _():
            ext_ref[0:CARRY, :] = jnp.zeros((CARRY, ext_ref.shape[1]), F32)

    acc = _dot_nt(h_ref[...], wb_ref[...])
    if kind == "qscale":
        out = _silu(acc) * HG_DK ** -0.5
    elif kind == "silu":
        out = _silu(acc)
    elif kind == "logf":
        tab = extra[0][...]
        te = jnp.exp(tab - jnp.max(tab, axis=0, keepdims=True))
        lb = te[0:1, :] / jnp.sum(te, axis=0, keepdims=True)
        out = jnp.log(lb + (1.0 - lb) * _sigmoid(acc)) * LOG2E
    else:
        out = _silu(_conv_taps(ext_ref, acc, extra[0][...], extra[1][...], SSM_CONV))
    o_ref[...] = out.astype(o_ref.dtype)


def _seg_proj(h, w_t, row_off, n, kind, extras, tm, tn, name):
    t, d = h.shape
    assert row_off % tn == 0 and len(extras) == _STAGE_EXTRAS[kind]
    scratch = [pltpu.VMEM((tn, d), BF16)]
    if kind == "conv":
        scratch.append(pltpu.VMEM((CARRY + tm, tn), F32))
    return pl.pallas_call(
        functools.partial(_seg_proj_kernel, kind=kind),
        grid=(n // tn, t // tm),
        in_specs=[
            pl.BlockSpec((tm, d), lambda j, m: (m, 0)),
            pl.BlockSpec((tn, d), lambda j, m: (row_off // tn + j, 0)),
        ] + [pl.BlockSpec((e.shape[0], tn), lambda j, m: (0, j)) for e in extras],
        out_specs=pl.BlockSpec((tm, tn), lambda j, m: (m, j)),
        out_shape=jax.ShapeDtypeStruct((t, n), BF16),
        scratch_shapes=scratch,
        compiler_params=_params(("parallel", "arbitrary")),
        name=name,
    )(h, w_t, *extras)


def _hg_constants(c):
    levels = []
    h = c // 2
    while h >= 1:
        levels.append(h)
        h //= 2
    t = np.arange(c)[:, None]
    u = np.arange(c)[None, :]
    mats = [(u <= t)]
    masks = []
    for h in levels:
        blk = 2 * h
        mid = (t // blk) * blk + h
        second = (t % blk) >= h
        mats.append(np.where(second, (u >= mid) & (u <= t), (u > t) & (u < mid)))
        s = u
        masks.append(((t // blk) == (s // blk)) & second & ((s % blk) < h))
    masks.append(t == u)
    pm = np.concatenate(mats, axis=0).astype(np.float32)
    mk = np.stack(masks, axis=0).astype(np.float32)
    return len(levels), pm, mk


def _blockdiag(a, b):
    top = jnp.concatenate([a, jnp.zeros((a.shape[0], b.shape[1]), a.dtype)], axis=1)
    bot = jnp.concatenate([jnp.zeros((b.shape[0], a.shape[1]), b.dtype), b], axis=1)
    return jnp.concatenate([top, bot], axis=0)


def _hgrn2_kernel(q_ref, lf_ref, i_ref, g_ref, nw_ref, pm_ref, mk_ref, o_ref, st_ref, ex_ref,
                  *, chunk, nlev):
    c = chunk

    @pl.when(pl.program_id(1) == 0)
    def _():
        st_ref[...] = jnp.zeros_like(st_ref)

    nw = nw_ref[...]
    pm = pm_ref[...]
    nh = st_ref.shape[0]
    dk = HG_DK

    def body(ci, carry):
        r0 = pl.multiple_of(ci * c, c)
        rows = pl.ds(r0, c)
        for p in range(nh // 2):
            ha, hb = 2 * p, 2 * p + 1
            sa = slice(ha * dk, (ha + 1) * dk)
            sb = slice(hb * dk, (hb + 1) * dk)
            ps = slice(ha * dk, (hb + 1) * dk)
            lo, hi = slice(0, dk), slice(dk, 2 * dk)
            qp = q_ref[rows, ps]
            lfp = lf_ref[rows, ps]
            kp = (1.0 - jnp.exp2(lfp.astype(F32))).astype(BF16)
            ex_ref[:, ps] = _dot(pm, lfp)
            b = ex_ref[0:c, ps]
            b_last = b[c - 1:c, :]
            st_a = st_ref[ha]
            st_b = st_ref[hb]
            o = _dot_nt(qp * jnp.exp2(b).astype(BF16),
                        _blockdiag(st_a.astype(BF16), st_b.astype(BF16)))
            sc = mk_ref[nlev] * _dot_nt(qp, _blockdiag(kp[:, lo], kp[:, hi])).astype(BF16)
            for l in range(nlev):
                e = jnp.exp2(ex_ref[(l + 1) * c:(l + 2) * c, ps]).astype(BF16)
                ke = kp * e
                s = _dot_nt(qp * e, _blockdiag(ke[:, lo], ke[:, hi]))
                sc = sc + mk_ref[l] * s.astype(BF16)
            o = o + _dot(sc, _blockdiag(i_ref[rows, sa], i_ref[rows, sb]))
            kdec = kp * jnp.exp2(b_last - b).astype(BF16)
            sdec = jnp.exp2(b_last)
            st_ref[ha] = st_a * sdec[:, lo] + _dot_tn(i_ref[rows, sa], kdec[:, lo])
            st_ref[hb] = st_b * sdec[:, hi] + _dot_tn(i_ref[rows, sb], kdec[:, hi])
            o_ref[rows, sa] = (_rms(o[:, lo], nw) * g_ref[rows, sa].astype(F32)).astype(o_ref.dtype)
            o_ref[rows, sb] = (_rms(o[:, hi], nw) * g_ref[rows, sb].astype(F32)).astype(o_ref.dtype)
        return carry

    lax.fori_loop(0, q_ref.shape[0] // c, body, 0)


def _hgrn2(qh, lf, v, gate, gate_off, norm_w, n_heads, tb, chunk, hps):
    t = qh.shape[0]
    nlev, pm, mk = _hg_constants(chunk)
    pm = jnp.asarray(pm, BF16)
    mk = jnp.asarray(np.concatenate([mk, mk], axis=2), BF16)
    w = hps * HG_DK
    nblk = n_heads // hps
    col = lambda off: (lambda h, i: (i, off // w + h))
    const2 = lambda h, i: (0, 0)
    return pl.pallas_call(
        functools.partial(_hgrn2_kernel, chunk=chunk, nlev=nlev),
        grid=(nblk, t // tb),
        in_specs=[
            pl.BlockSpec((tb, w), col(0)),
            pl.BlockSpec((tb, w), col(0)),
            pl.BlockSpec((tb, w), col(0)),
            pl.BlockSpec((tb, w), col(gate_off)),
            pl.BlockSpec((1, HG_DV), const2),
            pl.BlockSpec(pm.shape, const2),
            pl.BlockSpec(mk.shape, lambda h, i: (0, 0, 0)),
        ],
        out_specs=pl.BlockSpec((tb, w), lambda h, i: (i, h)),
        out_shape=jax.ShapeDtypeStruct((t, n_heads * HG_DV), BF16),
        scratch_shapes=[pltpu.VMEM((hps, HG_DV, HG_DK), F32),
                        pltpu.VMEM(((nlev + 1) * chunk, w), F32)],
        compiler_params=_params(("parallel", "arbitrary")),
        name="hgrn2_scan",
    )(qh, lf, v, gate, norm_w, pm, mk)


def _ssd_expand_mats(hpg):
    ea = np.zeros((128, hpg * SSM_HEADDIM), np.float32)
    ed = np.zeros((128, hpg * SSM_HEADDIM), np.float32)
    for h in range(hpg):
        ea[h, h * SSM_HEADDIM:(h + 1) * SSM_HEADDIM] = 1.0
        ed[hpg + h, h * SSM_HEADDIM:(h + 1) * SSM_HEADDIM] = 1.0
    return ea, ed


def _ssd_kernel(z_ref, x_ref, b_ref, c_ref, dt_ref, dtb_ref, alog_ref, dskip_ref, nw_ref,
                triu_ref, ea_ref, ed_ref, o_ref, s_ref):
    q = x_ref.shape[0]
    hpg = dt_ref.shape[1]

    @pl.when(pl.program_id(1) == 0)
    def _():
        s_ref[...] = jnp.zeros_like(s_ref)

    xs = x_ref[...].astype(F32)
    bm16 = b_ref[...]
    cm16 = c_ref[...]

    dt_r = jax.nn.softplus(dt_ref[0] + dtb_ref[0])
    da_r = dt_r * (-jnp.exp(alog_ref[0]) * LOG2E)
    da_hi, da_lo = _split_bf16(da_r)
    triu = triu_ref[...]
    acum_r = _dot(da_hi, triu) + _dot(da_lo, triu)
    pad = jnp.zeros((128 - 2 * hpg, q), F32)
    cols = jnp.concatenate([acum_r, dt_r, pad], axis=0).T

    lane = lax.broadcasted_iota(jnp.int32, (q, 128), 1)
    acol = jnp.where(lane < hpg, cols, 0.0)
    a_last = acol[q - 1:q, :]
    dt_x = _dot(cols.astype(BF16), ed_ref[...])
    ea_x = _dot(jnp.exp2(acol).astype(BF16), ea_ref[...])
    wd_x = _dot(jnp.exp2(a_last - acol).astype(BF16), ea_ref[...])
    sd_x = _dot(jnp.broadcast_to(jnp.exp2(a_last), (8, 128)).astype(BF16), ea_ref[...])[0:1, :]

    xdt = xs * dt_x
    xdt16 = xdt.astype(BF16)
    cb16 = _dot_nt(cm16, bm16).astype(BF16)
    cs = _dot(cm16, s_ref[...].astype(BF16))
    row = lax.broadcasted_iota(jnp.int32, (q, q), 0)
    colid = lax.broadcasted_iota(jnp.int32, (q, q), 1)
    causal = row >= colid
    lo_half = lane < SSM_HEADDIM

    y_parts = []
    for p in range(hpg // 2):
        ys = []
        for h in (2 * p, 2 * p + 1):
            dm = cols[:, h:h + 1] - acum_r[h:h + 1, :]
            lm16 = jnp.exp2(jnp.where(causal, dm, -1e30)).astype(BF16)
            ys.append(_dot(cb16 * lm16, xdt16[:, p * 128:(p + 1) * 128]))
        y_parts.append(jnp.where(lo_half, ys[0], ys[1]))

    y = jnp.concatenate(y_parts, axis=1) + cs * ea_x
    s_ref[...] = s_ref[...] * sd_x + _dot_tn(bm16, (xdt * wd_x).astype(BF16))

    y = y + dskip_ref[0] * xs
    y = y * z_ref[...].astype(F32)
    o_ref[...] = _rms(y, nw_ref[0]).astype(o_ref.dtype)


def _ssd(zs, z_off, xbc, dt_hm, dt_bias, a_log, d_skip, norm_w, d_inner):
    t = xbc.shape[0]
    xbc_off = 0
    q = SSD_CHUNK
    g = SSM_GROUPS
    n = SSM_DSTATE
    gw = d_inner // g
    hpg = gw // SSM_HEADDIM
    triu = jnp.asarray(np.triu(np.ones((q, q), np.float32)), BF16)
    ea, ed = (jnp.asarray(m, BF16) for m in _ssd_expand_mats(hpg))
    c2 = lambda gi, i: (0, 0)
    return pl.pallas_call(
        _ssd_kernel,
        grid=(g, t // q),
        in_specs=[
            pl.BlockSpec((q, gw), lambda gi, i: (i, z_off // gw + gi)),
            pl.BlockSpec((q, gw), lambda gi, i: (i, xbc_off // gw + gi)),
            pl.BlockSpec((q, n), lambda gi, i: (i, (xbc_off + d_inner) // n + gi)),
            pl.BlockSpec((q, n), lambda gi, i: (i, (xbc_off + d_inner) // n + g + gi)),
            pl.BlockSpec((1, hpg, q), lambda gi, i: (gi, 0, i)),
            pl.BlockSpec((1, hpg, 1), lambda gi, i: (gi, 0, 0)),
            pl.BlockSpec((1, hpg, 1), lambda gi, i: (gi, 0, 0)),
            pl.BlockSpec((1, 1, gw), lambda gi, i: (gi, 0, 0)),
            pl.BlockSpec((1, 1, gw), lambda gi, i: (gi, 0, 0)),
            pl.BlockSpec((q, q), c2),
            pl.BlockSpec(ea.shape, c2),
            pl.BlockSpec(ed.shape, c2),
        ],
        out_specs=pl.BlockSpec((q, gw), lambda gi, i: (i, gi)),
        out_shape=jax.ShapeDtypeStruct((t, d_inner), BF16),
        scratch_shapes=[pltpu.VMEM((n, gw), F32)],
        compiler_params=_params(("parallel", "arbitrary")),
        name="ssd_scan",
    )(zs, xbc, xbc, xbc, dt_hm, dt_bias, a_log, d_skip, norm_w, triu, ea, ed)


def _merge_kernel(yh_ref, ys_ref, wh_ref, ws_ref, gh_ref, gs_ref, o_ref):
    a = _dot(yh_ref[...], wh_ref[...])
    b = _dot(ys_ref[...], ws_ref[...])
    gh = _sigmoid(gh_ref[...].astype(F32))
    gs = _sigmoid(gs_ref[...].astype(F32))
    o_ref[...] = (gh * a + gs * b).astype(o_ref.dtype)


def _merge(y_hg, y_ssm, w_hg, w_ssm, gates, tm, tn):
    t = y_hg.shape[0]
    d = w_hg.shape[1]
    return pl.pallas_call(
        _merge_kernel,
        grid=(t // tm, d // tn),
        in_specs=[
            pl.BlockSpec((tm, y_hg.shape[1]), lambda m, j: (m, 0)),
            pl.BlockSpec((tm, y_ssm.shape[1]), lambda m, j: (m, 0)),
            pl.BlockSpec((w_hg.shape[0], tn), lambda m, j: (0, j)),
            pl.BlockSpec((w_ssm.shape[0], tn), lambda m, j: (0, j)),
            pl.BlockSpec((tm, tn), lambda m, j: (m, j)),
            pl.BlockSpec((tm, tn), lambda m, j: (m, d // tn + j)),
        ],
        out_specs=pl.BlockSpec((tm, tn), lambda m, j: (m, j)),
        out_shape=jax.ShapeDtypeStruct((t, d), BF16),
        compiler_params=_params(("parallel", "arbitrary")),
        name="branch_merge",
    )(y_hg, y_ssm, w_hg, w_ssm, gates, gates)


def _out_kernel(a_ref, w_ref, x_ref, post_ref, pre_ref, x1_ref, h_ref):
    r = _dot(a_ref[...], w_ref[...])
    x1 = x_ref[...] + _rms(r, post_ref[...])
    x1_ref[...] = x1
    h_ref[...] = _rms(x1, pre_ref[...]).astype(BF16)


def _out_proj(mixed, w_out, x, post_w, pre_w, tm):
    t, d = x.shape
    c2 = lambda m: (0, 0)
    row = lambda m: (m, 0)
    return pl.pallas_call(
        _out_kernel,
        grid=(t // tm,),
        in_specs=[
            pl.BlockSpec((tm, d), row),
            pl.BlockSpec((d, d), c2),
            pl.BlockSpec((tm, d), row),
            pl.BlockSpec((1, d), c2),
            pl.BlockSpec((1, d), c2),
        ],
        out_specs=[pl.BlockSpec((tm, d), row), pl.BlockSpec((tm, d), row)],
        out_shape=[jax.ShapeDtypeStruct((t, d), F32), jax.ShapeDtypeStruct((t, d), BF16)],
        compiler_params=_params(("parallel",)),
        name="out_proj",
    )(mixed, w_out, x, post_w, pre_w)


def _ffn_up_kernel(h_ref, wg_ref, wu_ref, cw_ref, cb_ref, o_ref, ext_ref, wgb_ref, wub_ref):
    @pl.when(pl.program_id(1) == 0)
    def _():
        wgb_ref[...] = wg_ref[...].astype(BF16)
        wub_ref[...] = wu_ref[...].astype(BF16)
        ext_ref[0:CARRY, :] = jnp.zeros((CARRY, ext_ref.shape[1]), F32)

    h = h_ref[...]
    gate = _dot(h, wgb_ref[...])
    up = _dot(h, wub_ref[...])
    conv = _conv_taps(ext_ref, gate, cw_ref[...], cb_ref[...], FFN_CONV)
    o_ref[...] = (jax.nn.gelu(conv, approximate=True) * up).astype(o_ref.dtype)


def _ffn_up(h, w_up, conv_w, conv_b, d_ff, tm, tn):
    t, d = h.shape
    k = conv_w.shape[0]
    return pl.pallas_call(
        _ffn_up_kernel,
        grid=(d_ff // tn, t // tm),
        in_specs=[
            pl.BlockSpec((tm, d), lambda j, m: (m, 0)),
            pl.BlockSpec((d, tn), lambda j, m: (0, j)),
            pl.BlockSpec((d, tn), lambda j, m: (0, d_ff // tn + j)),
            pl.BlockSpec((k, tn), lambda j, m: (0, j)),
            pl.BlockSpec((1, tn), lambda j, m: (0, j)),
        ],
        out_specs=pl.BlockSpec((tm, tn), lambda j, m: (m, j)),
        out_shape=jax.ShapeDtypeStruct((t, d_ff), BF16),
        scratch_shapes=[pltpu.VMEM((CARRY + tm, tn), F32),
                        pltpu.VMEM((d, tn), BF16), pltpu.VMEM((d, tn), BF16)],
        compiler_params=_params(("parallel", "arbitrary")),
        name="ffn_up",
    )(h, w_up, w_up, conv_w, conv_b)


def _ffn_down_kernel(a_ref, w_ref, x_ref, post_ref, o_ref, acc_ref):
    kstep = pl.program_id(1)

    @pl.when(kstep == 0)
    def _():
        acc_ref[...] = jnp.zeros_like(acc_ref)

    acc_ref[...] += _dot(a_ref[...], w_ref[...])

    @pl.when(kstep == pl.num_programs(1) - 1)
    def _():
        o_ref[...] = x_ref[...] + _rms(acc_ref[...], post_ref[...])


def _ffn_down(act, w_down, x1, post_w, tm, tk):
    t, d = x1.shape
    d_ff = act.shape[1]
    return pl.pallas_call(
        _ffn_down_kernel,
        grid=(t // tm, d_ff // tk),
        in_specs=[
            pl.BlockSpec((tm, tk), lambda m, k: (m, k)),
            pl.BlockSpec((tk, d), lambda m, k: (k, 0)),
            pl.BlockSpec((tm, d), lambda m, k: (m, 0)),
            pl.BlockSpec((1, d), lambda m, k: (0, 0)),
        ],
        out_specs=pl.BlockSpec((tm, d), lambda m, k: (m, 0)),
        out_shape=jax.ShapeDtypeStruct((t, d), F32),
        scratch_shapes=[pltpu.VMEM((tm, d), F32)],
        compiler_params=_params(("parallel", "arbitrary")),
        name="ffn_down",
    )(act, w_down, x1, post_w)


def _layer(x, w_in, mix_pre, mix_post, lb_table, hg_norm, conv_w, conv_b, dt_bias, a_log, d_skip,
           ssm_norm, w_hg, w_ssm, w_out, ffn_pre, ffn_post, w_up, ffn_cw, ffn_cb, w_down):
    t, d = x.shape
    hg_v = w_hg.shape[0]
    hg_heads = hg_v // HG_DV
    d_inner = w_ssm.shape[0]
    ssm_heads = a_log.shape[0]
    hpg = ssm_heads // SSM_GROUPS
    gw = d_inner // SSM_GROUPS
    d_ff = w_down.shape[0]
    conv_dim = conv_w.shape[1]

    main_n = 4 * hg_v + d_inner + conv_dim
    dt_off = main_n
    gate_off = dt_off + ssm_heads
    z_off = 4 * hg_v
    xbc_off = z_off + d_inner

    w_t = w_in.T
    h, dt_t = _norm_dt(x, mix_pre.reshape(1, d), w_t[dt_off:gate_off], tm=512)
    tm, tn = INPROJ_ROWS, INPROJ_TILE
    qh = _seg_proj(h, w_t, 0, hg_v, "qscale", (), tm, tn, "inproj_q")
    lf = _seg_proj(h, w_t, hg_v, hg_v, "logf", (lb_table,), tm, tn, "inproj_f")
    v = _proj_nt(h, w_t, 2 * hg_v, hg_v, tm, tn, "inproj_i")
    gz = _seg_proj(h, w_t, 3 * hg_v, hg_v + d_inner, "silu", (), tm, tn, "inproj_gz")
    xbc = _seg_proj(h, w_t, xbc_off, conv_dim, "conv", (conv_w, conv_b.reshape(1, conv_dim)), tm, tn,
                    "inproj_xbc")
    gates = _proj_nt(h, w_t, gate_off, 2 * d, tm, tn, "inproj_gates")

    y_hg = _hgrn2(qh, lf, v, gz, 0, hg_norm.reshape(1, HG_DV), hg_heads, HG_BLOCK, HG_CHUNK,
                  HG_HEADS_PER_STEP)

    dt_hm = dt_t.reshape(SSM_GROUPS, hpg, t)
    y_ssm = _ssd(
        gz, hg_v, xbc, dt_hm,
        dt_bias.reshape(SSM_GROUPS, hpg, 1), a_log.reshape(SSM_GROUPS, hpg, 1),
        jnp.repeat(d_skip, SSM_HEADDIM).reshape(SSM_GROUPS, 1, gw),
        ssm_norm.reshape(SSM_GROUPS, 1, gw), d_inner)

    mixed = _merge(y_hg, y_ssm, w_hg.astype(BF16), w_ssm.astype(BF16), gates, tm=512, tn=512)
    x1, h2 = _out_proj(mixed, w_out.astype(BF16), x, mix_post.reshape(1, d), ffn_pre.reshape(1, d), tm=512)
    act = _ffn_up(h2, w_up, ffn_cw, ffn_cb.reshape(1, d_ff), d_ff, tm=1024, tn=512)
    return _ffn_down(act, w_down.astype(BF16), x1, ffn_post.reshape(1, d), tm=512, tk=FFN_DOWN_TK)


def kernel(x, w_in, mix_pre_norm, mix_post_norm, hg_lb_table, hg_out_norm, ssm_conv_w, ssm_conv_b,
           ssm_dt_bias, ssm_A_log, ssm_D, ssm_out_norm, w_branch_hg, w_branch_ssm, w_out,
           ffn_pre_norm, ffn_post_norm, ffn_w_up, ffn_conv_w, ffn_conv_b, ffn_w_down):
    bsz, t, d = x.shape
    depth = w_in.shape[0]
    assert depth == 1 and hg_lb_table.shape[0] == 2, "forget-gate lower bound is computed for one layer"
    outs = []
    for b in range(bsz):
        xb = x[b]
        for l in range(depth):
            xb = _layer(xb, w_in[l], mix_pre_norm[l], mix_post_norm[l], hg_lb_table, hg_out_norm[l],
                        ssm_conv_w[l], ssm_conv_b[l], ssm_dt_bias[l], ssm_A_log[l], ssm_D[l],
                        ssm_out_norm[l], w_branch_hg[l], w_branch_ssm[l], w_out[l],
                        ffn_pre_norm[l], ffn_post_norm[l], ffn_w_up[l], ffn_conv_w[l], ffn_conv_b[l],
                        ffn_w_down[l])
        outs.append(xb)
    return jnp.stack(outs, axis=0)
```

```python
import functools

import numpy as np
import jax
import jax.numpy as jnp
from jax import lax
from jax.experimental import pallas as pl
from jax.experimental.pallas import tpu as pltpu

F32 = jnp.float32
BF16 = jnp.bfloat16
EPS = 1e-6

HG_DK = 128
HG_DV = 128
SSM_HEADDIM = 64
SSM_GROUPS = 8
SSM_DSTATE = 128
SSM_CONV = 4
FFN_CONV = 3

INPROJ_TILE = 1024
INPROJ_ROWS = 1024
HG_CHUNK = 128
HG_BLOCK = 1024
HG_HEADS_PER_STEP = 8
SSD_CHUNK = 256
SSD_GROUPS_PER_STEP = 8
CARRY = 8
FFN_DOWN_TK = 1408
FFN_DOWN_SLAB = 512

VMEM_LIMIT = 48 * 1024 * 1024
VMEM_LIMIT_LARGE = 56 * 1024 * 1024
LOG2E = 1.4426950408889634


def _params(sem, vmem=VMEM_LIMIT):
    return pltpu.CompilerParams(dimension_semantics=sem, vmem_limit_bytes=vmem)


def _dot(a, b):
    return jnp.dot(a, b, preferred_element_type=F32)


def _dot_nt(a, b):
    return lax.dot_general(a, b, (((1,), (1,)), ((), ())), preferred_element_type=F32)


def _dot_tn(a, b):
    return lax.dot_general(a, b, (((0,), (0,)), ((), ())), preferred_element_type=F32)


def _split_bf16(a):
    hi = a.astype(BF16)
    lo = (a - hi.astype(F32)).astype(BF16)
    return hi, lo


def _sigmoid(a):
    return 0.5 * jnp.tanh(0.5 * a) + 0.5


def _silu(a):
    return a * _sigmoid(a)


def _rms(x, w):
    return x * lax.rsqrt(jnp.mean(x * x, axis=-1, keepdims=True) + EPS) * w


def _norm_dt_kernel(x_ref, nw_ref, wdt_ref, h_ref, dt_ref):
    h = _rms(x_ref[...], nw_ref[...]).astype(BF16)
    h_ref[...] = h
    dt_ref[...] = _dot_nt(wdt_ref[...].astype(BF16), h)


def _norm_dt(x, nw, wdt_t, tm):
    t, d = x.shape
    nh = wdt_t.shape[0]
    return pl.pallas_call(
        _norm_dt_kernel,
        grid=(t // tm,),
        in_specs=[
            pl.BlockSpec((tm, d), lambda m: (m, 0)),
            pl.BlockSpec((1, d), lambda m: (0, 0)),
            pl.BlockSpec((nh, d), lambda m: (0, 0)),
        ],
        out_specs=[pl.BlockSpec((tm, d), lambda m: (m, 0)),
                   pl.BlockSpec((nh, tm), lambda m: (0, m))],
        out_shape=[jax.ShapeDtypeStruct((t, d), BF16), jax.ShapeDtypeStruct((nh, t), F32)],
        compiler_params=_params(("parallel",)),
        name="norm_dt",
    )(x, nw, wdt_t)


def _proj_nt_kernel(h_ref, w_ref, o_ref, wb_ref):
    @pl.when(pl.program_id(1) == 0)
    def _():
        wb_ref[...] = w_ref[...].astype(BF16)

    o_ref[...] = _dot_nt(h_ref[...], wb_ref[...]).astype(o_ref.dtype)


def _proj_nt(h, w_t, row_off, n, tm, tn, name):
    t, d = h.shape
    if row_off % tn == 0:
        w_spec = pl.BlockSpec((tn, d), lambda j, m: (row_off // tn + j, 0))
    else:
        assert row_off % 8 == 0 and tn % 8 == 0
        w_spec = pl.BlockSpec((pl.Element(tn), pl.Element(d)),
                              lambda j, m: (pl.multiple_of(row_off + j * tn, 8), 0))
    return pl.pallas_call(
        _proj_nt_kernel,
        grid=(n // tn, t // tm),
        in_specs=[pl.BlockSpec((tm, d), lambda j, m: (m, 0)), w_spec],
        out_specs=pl.BlockSpec((tm, tn), lambda j, m: (m, j)),
        out_shape=jax.ShapeDtypeStruct((t, n), BF16),
        scratch_shapes=[pltpu.VMEM((tn, d), BF16)],
        compiler_params=_params(("parallel", "arbitrary")),
        name=name,
    )(h, w_t)


def _conv_taps(ext_ref, cur, w, b, ksize):
    q = cur.shape[0]
    ext_ref[CARRY:CARRY + q, :] = cur
    acc = cur * w[ksize - 1:ksize, :] + b
    for j in range(1, ksize):
        acc = acc + ext_ref[CARRY - j:CARRY - j + q, :] * w[ksize - 1 - j:ksize - j, :]
    ext_ref[0:CARRY, :] = ext_ref[q:q + CARRY, :]
    return acc


_STAGE_EXTRAS = {"qscale": 0, "silu": 0, "logf": 1, "conv": 2}


def _seg_proj_kernel(h_ref, w_ref, *rest, kind):
    nx = _STAGE_EXTRAS[kind]
    extra = rest[:nx]
    o_ref, wb_ref = rest[nx:nx + 2]
    first = pl.program_id(1) == 0

    @pl.when(first)
    def _():
        wb_ref[...] = w_ref[...].astype(BF16)

    if kind == "conv":
        ext_ref = rest[nx + 2]

        @pl.when(first)
        def _():
            ext_ref[0:CARRY, :] = jnp.zeros((CARRY, ext_ref.shape[1]), F32)

    acc = _dot_nt(h_ref[...], wb_ref[...])
    if kind == "qscale":
        out = _silu(acc) * HG_DK ** -0.5
    elif kind == "silu":
        out = _silu(acc)
    elif kind == "logf":
        tab = extra[0][...]
        te = jnp.exp(tab - jnp.max(tab, axis=0, keepdims=True))
        lb = te[0:1, :] / jnp.sum(te, axis=0, keepdims=True)
        out = jnp.log(lb + (1.0 - lb) * _sigmoid(acc)) * LOG2E
    else:
        out = _silu(_conv_taps(ext_ref, acc, extra[0][...], extra[1][...], SSM_CONV))
    o_ref[...] = out.astype(o_ref.dtype)


def _seg_proj(h, w_t, row_off, n, kind, extras, tm, tn, name):
    t, d = h.shape
    assert row_off % tn == 0 and len(extras) == _STAGE_EXTRAS[kind]
    scratch = [pltpu.VMEM((tn, d), BF16)]
    if kind == "conv":
        scratch.append(pltpu.VMEM((CARRY + tm, tn), F32))
    return pl.pallas_call(
        functools.partial(_seg_proj_kernel, kind=kind),
        grid=(n // tn, t // tm),
        in_specs=[
            pl.BlockSpec((tm, d), lambda j, m: (m, 0)),
            pl.BlockSpec((tn, d), lambda j, m: (row_off // tn + j, 0)),
        ] + [pl.BlockSpec((e.shape[0], tn), lambda j, m: (0, j)) for e in extras],
        out_specs=pl.BlockSpec((tm, tn), lambda j, m: (m, j)),
        out_shape=jax.ShapeDtypeStruct((t, n), BF16),
        scratch_shapes=scratch,
        compiler_params=_params(("parallel", "arbitrary")),
        name=name,
    )(h, w_t, *extras)


def _hg_constants(c):
    levels = []
    h = c // 2
    while h >= 1:
        levels.append(h)
        h //= 2
    t = np.arange(c)[:, None]
    u = np.arange(c)[None, :]
    mats = [(u <= t)]
    masks = []
    for h in levels:
        blk = 2 * h
        mid = (t // blk) * blk + h
        second = (t % blk) >= h
        mats.append(np.where(second, (u >= mid) & (u <= t), (u > t) & (u < mid)))
        s = u
        masks.append(((t // blk) == (s // blk)) & second & ((s % blk) < h))
    masks.append(t == u)
    pm = np.concatenate(mats, axis=0).astype(np.float32)
    mk = np.stack(masks, axis=0).astype(np.float32)
    return len(levels), pm, mk


def _blockdiag(a, b):
    top = jnp.concatenate([a, jnp.zeros((a.shape[0], b.shape[1]), a.dtype)], axis=1)
    bot = jnp.concatenate([jnp.zeros((b.shape[0], a.shape[1]), b.dtype), b], axis=1)
    return jnp.concatenate([top, bot], axis=0)


def _hgrn2_kernel(q_ref, lf_ref, i_ref, g_ref, nw_ref, pm_ref, mk_ref, o_ref, st_ref, ex_ref,
                  *, chunk, nlev):
    c = chunk

    @pl.when(pl.program_id(1) == 0)
    def _():
        st_ref[...] = jnp.zeros_like(st_ref)

    nw = nw_ref[...]
    pm = pm_ref[...]
    nh = st_ref.shape[0]
    dk = HG_DK

    def body(ci, carry):
        r0 = pl.multiple_of(ci * c, c)
        rows = pl.ds(r0, c)
        for p in range(nh // 2):
            ha, hb = 2 * p, 2 * p + 1
            sa = slice(ha * dk, (ha + 1) * dk)
            sb = slice(hb * dk, (hb + 1) * dk)
            ps = slice(ha * dk, (hb + 1) * dk)
            lo, hi = slice(0, dk), slice(dk, 2 * dk)
            qp = q_ref[rows, ps]
            lfp = lf_ref[rows, ps]
            kp = (1.0 - jnp.exp2(lfp.astype(F32))).astype(BF16)
            ex_ref[:, ps] = _dot(pm, lfp)
            b = ex_ref[0:c, ps]
            b_last = b[c - 1:c, :]
            st_a = st_ref[ha]
            st_b = st_ref[hb]
            o = _dot_nt(qp * jnp.exp2(b).astype(BF16),
                        _blockdiag(st_a.astype(BF16), st_b.astype(BF16)))
            sc = mk_ref[nlev] * _dot_nt(qp, _blockdiag(kp[:, lo], kp[:, hi])).astype(BF16)
            for l in range(nlev):
                e = jnp.exp2(ex_ref[(l + 1) * c:(l + 2) * c, ps]).astype(BF16)
                ke = kp * e
                s = _dot_nt(qp * e, _blockdiag(ke[:, lo], ke[:, hi]))
                sc = sc + mk_ref[l] * s.astype(BF16)
            o = o + _dot(sc, _blockdiag(i_ref[rows, sa], i_ref[rows, sb]))
            kdec = kp * jnp.exp2(b_last - b).astype(BF16)
            sdec = jnp.exp2(b_last)
            st_ref[ha] = st_a * sdec[:, lo] + _dot_tn(i_ref[rows, sa], kdec[:, lo])
            st_ref[hb] = st_b * sdec[:, hi] + _dot_tn(i_ref[rows, sb], kdec[:, hi])
            o_ref[rows, sa] = (_rms(o[:, lo], nw) * g_ref[rows, sa].astype(F32)).astype(o_ref.dtype)
            o_ref[rows, sb] = (_rms(o[:, hi], nw) * g_ref[rows, sb].astype(F32)).astype(o_ref.dtype)
        return carry

    lax.fori_loop(0, q_ref.shape[0] // c, body, 0)


def _hgrn2(qh, lf, v, gate, norm_w, n_heads, tb, chunk, hps):
    t = qh.shape[0]
    gate_off = 0
    nlev, pm, mk = _hg_constants(chunk)
    pm = jnp.asarray(pm, BF16)
    mk = jnp.asarray(np.concatenate([mk, mk], axis=2), BF16)
    w = hps * HG_DK
    nblk = n_heads // hps
    col = lambda off: (lambda h, i: (i, off // w + h))
    const2 = lambda h, i: (0, 0)
    return pl.pallas_call(
        functools.partial(_hgrn2_kernel, chunk=chunk, nlev=nlev),
        grid=(nblk, t // tb),
        in_specs=[
            pl.BlockSpec((tb, w), col(0)),
            pl.BlockSpec((tb, w), col(0)),
            pl.BlockSpec((tb, w), col(0)),
            pl.BlockSpec((tb, w), col(gate_off)),
            pl.BlockSpec((1, HG_DV), const2),
            pl.BlockSpec(pm.shape, const2),
            pl.BlockSpec(mk.shape, lambda h, i: (0, 0, 0)),
        ],
        out_specs=pl.BlockSpec((tb, w), lambda h, i: (i, h)),
        out_shape=jax.ShapeDtypeStruct((t, n_heads * HG_DV), BF16),
        scratch_shapes=[pltpu.VMEM((hps, HG_DV, HG_DK), F32),
                        pltpu.VMEM(((nlev + 1) * chunk, w), F32)],
        compiler_params=_params(("parallel", "arbitrary")),
        name="hgrn2_scan",
    )(qh, lf, v, gate, norm_w, pm, mk)


def _ssd_expand_mats(hpg):
    ea = np.zeros((128, hpg * SSM_HEADDIM), np.float32)
    ed = np.zeros((128, hpg * SSM_HEADDIM), np.float32)
    for h in range(hpg):
        ea[h, h * SSM_HEADDIM:(h + 1) * SSM_HEADDIM] = 1.0
        ed[hpg + h, h * SSM_HEADDIM:(h + 1) * SSM_HEADDIM] = 1.0
    return ea, ed


def _ssd_kernel(z_ref, x_ref, b_ref, c_ref, dt_ref, dtb_ref, alog_ref, dskip_ref, nw_ref,
                triu_ref, ea_ref, ed_ref, o_ref, s_ref):
    @pl.when(pl.program_id(1) == 0)
    def _():
        s_ref[...] = jnp.zeros_like(s_ref)

    gw = s_ref.shape[2]
    n = s_ref.shape[1]
    for gi in range(s_ref.shape[0]):
        _ssd_group(z_ref.at[:, gi * gw:(gi + 1) * gw], x_ref.at[:, gi * gw:(gi + 1) * gw],
                   b_ref.at[:, gi * n:(gi + 1) * n], c_ref.at[:, gi * n:(gi + 1) * n],
                   dt_ref.at[gi], dtb_ref.at[gi], alog_ref.at[gi], dskip_ref.at[gi], nw_ref.at[gi],
                   triu_ref, ea_ref, ed_ref, o_ref.at[:, gi * gw:(gi + 1) * gw], s_ref.at[gi])


def _ssd_group(z_ref, x_ref, b_ref, c_ref, dt_ref, dtb_ref, alog_ref, dskip_ref, nw_ref,
               triu_ref, ea_ref, ed_ref, o_ref, s_ref):
    q = x_ref.shape[0]
    hpg = dt_ref.shape[0]
    xs = x_ref[...].astype(F32)
    bm16 = b_ref[...]
    cm16 = c_ref[...]

    dt_r = jax.nn.softplus(dt_ref[...] + dtb_ref[...])
    da_r = dt_r * (-jnp.exp(alog_ref[...]) * LOG2E)
    da_hi, da_lo = _split_bf16(da_r)
    triu = triu_ref[...]
    acum_r = _dot(da_hi, triu) + _dot(da_lo, triu)
    pad = jnp.zeros((128 - 2 * hpg, q), F32)
    cols = jnp.concatenate([acum_r, dt_r, pad], axis=0).T

    lane = lax.broadcasted_iota(jnp.int32, (q, 128), 1)
    acol = jnp.where(lane < hpg, cols, 0.0)
    a_last = acol[q - 1:q, :]
    dt_x = _dot(cols.astype(BF16), ed_ref[...])
    ea_x = _dot(jnp.exp2(acol).astype(BF16), ea_ref[...])
    wd_x = _dot(jnp.exp2(a_last - acol).astype(BF16), ea_ref[...])
    sd_x = _dot(jnp.broadcast_to(jnp.exp2(a_last), (8, 128)).astype(BF16), ea_ref[...])[0:1, :]

    xdt = xs * dt_x
    xdt16 = xdt.astype(BF16)
    cb16 = _dot_nt(cm16, bm16).astype(BF16)
    cs = _dot(cm16, s_ref[...].astype(BF16))
    row = lax.broadcasted_iota(jnp.int32, (q, q), 0)
    colid = lax.broadcasted_iota(jnp.int32, (q, q), 1)
    causal = row >= colid
    lo_half = lane < SSM_HEADDIM

    y_parts = []
    for p in range(hpg // 2):
        ys = []
        for h in (2 * p, 2 * p + 1):
            dm = cols[:, h:h + 1] - acum_r[h:h + 1, :]
            lm16 = jnp.exp2(jnp.where(causal, dm, -1e30)).astype(BF16)
            ys.append(_dot(cb16 * lm16, xdt16[:, p * 128:(p + 1) * 128]))
        y_parts.append(jnp.where(lo_half, ys[0], ys[1]))

    y = jnp.concatenate(y_parts, axis=1) + cs * ea_x
    s_ref[...] = s_ref[...] * sd_x + _dot_tn(bm16, (xdt * wd_x).astype(BF16))

    y = y + dskip_ref[...] * xs
    y = y * z_ref[...].astype(F32)
    o_ref[...] = _rms(y, nw_ref[...]).astype(o_ref.dtype)


def _ssd(zs, xbc, dt_hm, dt_bias, a_log, d_skip, norm_w, d_inner):
    t = xbc.shape[0]
    z_off = xbc_off = 0
    q = SSD_CHUNK
    g = SSM_GROUPS
    n = SSM_DSTATE
    gw = d_inner // g
    hpg = gw // SSM_HEADDIM
    triu = jnp.asarray(np.triu(np.ones((q, q), np.float32)), BF16)
    ea, ed = (jnp.asarray(m, BF16) for m in _ssd_expand_mats(hpg))
    c2 = lambda gi, i: (0, 0)
    gs = SSD_GROUPS_PER_STEP
    assert g % gs == 0 and d_inner % (gs * n) == 0, "B / C column blocks must be block-aligned"
    return pl.pallas_call(
        _ssd_kernel,
        grid=(g // gs, t // q),
        in_specs=[
            pl.BlockSpec((q, gs * gw), lambda gi, i: (i, z_off // (gs * gw) + gi)),
            pl.BlockSpec((q, gs * gw), lambda gi, i: (i, xbc_off // (gs * gw) + gi)),
            pl.BlockSpec((q, gs * n), lambda gi, i: (i, (xbc_off + d_inner) // (gs * n) + gi)),
            pl.BlockSpec((q, gs * n), lambda gi, i: (i, (xbc_off + d_inner + g * n) // (gs * n) + gi)),
            pl.BlockSpec((gs, hpg, q), lambda gi, i: (gi, 0, i)),
            pl.BlockSpec((gs, hpg, 1), lambda gi, i: (gi, 0, 0)),
            pl.BlockSpec((gs, hpg, 1), lambda gi, i: (gi, 0, 0)),
            pl.BlockSpec((gs, 1, gw), lambda gi, i: (gi, 0, 0)),
            pl.BlockSpec((gs, 1, gw), lambda gi, i: (gi, 0, 0)),
            pl.BlockSpec((q, q), c2),
            pl.BlockSpec(ea.shape, c2),
            pl.BlockSpec(ed.shape, c2),
        ],
        out_specs=pl.BlockSpec((q, gs * gw), lambda gi, i: (i, gi)),
        out_shape=jax.ShapeDtypeStruct((t, d_inner), BF16),
        scratch_shapes=[pltpu.VMEM((gs, n, gw), F32)],
        compiler_params=_params(("parallel", "arbitrary")),
        name="ssd_scan",
    )(zs, xbc, xbc, xbc, dt_hm, dt_bias, a_log, d_skip, norm_w, triu, ea, ed)


def _merge_kernel(yh_ref, ys_ref, wh_ref, ws_ref, gh_ref, gs_ref, o_ref):
    a = _dot(yh_ref[...], wh_ref[...])
    b = _dot(ys_ref[...], ws_ref[...])
    gh = _sigmoid(gh_ref[...].astype(F32))
    gs = _sigmoid(gs_ref[...].astype(F32))
    o_ref[...] = (gh * a + gs * b).astype(o_ref.dtype)


def _merge(y_hg, y_ssm, w_hg, w_ssm, gates, tm, tn):
    t = y_hg.shape[0]
    d = w_hg.shape[1]
    return pl.pallas_call(
        _merge_kernel,
        grid=(t // tm, d // tn),
        in_specs=[
            pl.BlockSpec((tm, y_hg.shape[1]), lambda m, j: (m, 0)),
            pl.BlockSpec((tm, y_ssm.shape[1]), lambda m, j: (m, 0)),
            pl.BlockSpec((w_hg.shape[0], tn), lambda m, j: (0, j)),
            pl.BlockSpec((w_ssm.shape[0], tn), lambda m, j: (0, j)),
            pl.BlockSpec((tm, tn), lambda m, j: (m, j)),
            pl.BlockSpec((tm, tn), lambda m, j: (m, d // tn + j)),
        ],
        out_specs=pl.BlockSpec((tm, tn), lambda m, j: (m, j)),
        out_shape=jax.ShapeDtypeStruct((t, d), BF16),
        compiler_params=_params(("parallel", "arbitrary")),
        name="branch_merge",
    )(y_hg, y_ssm, w_hg, w_ssm, gates, gates)


def _out_kernel(a_ref, w_ref, x_ref, post_ref, pre_ref, x1_ref, h_ref):
    r = _dot(a_ref[...], w_ref[...])
    x1 = x_ref[...] + _rms(r, post_ref[...])
    x1_ref[...] = x1
    h_ref[...] = _rms(x1, pre_ref[...]).astype(BF16)


def _out_proj(mixed, w_out, x, post_w, pre_w, tm):
    t, d = x.shape
    c2 = lambda m: (0, 0)
    row = lambda m: (m, 0)
    return pl.pallas_call(
        _out_kernel,
        grid=(t // tm,),
        in_specs=[
            pl.BlockSpec((tm, d), row),
            pl.BlockSpec((d, d), c2),
            pl.BlockSpec((tm, d), row),
            pl.BlockSpec((1, d), c2),
            pl.BlockSpec((1, d), c2),
        ],
        out_specs=[pl.BlockSpec((tm, d), row), pl.BlockSpec((tm, d), row)],
        out_shape=[jax.ShapeDtypeStruct((t, d), F32), jax.ShapeDtypeStruct((t, d), BF16)],
        compiler_params=_params(("parallel",)),
        name="out_proj",
    )(mixed, w_out, x, post_w, pre_w)


def _ffn_up_kernel(h_ref, wg_ref, wu_ref, cw_ref, cb_ref, o_ref, ext_ref, wgb_ref, wub_ref):
    @pl.when(pl.program_id(1) == 0)
    def _():
        wgb_ref[...] = wg_ref[...].astype(BF16)
        wub_ref[...] = wu_ref[...].astype(BF16)
        ext_ref[0:CARRY, :] = jnp.zeros((CARRY, ext_ref.shape[1]), F32)

    h = h_ref[...]
    gate = _dot(h, wgb_ref[...])
    up = _dot(h, wub_ref[...])
    conv = _conv_taps(ext_ref, gate, cw_ref[...], cb_ref[...], FFN_CONV)
    o_ref[...] = (jax.nn.gelu(conv, approximate=True) * up).astype(o_ref.dtype)


def _ffn_up(h, w_up, conv_w, conv_b, d_ff, tm, tn):
    t, d = h.shape
    k = conv_w.shape[0]
    return pl.pallas_call(
        _ffn_up_kernel,
        grid=(d_ff // tn, t // tm),
        in_specs=[
            pl.BlockSpec((tm, d), lambda j, m: (m, 0)),
            pl.BlockSpec((d, tn), lambda j, m: (0, j)),
            pl.BlockSpec((d, tn), lambda j, m: (0, d_ff // tn + j)),
            pl.BlockSpec((k, tn), lambda j, m: (0, j)),
            pl.BlockSpec((1, tn), lambda j, m: (0, j)),
        ],
        out_specs=pl.BlockSpec((tm, tn), lambda j, m: (m, j)),
        out_shape=jax.ShapeDtypeStruct((t, d_ff), BF16),
        scratch_shapes=[pltpu.VMEM((CARRY + tm, tn), F32),
                        pltpu.VMEM((d, tn), BF16), pltpu.VMEM((d, tn), BF16)],
        compiler_params=_params(("parallel", "arbitrary")),
        name="ffn_up",
    )(h, w_up, w_up, conv_w, conv_b)


def _ffn_down_kernel(a_ref, w_ref, x_ref, post_ref, o_ref):
    kstep = pl.program_id(1)

    @pl.when(kstep == 0)
    def _():
        o_ref[...] = jnp.zeros_like(o_ref)

    d = o_ref.shape[1]
    slabs = [slice(c, c + FFN_DOWN_SLAB) for c in range(0, d, FFN_DOWN_SLAB)]
    a = a_ref[...]
    for cs in slabs:
        o_ref[:, cs] += _dot(a, w_ref[:, cs])

    @pl.when(kstep == pl.num_programs(1) - 1)
    def _():
        ssq = sum(jnp.sum(o_ref[:, cs] * o_ref[:, cs], axis=-1, keepdims=True) for cs in slabs)
        inv = lax.rsqrt(ssq / d + EPS)
        for cs in slabs:
            o_ref[:, cs] = x_ref[:, cs] + o_ref[:, cs] * inv * post_ref[:, cs]


def _ffn_down(act, w_down, x1, post_w, tm, tk):
    t, d = x1.shape
    d_ff = act.shape[1]
    return pl.pallas_call(
        _ffn_down_kernel,
        grid=(t // tm, d_ff // tk),
        in_specs=[
            pl.BlockSpec((tm, tk), lambda m, k: (m, k)),
            pl.BlockSpec((tk, d), lambda m, k: (k, 0)),
            pl.BlockSpec((tm, d), lambda m, k: (m, 0)),
            pl.BlockSpec((1, d), lambda m, k: (0, 0)),
        ],
        out_specs=pl.BlockSpec((tm, d), lambda m, k: (m, 0)),
        out_shape=jax.ShapeDtypeStruct((t, d), F32),
        compiler_params=_params(("parallel", "arbitrary"), VMEM_LIMIT_LARGE),
        name="ffn_down",
    )(act, w_down, x1, post_w)


def _layer(x, w_in, mix_pre, mix_post, lb_table, hg_norm, conv_w, conv_b, dt_bias, a_log, d_skip,
           ssm_norm, w_hg, w_ssm, w_out, ffn_pre, ffn_post, w_up, ffn_cw, ffn_cb, w_down):
    t, d = x.shape
    hg_v = w_hg.shape[0]
    hg_heads = hg_v // HG_DV
    d_inner = w_ssm.shape[0]
    ssm_heads = a_log.shape[0]
    hpg = ssm_heads // SSM_GROUPS
    gw = d_inner // SSM_GROUPS
    d_ff = w_down.shape[0]
    conv_dim = conv_w.shape[1]

    main_n = 4 * hg_v + d_inner + conv_dim
    dt_off = main_n
    gate_off = dt_off + ssm_heads
    z_off = 4 * hg_v
    xbc_off = z_off + d_inner

    w_t = w_in.T
    h, dt_t = _norm_dt(x, mix_pre.reshape(1, d), w_t[dt_off:gate_off], tm=512)
    tm, tn = INPROJ_ROWS, INPROJ_TILE
    qh = _seg_proj(h, w_t, 0, hg_v, "qscale", (), tm, tn, "inproj_q")
    lf = _seg_proj(h, w_t, hg_v, hg_v, "logf", (lb_table,), tm, tn, "inproj_f")
    v = _proj_nt(h, w_t, 2 * hg_v, hg_v, tm, tn, "inproj_i")
    go = _seg_proj(h, w_t, 3 * hg_v, hg_v, "silu", (), tm, tn, "inproj_g")
    zs = _seg_proj(h, w_t, z_off, d_inner, "silu", (), tm, tn, "inproj_z")
    xbc = _seg_proj(h, w_t, xbc_off, conv_dim, "conv", (conv_w, conv_b.reshape(1, conv_dim)), tm, tn,
                    "inproj_xbc")
    gates = _proj_nt(h, w_t, gate_off, 2 * d, tm, tn, "inproj_gates")

    y_hg = _hgrn2(qh, lf, v, go, hg_norm.reshape(1, HG_DV), hg_heads, HG_BLOCK, HG_CHUNK,
                  HG_HEADS_PER_STEP)

    dt_hm = dt_t.reshape(SSM_GROUPS, hpg, t)
    y_ssm = _ssd(
        zs, xbc, dt_hm,
        dt_bias.reshape(SSM_GROUPS, hpg, 1), a_log.reshape(SSM_GROUPS, hpg, 1),
        jnp.repeat(d_skip, SSM_HEADDIM).reshape(SSM_GROUPS, 1, gw),
        ssm_norm.reshape(SSM_GROUPS, 1, gw), d_inner)

    mixed = _merge(y_hg, y_ssm, w_hg.astype(BF16), w_ssm.astype(BF16), gates, tm=1024, tn=512)
    x1, h2 = _out_proj(mixed, w_out.astype(BF16), x, mix_post.reshape(1, d), ffn_pre.reshape(1, d), tm=512)
    act = _ffn_up(h2, w_up, ffn_cw, ffn_cb.reshape(1, d_ff), d_ff, tm=1024, tn=512)
    return _ffn_down(act, w_down.astype(BF16), x1, ffn_post.reshape(1, d), tm=1024, tk=FFN_DOWN_TK)


def kernel(x, w_in, mix_pre_norm, mix_post_norm, hg_lb_table, hg_out_norm, ssm_conv_w, ssm_conv_b,
           ssm_dt_bias, ssm_A_log, ssm_D, ssm_out_norm, w_branch_hg, w_branch_ssm, w_out,
           ffn_pre_norm, ffn_post_norm, ffn_w_up, ffn_conv_w, ffn_conv_b, ffn_w_down):
    bsz, t, d = x.shape
    depth = w_in.shape[0]
    assert depth == 1 and hg_lb_table.shape[0] == 2, "forget-gate lower bound is computed for one layer"
    outs = []
    for b in range(bsz):
        xb = x[b]
        for l in range(depth):
            xb = _layer(xb, w_in[l], mix_pre_norm[l], mix_post_norm[l], hg_lb_table, hg_out_norm[l],
                        ssm_conv_w[l], ssm_conv_b[l], ssm_dt_bias[l], ssm_A_log[l], ssm_D[l],
                        ssm_out_norm[l], w_branch_hg[l], w_branch_ssm[l], w_out[l],
                        ffn_pre_norm[l], ffn_post_norm[l], ffn_w_up[l], ffn_conv_w[l], ffn_conv_b[l],
                        ffn_w_down[l])
        outs.append(xb)
    return jnp.stack(outs, axis=0)
```

```python
import functools

import numpy as np
import jax
import jax.numpy as jnp
from jax import lax
from jax.experimental import pallas as pl
from jax.experimental.pallas import tpu as pltpu

F32 = jnp.float32
BF16 = jnp.bfloat16
EPS = 1e-6

HG_DK = 128
HG_DV = 128
SSM_HEADDIM = 64
SSM_GROUPS = 8
SSM_DSTATE = 128
SSM_CONV = 4
FFN_CONV = 3

INPROJ_TILE = 1024
INPROJ_ROWS = 1024
HG_CHUNK = 128
HG_BLOCK = 1024
HG_HEADS_PER_STEP = 8
SSD_CHUNK = 256
SSD_GROUPS_PER_STEP = 8
CARRY = 8
FFN_DOWN_TK = 1408
FFN_DOWN_SLAB = 512

VMEM_LIMIT = 48 * 1024 * 1024
VMEM_LIMIT_LARGE = 56 * 1024 * 1024
LOG2E = 1.4426950408889634


def _params(sem, vmem=VMEM_LIMIT):
    return pltpu.CompilerParams(dimension_semantics=sem, vmem_limit_bytes=vmem)


def _dot(a, b):
    return jnp.dot(a, b, preferred_element_type=F32)


def _dot_nt(a, b):
    return lax.dot_general(a, b, (((1,), (1,)), ((), ())), preferred_element_type=F32)


def _dot_tn(a, b):
    return lax.dot_general(a, b, (((0,), (0,)), ((), ())), preferred_element_type=F32)


def _split_bf16(a):
    hi = a.astype(BF16)
    lo = (a - hi.astype(F32)).astype(BF16)
    return hi, lo


def _sigmoid(a):
    return 0.5 * jnp.tanh(0.5 * a) + 0.5


def _silu(a):
    return a * _sigmoid(a)


def _rms(x, w):
    return x * lax.rsqrt(jnp.mean(x * x, axis=-1, keepdims=True) + EPS) * w


def _norm_dt_kernel(x_ref, nw_ref, wdt_ref, h_ref, dt_ref):
    h = _rms(x_ref[...], nw_ref[...]).astype(BF16)
    h_ref[...] = h
    dt_ref[...] = _dot_nt(wdt_ref[...].astype(BF16), h)


def _norm_dt(x, nw, wdt_t, tm):
    t, d = x.shape
    nh = wdt_t.shape[0]
    return pl.pallas_call(
        _norm_dt_kernel,
        grid=(t // tm,),
        in_specs=[
            pl.BlockSpec((tm, d), lambda m: (m, 0)),
            pl.BlockSpec((1, d), lambda m: (0, 0)),
            pl.BlockSpec((nh, d), lambda m: (0, 0)),
        ],
        out_specs=[pl.BlockSpec((tm, d), lambda m: (m, 0)),
                   pl.BlockSpec((nh, tm), lambda m: (0, m))],
        out_shape=[jax.ShapeDtypeStruct((t, d), BF16), jax.ShapeDtypeStruct((nh, t), F32)],
        compiler_params=_params(("parallel",)),
        name="norm_dt",
    )(x, nw, wdt_t)


def _proj_nt_kernel(h_ref, w_ref, o_ref, wb_ref):
    @pl.when(pl.program_id(1) == 0)
    def _():
        wb_ref[...] = w_ref[...].astype(BF16)

    o_ref[...] = _dot_nt(h_ref[...], wb_ref[...]).astype(o_ref.dtype)


def _proj_nt(h, w_t, row_off, n, tm, tn, name):
    t, d = h.shape
    if row_off % tn == 0:
        w_spec = pl.BlockSpec((tn, d), lambda j, m: (row_off // tn + j, 0))
    else:
        assert row_off % 8 == 0 and tn % 8 == 0
        w_spec = pl.BlockSpec((pl.Element(tn), pl.Element(d)),
                              lambda j, m: (pl.multiple_of(row_off + j * tn, 8), 0))
    return pl.pallas_call(
        _proj_nt_kernel,
        grid=(n // tn, t // tm),
        in_specs=[pl.BlockSpec((tm, d), lambda j, m: (m, 0)), w_spec],
        out_specs=pl.BlockSpec((tm, tn), lambda j, m: (m, j)),
        out_shape=jax.ShapeDtypeStruct((t, n), BF16),
        scratch_shapes=[pltpu.VMEM((tn, d), BF16)],
        compiler_params=_params(("parallel", "arbitrary")),
        name=name,
    )(h, w_t)


def _conv_taps(ext_ref, cur, w, b, ksize):
    q = cur.shape[0]
    ext_ref[CARRY:CARRY + q, :] = cur
    acc = cur * w[ksize - 1:ksize, :] + b
    for j in range(1, ksize):
        acc = acc + ext_ref[CARRY - j:CARRY - j + q, :] * w[ksize - 1 - j:ksize - j, :]
    ext_ref[0:CARRY, :] = ext_ref[q:q + CARRY, :]
    return acc


_STAGE_EXTRAS = {"qscale": 0, "silu": 0, "logf": 1, "conv": 2}


def _seg_proj_kernel(h_ref, w_ref, *rest, kind):
    nx = _STAGE_EXTRAS[kind]
    extra = rest[:nx]
    o_ref, wb_ref = rest[nx:nx + 2]
    first = pl.program_id(1) == 0

    @pl.when(first)
    def _():
        wb_ref[...] = w_ref[...].astype(BF16)

    if kind == "conv":
        ext_ref = rest[nx + 2]

        @pl.when(first)
        def _():
            ext_ref[0:CARRY, :] = jnp.zeros((CARRY, ext_ref.shape[1]), F32)

    acc = _dot_nt(h_ref[...], wb_ref[...])
    if kind == "qscale":
        out = _silu(acc) * HG_DK ** -0.5
    elif kind == "silu":
        out = _silu(acc)
    elif kind == "logf":
        tab = extra[0][...]
        te = jnp.exp(tab - jnp.max(tab, axis=0, keepdims=True))
        lb = te[0:1, :] / jnp.sum(te, axis=0, keepdims=True)
        out = jnp.log(lb + (1.0 - lb) * _sigmoid(acc)) * LOG2E
    else:
        out = _silu(_conv_taps(ext_ref, acc, extra[0][...], extra[1][...], SSM_CONV))
    o_ref[...] = out.astype(o_ref.dtype)


def _seg_proj(h, w_t, row_off, n, kind, extras, tm, tn, name):
    t, d = h.shape
    assert row_off % tn == 0 and len(extras) == _STAGE_EXTRAS[kind]
    scratch = [pltpu.VMEM((tn, d), BF16)]
    if kind == "conv":
        scratch.append(pltpu.VMEM((CARRY + tm, tn), F32))
    return pl.pallas_call(
        functools.partial(_seg_proj_kernel, kind=kind),
        grid=(n // tn, t // tm),
        in_specs=[
            pl.BlockSpec((tm, d), lambda j, m: (m, 0)),
            pl.BlockSpec((tn, d), lambda j, m: (row_off // tn + j, 0)),
        ] + [pl.BlockSpec((e.shape[0], tn), lambda j, m: (0, j)) for e in extras],
        out_specs=pl.BlockSpec((tm, tn), lambda j, m: (m, j)),
        out_shape=jax.ShapeDtypeStruct((t, n), BF16),
        scratch_shapes=scratch,
        compiler_params=_params(("parallel", "arbitrary")),
        name=name,
    )(h, w_t, *extras)


def _hg_constants(c):
    levels = []
    h = c // 2
    while h >= 1:
        levels.append(h)
        h //= 2
    t = np.arange(c)[:, None]
    u = np.arange(c)[None, :]
    mats = [(u <= t)]
    masks = []
    for h in levels:
        blk = 2 * h
        mid = (t // blk) * blk + h
        second = (t % blk) >= h
        mats.append(np.where(second, (u >= mid) & (u <= t), (u > t) & (u < mid)))
        s = u
        masks.append(((t // blk) == (s // blk)) & second & ((s % blk) < h))
    masks.append(t == u)
    pm = np.concatenate(mats, axis=0).astype(np.float32)
    mk = np.stack(masks, axis=0).astype(np.float32)
    return len(levels), pm, mk


def _blockdiag(a, b):
    top = jnp.concatenate([a, jnp.zeros((a.shape[0], b.shape[1]), a.dtype)], axis=1)
    bot = jnp.concatenate([jnp.zeros((b.shape[0], a.shape[1]), b.dtype), b], axis=1)
    return jnp.concatenate([top, bot], axis=0)


def _blockdiag_t(kpair):
    kt = kpair.T
    half = kt.shape[0] // 2
    return _blockdiag(kt[0:half], kt[half:])


def _hgrn2_kernel(q_ref, lf_ref, i_ref, g_ref, nw_ref, pm_ref, mk_ref, o_ref, st_ref, ex_ref,
                  *, chunk, nlev):
    c = chunk

    @pl.when(pl.program_id(1) == 0)
    def _():
        st_ref[...] = jnp.zeros_like(st_ref)

    nw = nw_ref[...]
    pm = pm_ref[...]
    nh = st_ref.shape[0]
    dk = HG_DK

    def body(ci, carry):
        r0 = pl.multiple_of(ci * c, c)
        rows = pl.ds(r0, c)
        for p in range(nh // 2):
            ha, hb = 2 * p, 2 * p + 1
            sa = slice(ha * dk, (ha + 1) * dk)
            sb = slice(hb * dk, (hb + 1) * dk)
            ps = slice(ha * dk, (hb + 1) * dk)
            lo, hi = slice(0, dk), slice(dk, 2 * dk)
            qp = q_ref[rows, ps]
            lfp = lf_ref[rows, ps]
            kp = (1.0 - jnp.exp2(lfp.astype(F32))).astype(BF16)
            ex_ref[:, ps] = _dot(pm, lfp)
            b = ex_ref[0:c, ps]
            b_last = b[c - 1:c, :]
            st_a = st_ref[ha]
            st_b = st_ref[hb]
            o = _dot_nt(qp * jnp.exp2(b).astype(BF16),
                        _blockdiag(st_a.astype(BF16), st_b.astype(BF16)))
            sc = mk_ref[nlev] * _dot(qp, _blockdiag_t(kp)).astype(BF16)
            for l in range(nlev):
                e = jnp.exp2(ex_ref[(l + 1) * c:(l + 2) * c, ps]).astype(BF16)
                ke = kp * e
                s = _dot(qp * e, _blockdiag_t(ke))
                sc = sc + mk_ref[l] * s.astype(BF16)
            o = o + _dot(sc, _blockdiag(i_ref[rows, sa], i_ref[rows, sb]))
            kdec = kp * jnp.exp2(b_last - b).astype(BF16)
            sdec = jnp.exp2(b_last)
            st_ref[ha] = st_a * sdec[:, lo] + _dot_tn(i_ref[rows, sa], kdec[:, lo])
            st_ref[hb] = st_b * sdec[:, hi] + _dot_tn(i_ref[rows, sb], kdec[:, hi])
            o_ref[rows, sa] = (_rms(o[:, lo], nw) * g_ref[rows, sa].astype(F32)).astype(o_ref.dtype)
            o_ref[rows, sb] = (_rms(o[:, hi], nw) * g_ref[rows, sb].astype(F32)).astype(o_ref.dtype)
        return carry

    lax.fori_loop(0, q_ref.shape[0] // c, body, 0)


def _hgrn2(qh, lf, v, gate, norm_w, n_heads, tb, chunk, hps):
    t = qh.shape[0]
    gate_off = 0
    nlev, pm, mk = _hg_constants(chunk)
    pm = jnp.asarray(pm, BF16)
    mk = jnp.asarray(np.concatenate([mk, mk], axis=2), BF16)
    w = hps * HG_DK
    nblk = n_heads // hps
    col = lambda off: (lambda h, i: (i, off // w + h))
    const2 = lambda h, i: (0, 0)
    return pl.pallas_call(
        functools.partial(_hgrn2_kernel, chunk=chunk, nlev=nlev),
        grid=(nblk, t // tb),
        in_specs=[
            pl.BlockSpec((tb, w), col(0)),
            pl.BlockSpec((tb, w), col(0)),
            pl.BlockSpec((tb, w), col(0)),
            pl.BlockSpec((tb, w), col(gate_off)),
            pl.BlockSpec((1, HG_DV), const2),
            pl.BlockSpec(pm.shape, const2),
            pl.BlockSpec(mk.shape, lambda h, i: (0, 0, 0)),
        ],
        out_specs=pl.BlockSpec((tb, w), lambda h, i: (i, h)),
        out_shape=jax.ShapeDtypeStruct((t, n_heads * HG_DV), BF16),
        scratch_shapes=[pltpu.VMEM((hps, HG_DV, HG_DK), F32),
                        pltpu.VMEM(((nlev + 1) * chunk, w), F32)],
        compiler_params=_params(("parallel", "arbitrary")),
        name="hgrn2_scan",
    )(qh, lf, v, gate, norm_w, pm, mk)


def _ssd_expand_mats(hpg):
    ea = np.zeros((128, hpg * SSM_HEADDIM), np.float32)
    ed = np.zeros((128, hpg * SSM_HEADDIM), np.float32)
    for h in range(hpg):
        ea[h, h * SSM_HEADDIM:(h + 1) * SSM_HEADDIM] = 1.0
        ed[hpg + h, h * SSM_HEADDIM:(h + 1) * SSM_HEADDIM] = 1.0
    return ea, ed


def _ssd_kernel(z_ref, x_ref, b_ref, c_ref, dt_ref, dtb_ref, alog_ref, dskip_ref, nw_ref,
                triu_ref, ea_ref, ed_ref, o_ref, s_ref):
    @pl.when(pl.program_id(1) == 0)
    def _():
        s_ref[...] = jnp.zeros_like(s_ref)

    gw = s_ref.shape[2]
    n = s_ref.shape[1]
    for gi in range(s_ref.shape[0]):
        _ssd_group(z_ref.at[:, gi * gw:(gi + 1) * gw], x_ref.at[:, gi * gw:(gi + 1) * gw],
                   b_ref.at[:, gi * n:(gi + 1) * n], c_ref.at[:, gi * n:(gi + 1) * n],
                   dt_ref.at[gi], dtb_ref.at[gi], alog_ref.at[gi], dskip_ref.at[gi], nw_ref.at[gi],
                   triu_ref, ea_ref, ed_ref, o_ref.at[:, gi * gw:(gi + 1) * gw], s_ref.at[gi])


def _ssd_group(z_ref, x_ref, b_ref, c_ref, dt_ref, dtb_ref, alog_ref, dskip_ref, nw_ref,
               triu_ref, ea_ref, ed_ref, o_ref, s_ref):
    q = x_ref.shape[0]
    hpg = dt_ref.shape[0]
    xs = x_ref[...].astype(F32)
    bm16 = b_ref[...]
    cm16 = c_ref[...]

    dt_r = jax.nn.softplus(dt_ref[...] + dtb_ref[...])
    da_r = dt_r * (-jnp.exp(alog_ref[...]) * LOG2E)
    da_hi, da_lo = _split_bf16(da_r)
    triu = triu_ref[...]
    acum_r = _dot(da_hi, triu) + _dot(da_lo, triu)
    pad = jnp.zeros((128 - 2 * hpg, q), F32)
    cols = jnp.concatenate([acum_r, dt_r, pad], axis=0).T

    lane = lax.broadcasted_iota(jnp.int32, (q, 128), 1)
    acol = jnp.where(lane < hpg, cols, 0.0)
    a_last = acol[q - 1:q, :]
    dt_x = _dot(cols.astype(BF16), ed_ref[...])
    ea_x = _dot(jnp.exp2(acol).astype(BF16), ea_ref[...])
    wd_x = _dot(jnp.exp2(a_last - acol).astype(BF16), ea_ref[...])
    sd_x = _dot(jnp.broadcast_to(jnp.exp2(a_last), (8, 128)).astype(BF16), ea_ref[...])[0:1, :]

    xdt = xs * dt_x
    xdt16 = xdt.astype(BF16)
    cb16 = _dot_nt(cm16, bm16).astype(BF16)
    cs = _dot(cm16, s_ref[...].astype(BF16))
    row = lax.broadcasted_iota(jnp.int32, (q, q), 0)
    colid = lax.broadcasted_iota(jnp.int32, (q, q), 1)
    causal = row >= colid
    lo_half = lane < SSM_HEADDIM

    y_parts = []
    for p in range(hpg // 2):
        ys = []
        for h in (2 * p, 2 * p + 1):
            dm = cols[:, h:h + 1] - acum_r[h:h + 1, :]
            lm16 = jnp.exp2(jnp.where(causal, dm, -1e30)).astype(BF16)
            ys.append(_dot(cb16 * lm16, xdt16[:, p * 128:(p + 1) * 128]))
        y_parts.append(jnp.where(lo_half, ys[0], ys[1]))

    y = jnp.concatenate(y_parts, axis=1) + cs * ea_x
    s_ref[...] = s_ref[...] * sd_x + _dot_tn(bm16, (xdt * wd_x).astype(BF16))

    y = y + dskip_ref[...] * xs
    y = y * z_ref[...].astype(F32)
    o_ref[...] = _rms(y, nw_ref[...]).astype(o_ref.dtype)


def _ssd(zs, xbc, dt_hm, dt_bias, a_log, d_skip, norm_w, d_inner):
    t = xbc.shape[0]
    z_off = xbc_off = 0
    q = SSD_CHUNK
    g = SSM_GROUPS
    n = SSM_DSTATE
    gw = d_inner // g
    hpg = gw // SSM_HEADDIM
    triu = jnp.asarray(np.triu(np.ones((q, q), np.float32)), BF16)
    ea, ed = (jnp.asarray(m, BF16) for m in _ssd_expand_mats(hpg))
    c2 = lambda gi, i: (0, 0)
    gs = SSD_GROUPS_PER_STEP
    assert g % gs == 0 and d_inner % (gs * n) == 0, "B / C column blocks must be block-aligned"
    return pl.pallas_call(
        _ssd_kernel,
        grid=(g // gs, t // q),
        in_specs=[
            pl.BlockSpec((q, gs * gw), lambda gi, i: (i, z_off // (gs * gw) + gi)),
            pl.BlockSpec((q, gs * gw), lambda gi, i: (i, xbc_off // (gs * gw) + gi)),
            pl.BlockSpec((q, gs * n), lambda gi, i: (i, (xbc_off + d_inner) // (gs * n) + gi)),
            pl.BlockSpec((q, gs * n), lambda gi, i: (i, (xbc_off + d_inner + g * n) // (gs * n) + gi)),
            pl.BlockSpec((gs, hpg, q), lambda gi, i: (gi, 0, i)),
            pl.BlockSpec((gs, hpg, 1), lambda gi, i: (gi, 0, 0)),
            pl.BlockSpec((gs, hpg, 1), lambda gi, i: (gi, 0, 0)),
            pl.BlockSpec((gs, 1, gw), lambda gi, i: (gi, 0, 0)),
            pl.BlockSpec((gs, 1, gw), lambda gi, i: (gi, 0, 0)),
            pl.BlockSpec((q, q), c2),
            pl.BlockSpec(ea.shape, c2),
            pl.BlockSpec(ed.shape, c2),
        ],
        out_specs=pl.BlockSpec((q, gs * gw), lambda gi, i: (i, gi)),
        out_shape=jax.ShapeDtypeStruct((t, d_inner), BF16),
        scratch_shapes=[pltpu.VMEM((gs, n, gw), F32)],
        compiler_params=_params(("parallel", "arbitrary")),
        name="ssd_scan",
    )(zs, xbc, xbc, xbc, dt_hm, dt_bias, a_log, d_skip, norm_w, triu, ea, ed)


def _merge_kernel(yh_ref, ys_ref, wh_ref, ws_ref, gh_ref, gs_ref, o_ref):
    a = _dot(yh_ref[...], wh_ref[...])
    b = _dot(ys_ref[...], ws_ref[...])
    gh = _sigmoid(gh_ref[...].astype(F32))
    gs = _sigmoid(gs_ref[...].astype(F32))
    o_ref[...] = (gh * a + gs * b).astype(o_ref.dtype)


def _merge(y_hg, y_ssm, w_hg, w_ssm, gates, tm, tn):
    t = y_hg.shape[0]
    d = w_hg.shape[1]
    return pl.pallas_call(
        _merge_kernel,
        grid=(t // tm, d // tn),
        in_specs=[
            pl.BlockSpec((tm, y_hg.shape[1]), lambda m, j: (m, 0)),
            pl.BlockSpec((tm, y_ssm.shape[1]), lambda m, j: (m, 0)),
            pl.BlockSpec((w_hg.shape[0], tn), lambda m, j: (0, j)),
            pl.BlockSpec((w_ssm.shape[0], tn), lambda m, j: (0, j)),
            pl.BlockSpec((tm, tn), lambda m, j: (m, j)),
            pl.BlockSpec((tm, tn), lambda m, j: (m, d // tn + j)),
        ],
        out_specs=pl.BlockSpec((tm, tn), lambda m, j: (m, j)),
        out_shape=jax.ShapeDtypeStruct((t, d), BF16),
        compiler_params=_params(("parallel", "arbitrary")),
        name="branch_merge",
    )(y_hg, y_ssm, w_hg, w_ssm, gates, gates)


def _out_kernel(a_ref, w_ref, x_ref, post_ref, pre_ref, x1_ref, h_ref):
    r = _dot(a_ref[...], w_ref[...])
    x1 = x_ref[...] + _rms(r, post_ref[...])
    x1_ref[...] = x1
    h_ref[...] = _rms(x1, pre_ref[...]).astype(BF16)


def _out_proj(mixed, w_out, x, post_w, pre_w, tm):
    t, d = x.shape
    c2 = lambda m: (0, 0)
    row = lambda m: (m, 0)
    return pl.pallas_call(
        _out_kernel,
        grid=(t // tm,),
        in_specs=[
            pl.BlockSpec((tm, d), row),
            pl.BlockSpec((d, d), c2),
            pl.BlockSpec((tm, d), row),
            pl.BlockSpec((1, d), c2),
            pl.BlockSpec((1, d), c2),
        ],
        out_specs=[pl.BlockSpec((tm, d), row), pl.BlockSpec((tm, d), row)],
        out_shape=[jax.ShapeDtypeStruct((t, d), F32), jax.ShapeDtypeStruct((t, d), BF16)],
        compiler_params=_params(("parallel",)),
        name="out_proj",
    )(mixed, w_out, x, post_w, pre_w)


def _ffn_up_kernel(h_ref, wg_ref, wu_ref, cw_ref, cb_ref, o_ref, ext_ref, wgb_ref, wub_ref):
    @pl.when(pl.program_id(1) == 0)
    def _():
        wgb_ref[...] = wg_ref[...].astype(BF16)
        wub_ref[...] = wu_ref[...].astype(BF16)
        ext_ref[0:CARRY, :] = jnp.zeros((CARRY, ext_ref.shape[1]), F32)

    h = h_ref[...]
    gate = _dot(h, wgb_ref[...])
    up = _dot(h, wub_ref[...])
    conv = _conv_taps(ext_ref, gate, cw_ref[...], cb_ref[...], FFN_CONV)
    o_ref[...] = (jax.nn.gelu(conv, approximate=True) * up).astype(o_ref.dtype)


def _ffn_up(h, w_up, conv_w, conv_b, d_ff, tm, tn):
    t, d = h.shape
    k = conv_w.shape[0]
    return pl.pallas_call(
        _ffn_up_kernel,
        grid=(d_ff // tn, t // tm),
        in_specs=[
            pl.BlockSpec((tm, d), lambda j, m: (m, 0)),
            pl.BlockSpec((d, tn), lambda j, m: (0, j)),
            pl.BlockSpec((d, tn), lambda j, m: (0, d_ff // tn + j)),
            pl.BlockSpec((k, tn), lambda j, m: (0, j)),
            pl.BlockSpec((1, tn), lambda j, m: (0, j)),
        ],
        out_specs=pl.BlockSpec((tm, tn), lambda j, m: (m, j)),
        out_shape=jax.ShapeDtypeStruct((t, d_ff), BF16),
        scratch_shapes=[pltpu.VMEM((CARRY + tm, tn), F32),
                        pltpu.VMEM((d, tn), BF16), pltpu.VMEM((d, tn), BF16)],
        compiler_params=_params(("parallel", "arbitrary")),
        name="ffn_up",
    )(h, w_up, w_up, conv_w, conv_b)


def _ffn_down_kernel(a_ref, w_ref, x_ref, post_ref, o_ref):
    kstep = pl.program_id(1)

    @pl.when(kstep == 0)
    def _():
        o_ref[...] = jnp.zeros_like(o_ref)

    d = o_ref.shape[1]
    slabs = [slice(c, c + FFN_DOWN_SLAB) for c in range(0, d, FFN_DOWN_SLAB)]
    a = a_ref[...]
    for cs in slabs:
        o_ref[:, cs] += _dot(a, w_ref[:, cs])

    @pl.when(kstep == pl.num_programs(1) - 1)
    def _():
        ssq = sum(jnp.sum(o_ref[:, cs] * o_ref[:, cs], axis=-1, keepdims=True) for cs in slabs)
        inv = lax.rsqrt(ssq / d + EPS)
        for cs in slabs:
            o_ref[:, cs] = x_ref[:, cs] + o_ref[:, cs] * inv * post_ref[:, cs]


def _ffn_down(act, w_down, x1, post_w, tm, tk):
    t, d = x1.shape
    d_ff = act.shape[1]
    return pl.pallas_call(
        _ffn_down_kernel,
        grid=(t // tm, d_ff // tk),
        in_specs=[
            pl.BlockSpec((tm, tk), lambda m, k: (m, k)),
            pl.BlockSpec((tk, d), lambda m, k: (k, 0)),
            pl.BlockSpec((tm, d), lambda m, k: (m, 0)),
            pl.BlockSpec((1, d), lambda m, k: (0, 0)),
        ],
        out_specs=pl.BlockSpec((tm, d), lambda m, k: (m, 0)),
        out_shape=jax.ShapeDtypeStruct((t, d), F32),
        compiler_params=_params(("parallel", "arbitrary"), VMEM_LIMIT_LARGE),
        name="ffn_down",
    )(act, w_down, x1, post_w)


def _layer(x, w_in, mix_pre, mix_post, lb_table, hg_norm, conv_w, conv_b, dt_bias, a_log, d_skip,
           ssm_norm, w_hg, w_ssm, w_out, ffn_pre, ffn_post, w_up, ffn_cw, ffn_cb, w_down):
    t, d = x.shape
    hg_v = w_hg.shape[0]
    hg_heads = hg_v // HG_DV
    d_inner = w_ssm.shape[0]
    ssm_heads = a_log.shape[0]
    hpg = ssm_heads // SSM_GROUPS
    gw = d_inner // SSM_GROUPS
    d_ff = w_down.shape[0]
    conv_dim = conv_w.shape[1]

    main_n = 4 * hg_v + d_inner + conv_dim
    dt_off = main_n
    gate_off = dt_off + ssm_heads
    z_off = 4 * hg_v
    xbc_off = z_off + d_inner

    w_t = w_in.T
    h, dt_t = _norm_dt(x, mix_pre.reshape(1, d), w_t[dt_off:gate_off], tm=512)
    tm, tn = INPROJ_ROWS, INPROJ_TILE
    qh = _seg_proj(h, w_t, 0, hg_v, "qscale", (), tm, tn, "inproj_q")
    lf = _seg_proj(h, w_t, hg_v, hg_v, "logf", (lb_table,), tm, tn, "inproj_f")
    v = _proj_nt(h, w_t, 2 * hg_v, hg_v, tm, tn, "inproj_i")
    go = _seg_proj(h, w_t, 3 * hg_v, hg_v, "silu", (), tm, tn, "inproj_g")
    zs = _seg_proj(h, w_t, z_off, d_inner, "silu", (), tm, tn, "inproj_z")
    xbc = _seg_proj(h, w_t, xbc_off, conv_dim, "conv", (conv_w, conv_b.reshape(1, conv_dim)), tm, tn,
                    "inproj_xbc")
    gates = _proj_nt(h, w_t, gate_off, 2 * d, tm, tn, "inproj_gates")

    y_hg = _hgrn2(qh, lf, v, go, hg_norm.reshape(1, HG_DV), hg_heads, HG_BLOCK, HG_CHUNK,
                  HG_HEADS_PER_STEP)

    dt_hm = dt_t.reshape(SSM_GROUPS, hpg, t)
    y_ssm = _ssd(
        zs, xbc, dt_hm,
        dt_bias.reshape(SSM_GROUPS, hpg, 1), a_log.reshape(SSM_GROUPS, hpg, 1),
        jnp.repeat(d_skip, SSM_HEADDIM).reshape(SSM_GROUPS, 1, gw),
        ssm_norm.reshape(SSM_GROUPS, 1, gw), d_inner)

    mixed = _merge(y_hg, y_ssm, w_hg.astype(BF16), w_ssm.astype(BF16), gates, tm=1024, tn=512)
    x1, h2 = _out_proj(mixed, w_out.astype(BF16), x, mix_post.reshape(1, d), ffn_pre.reshape(1, d), tm=512)
    act = _ffn_up(h2, w_up, ffn_cw, ffn_cb.reshape(1, d_ff), d_ff, tm=1024, tn=512)
    return _ffn_down(act, w_down.astype(BF16), x1, ffn_post.reshape(1, d), tm=1024, tk=FFN_DOWN_TK)


def kernel(x, w_in, mix_pre_norm, mix_post_norm, hg_lb_table, hg_out_norm, ssm_conv_w, ssm_conv_b,
           ssm_dt_bias, ssm_A_log, ssm_D, ssm_out_norm, w_branch_hg, w_branch_ssm, w_out,
           ffn_pre_norm, ffn_post_norm, ffn_w_up, ffn_conv_w, ffn_conv_b, ffn_w_down):
    bsz, t, d = x.shape
    depth = w_in.shape[0]
    assert depth == 1 and hg_lb_table.shape[0] == 2, "forget-gate lower bound is computed for one layer"
    outs = []
    for b in range(bsz):
        xb = x[b]
        for l in range(depth):
            xb = _layer(xb, w_in[l], mix_pre_norm[l], mix_post_norm[l], hg_lb_table, hg_out_norm[l],
                        ssm_conv_w[l], ssm_conv_b[l], ssm_dt_bias[l], ssm_A_log[l], ssm_D[l],
                        ssm_out_norm[l], w_branch_hg[l], w_branch_ssm[l], w_out[l],
                        ffn_pre_norm[l], ffn_post_norm[l], ffn_w_up[l], ffn_conv_w[l], ffn_conv_b[l],
                        ffn_w_down[l])
        outs.append(xb)
    return jnp.stack(outs, axis=0)
```

```python
import functools

import numpy as np
import jax
import jax.numpy as jnp
from jax import lax
from jax.experimental import pallas as pl
from jax.experimental.pallas import tpu as pltpu

F32 = jnp.float32
BF16 = jnp.bfloat16
EPS = 1e-6

HG_DK = 128
HG_DV = 128
SSM_HEADDIM = 64
SSM_GROUPS = 8
SSM_DSTATE = 128
SSM_CONV = 4
FFN_CONV = 3

INPROJ_TILE = 1024
INPROJ_ROWS = 1024
HG_CHUNK = 128
HG_BLOCK = 1024
HG_HEADS_PER_STEP = 8
SSD_CHUNK = 256
SSD_GROUPS_PER_STEP = 8
CARRY = 8
FFN_DOWN_TK = 1408
FFN_DOWN_SLAB = 512

VMEM_LIMIT = 48 * 1024 * 1024
VMEM_LIMIT_LARGE = 56 * 1024 * 1024
LOG2E = 1.4426950408889634


def _params(sem, vmem=VMEM_LIMIT):
    return pltpu.CompilerParams(dimension_semantics=sem, vmem_limit_bytes=vmem)


def _dot(a, b):
    return jnp.dot(a, b, preferred_element_type=F32)


def _dot_nt(a, b):
    return lax.dot_general(a, b, (((1,), (1,)), ((), ())), preferred_element_type=F32)


def _dot_tn(a, b):
    return lax.dot_general(a, b, (((0,), (0,)), ((), ())), preferred_element_type=F32)


def _split_bf16(a):
    hi = a.astype(BF16)
    lo = (a - hi.astype(F32)).astype(BF16)
    return hi, lo


def _sigmoid(a):
    return 0.5 * jnp.tanh(0.5 * a) + 0.5


def _silu(a):
    return a * _sigmoid(a)


def _rms(x, w):
    return x * lax.rsqrt(jnp.mean(x * x, axis=-1, keepdims=True) + EPS) * w


def _norm_dt_kernel(x_ref, nw_ref, wdt_ref, h_ref, dt_ref):
    h = _rms(x_ref[...], nw_ref[...]).astype(BF16)
    h_ref[...] = h
    dt_ref[...] = _dot_nt(wdt_ref[...].astype(BF16), h)


def _norm_dt(x, nw, wdt_t, tm):
    t, d = x.shape
    nh = wdt_t.shape[0]
    return pl.pallas_call(
        _norm_dt_kernel,
        grid=(t // tm,),
        in_specs=[
            pl.BlockSpec((tm, d), lambda m: (m, 0)),
            pl.BlockSpec((1, d), lambda m: (0, 0)),
            pl.BlockSpec((nh, d), lambda m: (0, 0)),
        ],
        out_specs=[pl.BlockSpec((tm, d), lambda m: (m, 0)),
                   pl.BlockSpec((nh, tm), lambda m: (0, m))],
        out_shape=[jax.ShapeDtypeStruct((t, d), BF16), jax.ShapeDtypeStruct((nh, t), F32)],
        compiler_params=_params(("parallel",)),
        name="norm_dt",
    )(x, nw, wdt_t)


def _proj_nt_kernel(h_ref, w_ref, o_ref, wb_ref):
    @pl.when(pl.program_id(1) == 0)
    def _():
        wb_ref[...] = w_ref[...].astype(BF16)

    o_ref[...] = _dot_nt(h_ref[...], wb_ref[...]).astype(o_ref.dtype)


def _proj_nt(h, w_t, row_off, n, tm, tn, name):
    t, d = h.shape
    if row_off % tn == 0:
        w_spec = pl.BlockSpec((tn, d), lambda j, m: (row_off // tn + j, 0))
    else:
        assert row_off % 8 == 0 and tn % 8 == 0
        w_spec = pl.BlockSpec((pl.Element(tn), pl.Element(d)),
                              lambda j, m: (pl.multiple_of(row_off + j * tn, 8), 0))
    return pl.pallas_call(
        _proj_nt_kernel,
        grid=(n // tn, t // tm),
        in_specs=[pl.BlockSpec((tm, d), lambda j, m: (m, 0)), w_spec],
        out_specs=pl.BlockSpec((tm, tn), lambda j, m: (m, j)),
        out_shape=jax.ShapeDtypeStruct((t, n), BF16),
        scratch_shapes=[pltpu.VMEM((tn, d), BF16)],
        compiler_params=_params(("parallel", "arbitrary")),
        name=name,
    )(h, w_t)


def _conv_taps(ext_ref, cur, w, b, ksize):
    q = cur.shape[0]
    ext_ref[CARRY:CARRY + q, :] = cur
    acc = cur * w[ksize - 1:ksize, :] + b
    for j in range(1, ksize):
        acc = acc + ext_ref[CARRY - j:CARRY - j + q, :] * w[ksize - 1 - j:ksize - j, :]
    ext_ref[0:CARRY, :] = ext_ref[q:q + CARRY, :]
    return acc


_STAGE_EXTRAS = {"qscale": 0, "silu": 0, "logf": 1, "conv": 2}


def _seg_proj_kernel(h_ref, w_ref, *rest, kind):
    nx = _STAGE_EXTRAS[kind]
    extra = rest[:nx]
    o_ref, wb_ref = rest[nx:nx + 2]
    first = pl.program_id(1) == 0

    @pl.when(first)
    def _():
        wb_ref[...] = w_ref[...].astype(BF16)

    if kind == "conv":
        ext_ref = rest[nx + 2]

        @pl.when(first)
        def _():
            ext_ref[0:CARRY, :] = jnp.zeros((CARRY, ext_ref.shape[1]), F32)

    acc = _dot_nt(h_ref[...], wb_ref[...])
    if kind == "qscale":
        out = _silu(acc) * HG_DK ** -0.5
    elif kind == "silu":
        out = _silu(acc)
    elif kind == "logf":
        tab = extra[0][...]
        te = jnp.exp(tab - jnp.max(tab, axis=0, keepdims=True))
        lb = te[0:1, :] / jnp.sum(te, axis=0, keepdims=True)
        out = jnp.log(lb + (1.0 - lb) * _sigmoid(acc)) * LOG2E
    else:
        out = _silu(_conv_taps(ext_ref, acc, extra[0][...], extra[1][...], SSM_CONV))
    o_ref[...] = out.astype(o_ref.dtype)


def _seg_proj(h, w_t, row_off, n, kind, extras, tm, tn, name):
    t, d = h.shape
    assert row_off % tn == 0 and len(extras) == _STAGE_EXTRAS[kind]
    scratch = [pltpu.VMEM((tn, d), BF16)]
    if kind == "conv":
        scratch.append(pltpu.VMEM((CARRY + tm, tn), F32))
    return pl.pallas_call(
        functools.partial(_seg_proj_kernel, kind=kind),
        grid=(n // tn, t // tm),
        in_specs=[
            pl.BlockSpec((tm, d), lambda j, m: (m, 0)),
            pl.BlockSpec((tn, d), lambda j, m: (row_off // tn + j, 0)),
        ] + [pl.BlockSpec((e.shape[0], tn), lambda j, m: (0, j)) for e in extras],
        out_specs=pl.BlockSpec((tm, tn), lambda j, m: (m, j)),
        out_shape=jax.ShapeDtypeStruct((t, n), BF16),
        scratch_shapes=scratch,
        compiler_params=_params(("parallel", "arbitrary")),
        name=name,
    )(h, w_t, *extras)


def _hg_constants(c):
    levels = []
    h = c // 2
    while h >= 1:
        levels.append(h)
        h //= 2
    t = np.arange(c)[:, None]
    u = np.arange(c)[None, :]
    mats = [(u <= t)]
    masks = []
    for h in levels:
        blk = 2 * h
        mid = (t // blk) * blk + h
        second = (t % blk) >= h
        mats.append(np.where(second, (u >= mid) & (u <= t), (u > t) & (u < mid)))
        s = u
        masks.append(((t // blk) == (s // blk)) & second & ((s % blk) < h))
    masks.append(t == u)
    pm = np.concatenate(mats, axis=0).astype(np.float32)
    mk = np.stack(masks, axis=0).astype(np.float32)
    return len(levels), pm, mk


def _blockdiag(a, b):
    top = jnp.concatenate([a, jnp.zeros((a.shape[0], b.shape[1]), a.dtype)], axis=1)
    bot = jnp.concatenate([jnp.zeros((b.shape[0], a.shape[1]), b.dtype), b], axis=1)
    return jnp.concatenate([top, bot], axis=0)


def _blockdiag_t(kpair):
    kt = kpair.T
    half = kt.shape[0] // 2
    return _blockdiag(kt[0:half], kt[half:])


def _hgrn2_kernel(q_ref, lf_ref, i_ref, g_ref, nw_ref, pm_ref, mk_ref, *rest, chunk, nlev, ncast):
    c = chunk
    w_in_refs = rest[:ncast]
    o_ref = rest[ncast]
    w_out_refs = rest[ncast + 1:2 * ncast + 1]
    st_ref, ex_ref = rest[2 * ncast + 1:]
    for src, dst in zip(w_in_refs, w_out_refs):
        dst[...] = src[...].astype(BF16)

    @pl.when(pl.program_id(1) == 0)
    def _():
        st_ref[...] = jnp.zeros_like(st_ref)

    nw = nw_ref[...]
    pm = pm_ref[...]
    nh = st_ref.shape[0]
    dk = HG_DK

    def body(ci, carry):
        r0 = pl.multiple_of(ci * c, c)
        rows = pl.ds(r0, c)
        for p in range(nh // 2):
            ha, hb = 2 * p, 2 * p + 1
            sa = slice(ha * dk, (ha + 1) * dk)
            sb = slice(hb * dk, (hb + 1) * dk)
            ps = slice(ha * dk, (hb + 1) * dk)
            lo, hi = slice(0, dk), slice(dk, 2 * dk)
            qp = q_ref[rows, ps]
            lfp = lf_ref[rows, ps]
            kp = (1.0 - jnp.exp2(lfp.astype(F32))).astype(BF16)
            ex_ref[:, ps] = _dot(pm, lfp)
            b = ex_ref[0:c, ps]
            b_last = b[c - 1:c, :]
            st_a = st_ref[ha]
            st_b = st_ref[hb]
            o = _dot_nt(qp * jnp.exp2(b).astype(BF16),
                        _blockdiag(st_a.astype(BF16), st_b.astype(BF16)))
            sc = mk_ref[nlev] * _dot(qp, _blockdiag_t(kp)).astype(BF16)
            for l in range(nlev):
                e = jnp.exp2(ex_ref[(l + 1) * c:(l + 2) * c, ps]).astype(BF16)
                ke = kp * e
                s = _dot(qp * e, _blockdiag_t(ke))
                sc = sc + mk_ref[l] * s.astype(BF16)
            o = o + _dot(sc, _blockdiag(i_ref[rows, sa], i_ref[rows, sb]))
            kdec = kp * jnp.exp2(b_last - b).astype(BF16)
            sdec = jnp.exp2(b_last)
            st_ref[ha] = st_a * sdec[:, lo] + _dot_tn(i_ref[rows, sa], kdec[:, lo])
            st_ref[hb] = st_b * sdec[:, hi] + _dot_tn(i_ref[rows, sb], kdec[:, hi])
            o_ref[rows, sa] = (_rms(o[:, lo], nw) * g_ref[rows, sa].astype(F32)).astype(o_ref.dtype)
            o_ref[rows, sb] = (_rms(o[:, hi], nw) * g_ref[rows, sb].astype(F32)).astype(o_ref.dtype)
        return carry

    lax.fori_loop(0, q_ref.shape[0] // c, body, 0)


def _hgrn2(qh, lf, v, gate, norm_w, n_heads, tb, chunk, hps, cast_weights):
    t = qh.shape[0]
    gate_off = 0
    n_steps = (n_heads // hps) * (t // tb)
    for wgt in cast_weights:
        assert wgt.shape[0] % (16 * n_steps) == 0, "weight rows must split into bf16-tile-aligned slices"
    w_rows = [wgt.shape[0] // n_steps for wgt in cast_weights]
    w_specs = [pl.BlockSpec((r, wgt.shape[1]), lambda h, i: (h * (t // tb) + i, 0))
               for r, wgt in zip(w_rows, cast_weights)]
    nlev, pm, mk = _hg_constants(chunk)
    pm = jnp.asarray(pm, BF16)
    mk = jnp.asarray(np.concatenate([mk, mk], axis=2), BF16)
    w = hps * HG_DK
    nblk = n_heads // hps
    col = lambda off: (lambda h, i: (i, off // w + h))
    const2 = lambda h, i: (0, 0)
    return pl.pallas_call(
        functools.partial(_hgrn2_kernel, chunk=chunk, nlev=nlev, ncast=len(cast_weights)),
        grid=(nblk, t // tb),
        in_specs=[
            pl.BlockSpec((tb, w), col(0)),
            pl.BlockSpec((tb, w), col(0)),
            pl.BlockSpec((tb, w), col(0)),
            pl.BlockSpec((tb, w), col(gate_off)),
            pl.BlockSpec((1, HG_DV), const2),
            pl.BlockSpec(pm.shape, const2),
            pl.BlockSpec(mk.shape, lambda h, i: (0, 0, 0)),
        ] + w_specs,
        out_specs=[pl.BlockSpec((tb, w), lambda h, i: (i, h))] + w_specs,
        out_shape=[jax.ShapeDtypeStruct((t, n_heads * HG_DV), BF16)]
        + [jax.ShapeDtypeStruct(wgt.shape, BF16) for wgt in cast_weights],
        scratch_shapes=[pltpu.VMEM((hps, HG_DV, HG_DK), F32),
                        pltpu.VMEM(((nlev + 1) * chunk, w), F32)],
        compiler_params=_params(("parallel", "arbitrary")),
        name="hgrn2_scan",
    )(qh, lf, v, gate, norm_w, pm, mk, *cast_weights)


def _ssd_expand_mat(hpg):
    ea = np.zeros((128, hpg * SSM_HEADDIM), np.float32)
    for h in range(hpg):
        ea[h, h * SSM_HEADDIM:(h + 1) * SSM_HEADDIM] = 1.0
    return ea


def _ssd_kernel(z_ref, x_ref, b_ref, c_ref, dt_ref, dtb_ref, alog_ref, dskip_ref, nw_ref,
                triu_ref, ea_ref, o_ref, s_ref):
    @pl.when(pl.program_id(1) == 0)
    def _():
        s_ref[...] = jnp.zeros_like(s_ref)

    gw = s_ref.shape[2]
    n = s_ref.shape[1]
    for gi in range(s_ref.shape[0]):
        _ssd_group(z_ref.at[:, gi * gw:(gi + 1) * gw], x_ref.at[:, gi * gw:(gi + 1) * gw],
                   b_ref.at[:, gi * n:(gi + 1) * n], c_ref.at[:, gi * n:(gi + 1) * n],
                   dt_ref.at[gi], dtb_ref.at[gi], alog_ref.at[gi], dskip_ref.at[gi], nw_ref.at[gi],
                   triu_ref, ea_ref, o_ref.at[:, gi * gw:(gi + 1) * gw], s_ref.at[gi])


def _ssd_group(z_ref, x_ref, b_ref, c_ref, dt_ref, dtb_ref, alog_ref, dskip_ref, nw_ref,
               triu_ref, ea_ref, o_ref, s_ref):
    q = x_ref.shape[0]
    hpg = dt_ref.shape[0]
    xs = x_ref[...].astype(F32)
    bm16 = b_ref[...]
    cm16 = c_ref[...]

    dt_r = jax.nn.softplus(dt_ref[...] + dtb_ref[...])
    da_r = dt_r * (-jnp.exp(alog_ref[...]) * LOG2E)
    da_hi, da_lo = _split_bf16(da_r)
    triu = triu_ref[...]
    acum_r = _dot(da_hi, triu) + _dot(da_lo, triu)
    pad = jnp.zeros((128 - 2 * hpg, q), F32)
    cols = jnp.concatenate([acum_r, dt_r, pad], axis=0).T

    lane = lax.broadcasted_iota(jnp.int32, (q, 128), 1)
    acol = jnp.where(lane < hpg, cols, 0.0)
    a_last = acol[q - 1:q, :]
    dtwd = jnp.exp2(a_last - acol) * pltpu.roll(cols, 128 - hpg, axis=1)
    fx = _dot(jnp.concatenate([jnp.exp2(acol).astype(BF16), dtwd.astype(BF16)], axis=0), ea_ref[...])
    ea_x = fx[0:q]
    dtwd_x = fx[q:2 * q]
    sd_x = _dot(jnp.broadcast_to(jnp.exp2(a_last), (8, 128)).astype(BF16), ea_ref[...])[0:1, :]

    xs16 = x_ref[...]
    bmt = bm16.T
    cb16 = _dot(cm16, bmt).astype(BF16)
    cs = _dot(cm16, s_ref[...].astype(BF16))
    row = lax.broadcasted_iota(jnp.int32, (q, q), 0)
    colid = lax.broadcasted_iota(jnp.int32, (q, q), 1)
    causal = row >= colid
    lo_half = lane < SSM_HEADDIM

    y_parts = []
    for p in range(hpg // 2):
        ms = []
        for h in (2 * p, 2 * p + 1):
            dm = cols[:, h:h + 1] - acum_r[h:h + 1, :]
            lm16 = jnp.exp2(jnp.where(causal, dm, -1e30)).astype(BF16)
            ms.append(cb16 * lm16 * dt_r[h:h + 1, :].astype(BF16))
        yy = _dot(jnp.concatenate(ms, axis=0), xs16[:, p * 128:(p + 1) * 128])
        y_parts.append(jnp.where(lo_half, yy[0:q], yy[q:2 * q]))

    y = jnp.concatenate(y_parts, axis=1) + cs * ea_x
    s_ref[...] = s_ref[...] * sd_x + _dot(bmt, (xs * dtwd_x).astype(BF16))

    y = y + dskip_ref[...] * xs
    y = y * z_ref[...].astype(F32)
    o_ref[...] = _rms(y, nw_ref[...]).astype(o_ref.dtype)


def _ssd(zs, xbc, dt_hm, dt_bias, a_log, d_skip, norm_w, d_inner):
    t = xbc.shape[0]
    z_off = xbc_off = 0
    q = SSD_CHUNK
    g = SSM_GROUPS
    n = SSM_DSTATE
    gw = d_inner // g
    hpg = gw // SSM_HEADDIM
    triu = jnp.asarray(np.triu(np.ones((q, q), np.float32)), BF16)
    ea = jnp.asarray(_ssd_expand_mat(hpg), BF16)
    c2 = lambda gi, i: (0, 0)
    gs = SSD_GROUPS_PER_STEP
    assert g % gs == 0 and d_inner % (gs * n) == 0, "B / C column blocks must be block-aligned"
    return pl.pallas_call(
        _ssd_kernel,
        grid=(g // gs, t // q),
        in_specs=[
            pl.BlockSpec((q, gs * gw), lambda gi, i: (i, z_off // (gs * gw) + gi)),
            pl.BlockSpec((q, gs * gw), lambda gi, i: (i, xbc_off // (gs * gw) + gi)),
            pl.BlockSpec((q, gs * n), lambda gi, i: (i, (xbc_off + d_inner) // (gs * n) + gi)),
            pl.BlockSpec((q, gs * n), lambda gi, i: (i, (xbc_off + d_inner + g * n) // (gs * n) + gi)),
            pl.BlockSpec((gs, hpg, q), lambda gi, i: (gi, 0, i)),
            pl.BlockSpec((gs, hpg, 1), lambda gi, i: (gi, 0, 0)),
            pl.BlockSpec((gs, hpg, 1), lambda gi, i: (gi, 0, 0)),
            pl.BlockSpec((gs, 1, gw), lambda gi, i: (gi, 0, 0)),
            pl.BlockSpec((gs, 1, gw), lambda gi, i: (gi, 0, 0)),
            pl.BlockSpec((q, q), c2),
            pl.BlockSpec(ea.shape, c2),
        ],
        out_specs=pl.BlockSpec((q, gs * gw), lambda gi, i: (i, gi)),
        out_shape=jax.ShapeDtypeStruct((t, d_inner), BF16),
        scratch_shapes=[pltpu.VMEM((gs, n, gw), F32)],
        compiler_params=_params(("parallel", "arbitrary")),
        name="ssd_scan",
    )(zs, xbc, xbc, xbc, dt_hm, dt_bias, a_log, d_skip, norm_w, triu, ea)


def _merge_kernel(yh_ref, ys_ref, wh_ref, ws_ref, gh_ref, gs_ref, o_ref):
    a = _dot(yh_ref[...], wh_ref[...])
    b = _dot(ys_ref[...], ws_ref[...])
    gh = _sigmoid(gh_ref[...].astype(F32))
    gs = _sigmoid(gs_ref[...].astype(F32))
    o_ref[...] = (gh * a + gs * b).astype(o_ref.dtype)


def _merge(y_hg, y_ssm, w_hg, w_ssm, gates, tm, tn):
    t = y_hg.shape[0]
    d = w_hg.shape[1]
    return pl.pallas_call(
        _merge_kernel,
        grid=(t // tm, d // tn),
        in_specs=[
            pl.BlockSpec((tm, y_hg.shape[1]), lambda m, j: (m, 0)),
            pl.BlockSpec((tm, y_ssm.shape[1]), lambda m, j: (m, 0)),
            pl.BlockSpec((w_hg.shape[0], tn), lambda m, j: (0, j)),
            pl.BlockSpec((w_ssm.shape[0], tn), lambda m, j: (0, j)),
            pl.BlockSpec((tm, tn), lambda m, j: (m, j)),
            pl.BlockSpec((tm, tn), lambda m, j: (m, d // tn + j)),
        ],
        out_specs=pl.BlockSpec((tm, tn), lambda m, j: (m, j)),
        out_shape=jax.ShapeDtypeStruct((t, d), BF16),
        compiler_params=_params(("parallel", "arbitrary")),
        name="branch_merge",
    )(y_hg, y_ssm, w_hg, w_ssm, gates, gates)


def _out_kernel(a_ref, w_ref, x_ref, post_ref, pre_ref, x1_ref, h_ref):
    r = _dot(a_ref[...], w_ref[...])
    x1 = x_ref[...] + _rms(r, post_ref[...])
    x1_ref[...] = x1
    h_ref[...] = _rms(x1, pre_ref[...]).astype(BF16)


def _out_proj(mixed, w_out, x, post_w, pre_w, tm):
    t, d = x.shape
    c2 = lambda m: (0, 0)
    row = lambda m: (m, 0)
    return pl.pallas_call(
        _out_kernel,
        grid=(t // tm,),
        in_specs=[
            pl.BlockSpec((tm, d), row),
            pl.BlockSpec((d, d), c2),
            pl.BlockSpec((tm, d), row),
            pl.BlockSpec((1, d), c2),
            pl.BlockSpec((1, d), c2),
        ],
        out_specs=[pl.BlockSpec((tm, d), row), pl.BlockSpec((tm, d), row)],
        out_shape=[jax.ShapeDtypeStruct((t, d), F32), jax.ShapeDtypeStruct((t, d), BF16)],
        compiler_params=_params(("parallel",)),
        name="out_proj",
    )(mixed, w_out, x, post_w, pre_w)


def _ffn_up_kernel(h_ref, wg_ref, wu_ref, cw_ref, cb_ref, o_ref, ext_ref, wgb_ref, wub_ref):
    @pl.when(pl.program_id(1) == 0)
    def _():
        wgb_ref[...] = wg_ref[...].astype(BF16)
        wub_ref[...] = wu_ref[...].astype(BF16)
        ext_ref[0:CARRY, :] = jnp.zeros((CARRY, ext_ref.shape[1]), F32)

    h = h_ref[...]
    gate = _dot(h, wgb_ref[...])
    up = _dot(h, wub_ref[...])
    conv = _conv_taps(ext_ref, gate, cw_ref[...], cb_ref[...], FFN_CONV)
    o_ref[...] = (jax.nn.gelu(conv, approximate=True) * up).astype(o_ref.dtype)


def _ffn_up(h, w_up, conv_w, conv_b, d_ff, tm, tn):
    t, d = h.shape
    k = conv_w.shape[0]
    return pl.pallas_call(
        _ffn_up_kernel,
        grid=(d_ff // tn, t // tm),
        in_specs=[
            pl.BlockSpec((tm, d), lambda j, m: (m, 0)),
            pl.BlockSpec((d, tn), lambda j, m: (0, j)),
            pl.BlockSpec((d, tn), lambda j, m: (0, d_ff // tn + j)),
            pl.BlockSpec((k, tn), lambda j, m: (0, j)),
            pl.BlockSpec((1, tn), lambda j, m: (0, j)),
        ],
        out_specs=pl.BlockSpec((tm, tn), lambda j, m: (m, j)),
        out_shape=jax.ShapeDtypeStruct((t, d_ff), BF16),
        scratch_shapes=[pltpu.VMEM((CARRY + tm, tn), F32),
                        pltpu.VMEM((d, tn), BF16), pltpu.VMEM((d, tn), BF16)],
        compiler_params=_params(("parallel", "arbitrary")),
        name="ffn_up",
    )(h, w_up, w_up, conv_w, conv_b)


def _ffn_down_kernel(a_ref, w_ref, x_ref, post_ref, o_ref):
    kstep = pl.program_id(1)

    @pl.when(kstep == 0)
    def _():
        o_ref[...] = jnp.zeros_like(o_ref)

    d = o_ref.shape[1]
    slabs = [slice(c, c + FFN_DOWN_SLAB) for c in range(0, d, FFN_DOWN_SLAB)]
    a = a_ref[...]
    for cs in slabs:
        o_ref[:, cs] += _dot(a, w_ref[:, cs])

    @pl.when(kstep == pl.num_programs(1) - 1)
    def _():
        ssq = sum(jnp.sum(o_ref[:, cs] * o_ref[:, cs], axis=-1, keepdims=True) for cs in slabs)
        inv = lax.rsqrt(ssq / d + EPS)
        for cs in slabs:
            o_ref[:, cs] = x_ref[:, cs] + o_ref[:, cs] * inv * post_ref[:, cs]


def _ffn_down(act, w_down, x1, post_w, tm, tk):
    t, d = x1.shape
    d_ff = act.shape[1]
    return pl.pallas_call(
        _ffn_down_kernel,
        grid=(t // tm, d_ff // tk),
        in_specs=[
            pl.BlockSpec((tm, tk), lambda m, k: (m, k)),
            pl.BlockSpec((tk, d), lambda m, k: (k, 0)),
            pl.BlockSpec((tm, d), lambda m, k: (m, 0)),
            pl.BlockSpec((1, d), lambda m, k: (0, 0)),
        ],
        out_specs=pl.BlockSpec((tm, d), lambda m, k: (m, 0)),
        out_shape=jax.ShapeDtypeStruct((t, d), F32),
        compiler_params=_params(("parallel", "arbitrary"), VMEM_LIMIT_LARGE),
        name="ffn_down",
    )(act, w_down, x1, post_w)


def _layer(x, w_in, mix_pre, mix_post, lb_table, hg_norm, conv_w, conv_b, dt_bias, a_log, d_skip,
           ssm_norm, w_hg, w_ssm, w_out, ffn_pre, ffn_post, w_up, ffn_cw, ffn_cb, w_down):
    t, d = x.shape
    hg_v = w_hg.shape[0]
    hg_heads = hg_v // HG_DV
    d_inner = w_ssm.shape[0]
    ssm_heads = a_log.shape[0]
    hpg = ssm_heads // SSM_GROUPS
    gw = d_inner // SSM_GROUPS
    d_ff = w_down.shape[0]
    conv_dim = conv_w.shape[1]

    main_n = 4 * hg_v + d_inner + conv_dim
    dt_off = main_n
    gate_off = dt_off + ssm_heads
    z_off = 4 * hg_v
    xbc_off = z_off + d_inner

    w_t = w_in.T
    h, dt_t = _norm_dt(x, mix_pre.reshape(1, d), w_t[dt_off:gate_off], tm=512)
    tm, tn = INPROJ_ROWS, INPROJ_TILE
    qh = _seg_proj(h, w_t, 0, hg_v, "qscale", (), tm, tn, "inproj_q")
    lf = _seg_proj(h, w_t, hg_v, hg_v, "logf", (lb_table,), tm, tn, "inproj_f")
    v = _proj_nt(h, w_t, 2 * hg_v, hg_v, tm, tn, "inproj_i")
    go = _seg_proj(h, w_t, 3 * hg_v, hg_v, "silu", (), tm, tn, "inproj_g")
    zs = _seg_proj(h, w_t, z_off, d_inner, "silu", (), tm, tn, "inproj_z")
    xbc = _seg_proj(h, w_t, xbc_off, conv_dim, "conv", (conv_w, conv_b.reshape(1, conv_dim)), tm, tn,
                    "inproj_xbc")
    gates = _proj_nt(h, w_t, gate_off, 2 * d, tm, tn, "inproj_gates")

    y_hg, w_hg16, w_ssm16, w_out16, w_down16 = _hgrn2(
        qh, lf, v, go, hg_norm.reshape(1, HG_DV), hg_heads, HG_BLOCK, HG_CHUNK, HG_HEADS_PER_STEP,
        cast_weights=(w_hg, w_ssm, w_out, w_down))

    dt_hm = dt_t.reshape(SSM_GROUPS, hpg, t)
    y_ssm = _ssd(
        zs, xbc, dt_hm,
        dt_bias.reshape(SSM_GROUPS, hpg, 1), a_log.reshape(SSM_GROUPS, hpg, 1),
        jnp.repeat(d_skip, SSM_HEADDIM).reshape(SSM_GROUPS, 1, gw),
        ssm_norm.reshape(SSM_GROUPS, 1, gw), d_inner)

    mixed = _merge(y_hg, y_ssm, w_hg16, w_ssm16, gates, tm=1024, tn=512)
    x1, h2 = _out_proj(mixed, w_out16, x, mix_post.reshape(1, d), ffn_pre.reshape(1, d), tm=512)
    act = _ffn_up(h2, w_up, ffn_cw, ffn_cb.reshape(1, d_ff), d_ff, tm=1024, tn=512)
    return _ffn_down(act, w_down16, x1, ffn_post.reshape(1, d), tm=1024, tk=FFN_DOWN_TK)


def kernel(x, w_in, mix_pre_norm, mix_post_norm, hg_lb_table, hg_out_norm, ssm_conv_w, ssm_conv_b,
           ssm_dt_bias, ssm_A_log, ssm_D, ssm_out_norm, w_branch_hg, w_branch_ssm, w_out,
           ffn_pre_norm, ffn_post_norm, ffn_w_up, ffn_conv_w, ffn_conv_b, ffn_w_down):
    bsz, t, d = x.shape
    depth = w_in.shape[0]
    assert depth == 1 and hg_lb_table.shape[0] == 2, "forget-gate lower bound is computed for one layer"
    outs = []
    for b in range(bsz):
        xb = x[b]
        for l in range(depth):
            xb = _layer(xb, w_in[l], mix_pre_norm[l], mix_post_norm[l], hg_lb_table, hg_out_norm[l],
                        ssm_conv_w[l], ssm_conv_b[l], ssm_dt_bias[l], ssm_A_log[l], ssm_D[l],
                        ssm_out_norm[l], w_branch_hg[l], w_branch_ssm[l], w_out[l],
                        ffn_pre_norm[l], ffn_post_norm[l], ffn_w_up[l], ffn_conv_w[l], ffn_conv_b[l],
                        ffn_w_down[l])
        outs.append(xb)
    return jnp.stack(outs, axis=0)
```

```python
import functools

import numpy as np
import jax
import jax.numpy as jnp
from jax import lax
from jax.experimental import pallas as pl
from jax.experimental.pallas import tpu as pltpu

F32 = jnp.float32
BF16 = jnp.bfloat16
EPS = 1e-6

HG_DK = 128
HG_DV = 128
SSM_HEADDIM = 64
SSM_GROUPS = 8
SSM_DSTATE = 128
SSM_CONV = 4
FFN_CONV = 3

INPROJ_TILE = 1024
INPROJ_ROWS = 1024
HG_CHUNK = 128
HG_BLOCK = 1024
HG_HEADS_PER_STEP = 8
HG_VPU_LEVEL_MIN = 8
SSD_CHUNK = 256
SSD_GROUPS_PER_STEP = 8
CARRY = 8
FFN_DOWN_TK = 1408
FFN_DOWN_SLAB = 512

VMEM_LIMIT = 48 * 1024 * 1024
VMEM_LIMIT_LARGE = 56 * 1024 * 1024
LOG2E = 1.4426950408889634


def _params(sem, vmem=VMEM_LIMIT):
    return pltpu.CompilerParams(dimension_semantics=sem, vmem_limit_bytes=vmem)


def _dot(a, b):
    return jnp.dot(a, b, preferred_element_type=F32)


def _dot_nt(a, b):
    return lax.dot_general(a, b, (((1,), (1,)), ((), ())), preferred_element_type=F32)


def _dot_tn(a, b):
    return lax.dot_general(a, b, (((0,), (0,)), ((), ())), preferred_element_type=F32)


def _split_bf16(a):
    hi = a.astype(BF16)
    lo = (a - hi.astype(F32)).astype(BF16)
    return hi, lo


def _sigmoid(a):
    return 0.5 * jnp.tanh(0.5 * a) + 0.5


def _silu(a):
    return a * _sigmoid(a)


def _gelu_tanh(a):
    c0 = float(np.sqrt(2.0 / np.pi))
    return a * (0.5 * jnp.tanh(a * (c0 + (c0 * 0.044715) * (a * a))) + 0.5)


def _rms(x, w):
    return x * lax.rsqrt(jnp.mean(x * x, axis=-1, keepdims=True) + EPS) * w


def _norm_dt_kernel(x_ref, nw_ref, wdt_ref, h_ref, dt_ref):
    h = _rms(x_ref[...], nw_ref[...]).astype(BF16)
    h_ref[...] = h
    dt_ref[...] = _dot_nt(wdt_ref[...].astype(BF16), h)


def _norm_dt(x, nw, wdt_t, tm):
    t, d = x.shape
    nh = wdt_t.shape[0]
    return pl.pallas_call(
        _norm_dt_kernel,
        grid=(t // tm,),
        in_specs=[
            pl.BlockSpec((tm, d), lambda m: (m, 0)),
            pl.BlockSpec((1, d), lambda m: (0, 0)),
            pl.BlockSpec((nh, d), lambda m: (0, 0)),
        ],
        out_specs=[pl.BlockSpec((tm, d), lambda m: (m, 0)),
                   pl.BlockSpec((nh, tm), lambda m: (0, m))],
        out_shape=[jax.ShapeDtypeStruct((t, d), BF16), jax.ShapeDtypeStruct((nh, t), F32)],
        compiler_params=_params(("parallel",)),
        name="norm_dt",
    )(x, nw, wdt_t)


def _proj_nt_kernel(h_ref, w_ref, o_ref, wb_ref):
    @pl.when(pl.program_id(1) == 0)
    def _():
        wb_ref[...] = w_ref[...].astype(BF16)

    o_ref[...] = _dot_nt(h_ref[...], wb_ref[...]).astype(o_ref.dtype)


def _proj_nt(h, w_t, row_off, n, tm, tn, name):
    t, d = h.shape
    if row_off % tn == 0:
        w_spec = pl.BlockSpec((tn, d), lambda j, m: (row_off // tn + j, 0))
    else:
        assert row_off % 8 == 0 and tn % 8 == 0
        w_spec = pl.BlockSpec((pl.Element(tn), pl.Element(d)),
                              lambda j, m: (pl.multiple_of(row_off + j * tn, 8), 0))
    return pl.pallas_call(
        _proj_nt_kernel,
        grid=(n // tn, t // tm),
        in_specs=[pl.BlockSpec((tm, d), lambda j, m: (m, 0)), w_spec],
        out_specs=pl.BlockSpec((tm, tn), lambda j, m: (m, j)),
        out_shape=jax.ShapeDtypeStruct((t, n), BF16),
        scratch_shapes=[pltpu.VMEM((tn, d), BF16)],
        compiler_params=_params(("parallel", "arbitrary")),
        name=name,
    )(h, w_t)


def _conv_taps(ext_ref, cur, w, b, ksize):
    q = cur.shape[0]
    ext_ref[CARRY:CARRY + q, :] = cur
    acc = cur * w[ksize - 1:ksize, :] + b
    for j in range(1, ksize):
        acc = acc + ext_ref[CARRY - j:CARRY - j + q, :] * w[ksize - 1 - j:ksize - j, :]
    ext_ref[0:CARRY, :] = ext_ref[q:q + CARRY, :]
    return acc


_STAGE_EXTRAS = {"qscale": 0, "silu": 0, "logf": 1, "conv": 2}


def _seg_proj_kernel(h_ref, w_ref, *rest, kind):
    nx = _STAGE_EXTRAS[kind]
    extra = rest[:nx]
    o_ref, wb_ref = rest[nx:nx + 2]
    first = pl.program_id(1) == 0

    @pl.when(first)
    def _():
        wb_ref[...] = w_ref[...].astype(BF16)

    if kind == "conv":
        ext_ref = rest[nx + 2]

        @pl.when(first)
        def _():
            ext_ref[0:CARRY, :] = jnp.zeros((CARRY, ext_ref.shape[1]), F32)

    acc = _dot_nt(h_ref[...], wb_ref[...])
    if kind == "qscale":
        out = _silu(acc) * HG_DK ** -0.5
    elif kind == "silu":
        out = _silu(acc)
    elif kind == "logf":
        tab = extra[0][...]
        te = jnp.exp(tab - jnp.max(tab, axis=0, keepdims=True))
        lb = te[0:1, :] / jnp.sum(te, axis=0, keepdims=True)
        out = jnp.log(lb + (1.0 - lb) * _sigmoid(acc)) * LOG2E
    else:
        out = _silu(_conv_taps(ext_ref, acc, extra[0][...], extra[1][...], SSM_CONV))
    o_ref[...] = out.astype(o_ref.dtype)


def _seg_proj(h, w_t, row_off, n, kind, extras, tm, tn, name):
    t, d = h.shape
    assert row_off % tn == 0 and len(extras) == _STAGE_EXTRAS[kind]
    scratch = [pltpu.VMEM((tn, d), BF16)]
    if kind == "conv":
        scratch.append(pltpu.VMEM((CARRY + tm, tn), F32))
    return pl.pallas_call(
        functools.partial(_seg_proj_kernel, kind=kind),
        grid=(n // tn, t // tm),
        in_specs=[
            pl.BlockSpec((tm, d), lambda j, m: (m, 0)),
            pl.BlockSpec((tn, d), lambda j, m: (row_off // tn + j, 0)),
        ] + [pl.BlockSpec((e.shape[0], tn), lambda j, m: (0, j)) for e in extras],
        out_specs=pl.BlockSpec((tm, tn), lambda j, m: (m, j)),
        out_shape=jax.ShapeDtypeStruct((t, n), BF16),
        scratch_shapes=scratch,
        compiler_params=_params(("parallel", "arbitrary")),
        name=name,
    )(h, w_t, *extras)


def _hg_constants(c):
    levels = []
    h = c // 2
    while h >= 1:
        levels.append(h)
        h //= 2
    t = np.arange(c)[:, None]
    u = np.arange(c)[None, :]
    mats = [(u <= t)]
    masks = []
    for h in levels:
        blk = 2 * h
        mid = (t // blk) * blk + h
        second = (t % blk) >= h
        if h < HG_VPU_LEVEL_MIN:
            mats.append(np.where(second, (u >= mid) & (u <= t), (u > t) & (u < mid)))
        s = u
        masks.append(((t // blk) == (s // blk)) & second & ((s % blk) < h))
    masks.append(t == u)
    pm = np.concatenate(mats, axis=0).astype(np.float32)
    mk = np.stack(masks, axis=0).astype(np.float32)
    return levels, pm, mk


def _level_exponents(b, h):
    c = b.shape[0]
    parts = []
    for start in range(0, c, 2 * h):
        mid = start + h
        ref = b[mid - 1:mid, :]
        parts.append(ref - b[start:mid, :])
        parts.append(b[mid:mid + h, :] - ref)
    return jnp.concatenate(parts, axis=0)


def _blockdiag(a, b):
    top = jnp.concatenate([a, jnp.zeros((a.shape[0], b.shape[1]), a.dtype)], axis=1)
    bot = jnp.concatenate([jnp.zeros((b.shape[0], a.shape[1]), b.dtype), b], axis=1)
    return jnp.concatenate([top, bot], axis=0)


def _blockdiag_t(kpair):
    kt = kpair.T
    half = kt.shape[0] // 2
    return _blockdiag(kt[0:half], kt[half:])


def _hgrn2_kernel(q_ref, lf_ref, i_ref, g_ref, nw_ref, pm_ref, mk_ref, *rest, chunk, levels, ncast):
    c = chunk
    w_in_refs = rest[:ncast]
    o_ref = rest[ncast]
    w_out_refs = rest[ncast + 1:2 * ncast + 1]
    st_ref, ex_ref = rest[2 * ncast + 1:]
    for src, dst in zip(w_in_refs, w_out_refs):
        dst[...] = src[...].astype(BF16)

    @pl.when(pl.program_id(1) == 0)
    def _():
        st_ref[...] = jnp.zeros_like(st_ref)

    nw = nw_ref[...]
    pm = pm_ref[...]
    nh = st_ref.shape[0]
    dk = HG_DK

    def body(ci, carry):
        r0 = pl.multiple_of(ci * c, c)
        rows = pl.ds(r0, c)
        for p in range(nh // 2):
            ha, hb = 2 * p, 2 * p + 1
            sa = slice(ha * dk, (ha + 1) * dk)
            sb = slice(hb * dk, (hb + 1) * dk)
            ps = slice(ha * dk, (hb + 1) * dk)
            lo, hi = slice(0, dk), slice(dk, 2 * dk)
            qp = q_ref[rows, ps]
            lfp = lf_ref[rows, ps]
            kp = (1.0 - jnp.exp2(lfp.astype(F32))).astype(BF16)
            ex_ref[:, ps] = _dot(pm, lfp)
            b = ex_ref[0:c, ps]
            b_last = b[c - 1:c, :]
            st_a = st_ref[ha]
            st_b = st_ref[hb]
            o = _dot_nt(qp * jnp.exp2(b).astype(BF16),
                        _blockdiag(st_a.astype(BF16), st_b.astype(BF16)))
            sc = mk_ref[len(levels)] * _dot(qp, _blockdiag_t(kp)).astype(BF16)
            n_vpu = sum(h >= HG_VPU_LEVEL_MIN for h in levels)
            for l, h in enumerate(levels):
                if h >= HG_VPU_LEVEL_MIN:
                    ex = _level_exponents(b, h)
                else:
                    ex = ex_ref[(l - n_vpu + 1) * c:(l - n_vpu + 2) * c, ps]
                e = jnp.exp2(ex).astype(BF16)
                ke = kp * e
                s = _dot(qp * e, _blockdiag_t(ke))
                sc = sc + mk_ref[l] * s.astype(BF16)
            o = o + _dot(sc, _blockdiag(i_ref[rows, sa], i_ref[rows, sb]))
            kdec = kp * jnp.exp2(b_last - b).astype(BF16)
            sdec = jnp.exp2(b_last)
            st_ref[ha] = st_a * sdec[:, lo] + _dot_tn(i_ref[rows, sa], kdec[:, lo])
            st_ref[hb] = st_b * sdec[:, hi] + _dot_tn(i_ref[rows, sb], kdec[:, hi])
            o_ref[rows, sa] = (_rms(o[:, lo], nw) * g_ref[rows, sa].astype(F32)).astype(o_ref.dtype)
            o_ref[rows, sb] = (_rms(o[:, hi], nw) * g_ref[rows, sb].astype(F32)).astype(o_ref.dtype)
        return carry

    lax.fori_loop(0, q_ref.shape[0] // c, body, 0)


def _hgrn2(qh, lf, v, gate, norm_w, n_heads, tb, chunk, hps, cast_weights):
    t = qh.shape[0]
    gate_off = 0
    n_steps = (n_heads // hps) * (t // tb)
    for wgt in cast_weights:
        assert wgt.shape[0] % (16 * n_steps) == 0, "weight rows must split into bf16-tile-aligned slices"
    w_rows = [wgt.shape[0] // n_steps for wgt in cast_weights]
    w_specs = [pl.BlockSpec((r, wgt.shape[1]), lambda h, i: (h * (t // tb) + i, 0))
               for r, wgt in zip(w_rows, cast_weights)]
    levels, pm, mk = _hg_constants(chunk)
    pm = jnp.asarray(pm, BF16)
    mk = jnp.asarray(np.concatenate([mk, mk], axis=2), BF16)
    w = hps * HG_DK
    nblk = n_heads // hps
    col = lambda off: (lambda h, i: (i, off // w + h))
    const2 = lambda h, i: (0, 0)
    return pl.pallas_call(
        functools.partial(_hgrn2_kernel, chunk=chunk, levels=tuple(levels), ncast=len(cast_weights)),
        grid=(nblk, t // tb),
        in_specs=[
            pl.BlockSpec((tb, w), col(0)),
            pl.BlockSpec((tb, w), col(0)),
            pl.BlockSpec((tb, w), col(0)),
            pl.BlockSpec((tb, w), col(gate_off)),
            pl.BlockSpec((1, HG_DV), const2),
            pl.BlockSpec(pm.shape, const2),
            pl.BlockSpec(mk.shape, lambda h, i: (0, 0, 0)),
        ] + w_specs,
        out_specs=[pl.BlockSpec((tb, w), lambda h, i: (i, h))] + w_specs,
        out_shape=[jax.ShapeDtypeStruct((t, n_heads * HG_DV), BF16)]
        + [jax.ShapeDtypeStruct(wgt.shape, BF16) for wgt in cast_weights],
        scratch_shapes=[pltpu.VMEM((hps, HG_DV, HG_DK), F32),
                        pltpu.VMEM((pm.shape[0], w), F32)],
        compiler_params=_params(("parallel", "arbitrary")),
        name="hgrn2_scan",
    )(qh, lf, v, gate, norm_w, pm, mk, *cast_weights)


def _ssd_expand_mat(hpg):
    ea = np.zeros((128, hpg * SSM_HEADDIM), np.float32)
    for h in range(hpg):
        ea[h, h * SSM_HEADDIM:(h + 1) * SSM_HEADDIM] = 1.0
    return ea


def _ssd_kernel(z_ref, x_ref, b_ref, c_ref, dt_ref, dtb_ref, alog_ref, dskip_ref, nw_ref,
                triu_ref, ea_ref, o_ref, s_ref):
    @pl.when(pl.program_id(1) == 0)
    def _():
        s_ref[...] = jnp.zeros_like(s_ref)

    gw = s_ref.shape[2]
    n = s_ref.shape[1]
    for gi in range(s_ref.shape[0]):
        _ssd_group(z_ref.at[:, gi * gw:(gi + 1) * gw], x_ref.at[:, gi * gw:(gi + 1) * gw],
                   b_ref.at[:, gi * n:(gi + 1) * n], c_ref.at[:, gi * n:(gi + 1) * n],
                   dt_ref.at[gi], dtb_ref.at[gi], alog_ref.at[gi], dskip_ref.at[gi], nw_ref.at[gi],
                   triu_ref, ea_ref, o_ref.at[:, gi * gw:(gi + 1) * gw], s_ref.at[gi])


def _ssd_group(z_ref, x_ref, b_ref, c_ref, dt_ref, dtb_ref, alog_ref, dskip_ref, nw_ref,
               triu_ref, ea_ref, o_ref, s_ref):
    q = x_ref.shape[0]
    hpg = dt_ref.shape[0]
    xs = x_ref[...].astype(F32)
    bm16 = b_ref[...]
    cm16 = c_ref[...]

    dt_r = jax.nn.softplus(dt_ref[...] + dtb_ref[...])
    da_r = dt_r * (-jnp.exp(alog_ref[...]) * LOG2E)
    da_hi, da_lo = _split_bf16(da_r)
    triu = triu_ref[...]
    acum_r = _dot(da_hi, triu) + _dot(da_lo, triu)
    pad = jnp.zeros((128 - 2 * hpg, q), F32)
    cols = jnp.concatenate([acum_r, dt_r, pad], axis=0).T

    lane = lax.broadcasted_iota(jnp.int32, (q, 128), 1)
    acol = jnp.where(lane < hpg, cols, 0.0)
    a_last = acol[q - 1:q, :]
    dtwd = jnp.exp2(a_last - acol) * pltpu.roll(cols, 128 - hpg, axis=1)
    fx = _dot(jnp.concatenate([jnp.exp2(acol).astype(BF16), dtwd.astype(BF16)], axis=0), ea_ref[...])
    ea_x = fx[0:q]
    dtwd_x = fx[q:2 * q]
    sd_x = _dot(jnp.broadcast_to(jnp.exp2(a_last), (8, 128)).astype(BF16), ea_ref[...])[0:1, :]

    xs16 = x_ref[...]
    bmt = bm16.T
    cb16 = _dot(cm16, bmt).astype(BF16)
    cs = _dot(cm16, s_ref[...].astype(BF16))
    row = lax.broadcasted_iota(jnp.int32, (q, q), 0)
    colid = lax.broadcasted_iota(jnp.int32, (q, q), 1)
    causal = row >= colid
    lo_half = lane < SSM_HEADDIM

    y_parts = []
    for p in range(hpg // 2):
        ms = []
        for h in (2 * p, 2 * p + 1):
            dm = cols[:, h:h + 1] - acum_r[h:h + 1, :]
            lm16 = jnp.exp2(jnp.where(causal, dm, -1e30)).astype(BF16)
            ms.append(cb16 * lm16 * dt_r[h:h + 1, :].astype(BF16))
        yy = _dot(jnp.concatenate(ms, axis=0), xs16[:, p * 128:(p + 1) * 128])
        y_parts.append(jnp.where(lo_half, yy[0:q], yy[q:2 * q]))

    y = jnp.concatenate(y_parts, axis=1) + cs * ea_x
    s_ref[...] = s_ref[...] * sd_x + _dot(bmt, (xs * dtwd_x).astype(BF16))

    y = y + dskip_ref[...] * xs
    y = y * z_ref[...].astype(F32)
    o_ref[...] = _rms(y, nw_ref[...]).astype(o_ref.dtype)


def _ssd(zs, xbc, dt_hm, dt_bias, a_log, d_skip, norm_w, d_inner):
    t = xbc.shape[0]
    z_off = xbc_off = 0
    q = SSD_CHUNK
    g = SSM_GROUPS
    n = SSM_DSTATE
    gw = d_inner // g
    hpg = gw // SSM_HEADDIM
    triu = jnp.asarray(np.triu(np.ones((q, q), np.float32)), BF16)
    ea = jnp.asarray(_ssd_expand_mat(hpg), BF16)
    c2 = lambda gi, i: (0, 0)
    gs = SSD_GROUPS_PER_STEP
    assert g % gs == 0 and d_inner % (gs * n) == 0, "B / C column blocks must be block-aligned"
    return pl.pallas_call(
        _ssd_kernel,
        grid=(g // gs, t // q),
        in_specs=[
            pl.BlockSpec((q, gs * gw), lambda gi, i: (i, z_off // (gs * gw) + gi)),
            pl.BlockSpec((q, gs * gw), lambda gi, i: (i, xbc_off // (gs * gw) + gi)),
            pl.BlockSpec((q, gs * n), lambda gi, i: (i, (xbc_off + d_inner) // (gs * n) + gi)),
            pl.BlockSpec((q, gs * n), lambda gi, i: (i, (xbc_off + d_inner + g * n) // (gs * n) + gi)),
            pl.BlockSpec((gs, hpg, q), lambda gi, i: (gi, 0, i)),
            pl.BlockSpec((gs, hpg, 1), lambda gi, i: (gi, 0, 0)),
            pl.BlockSpec((gs, hpg, 1), lambda gi, i: (gi, 0, 0)),
            pl.BlockSpec((gs, 1, gw), lambda gi, i: (gi, 0, 0)),
            pl.BlockSpec((gs, 1, gw), lambda gi, i: (gi, 0, 0)),
            pl.BlockSpec((q, q), c2),
            pl.BlockSpec(ea.shape, c2),
        ],
        out_specs=pl.BlockSpec((q, gs * gw), lambda gi, i: (i, gi)),
        out_shape=jax.ShapeDtypeStruct((t, d_inner), BF16),
        scratch_shapes=[pltpu.VMEM((gs, n, gw), F32)],
        compiler_params=_params(("parallel", "arbitrary")),
        name="ssd_scan",
    )(zs, xbc, xbc, xbc, dt_hm, dt_bias, a_log, d_skip, norm_w, triu, ea)


def _merge_kernel(yh_ref, ys_ref, wh_ref, ws_ref, gh_ref, gs_ref, o_ref):
    a = _dot(yh_ref[...], wh_ref[...])
    b = _dot(ys_ref[...], ws_ref[...])
    gh = _sigmoid(gh_ref[...].astype(F32))
    gs = _sigmoid(gs_ref[...].astype(F32))
    o_ref[...] = (gh * a + gs * b).astype(o_ref.dtype)


def _merge(y_hg, y_ssm, w_hg, w_ssm, gates, tm, tn):
    t = y_hg.shape[0]
    d = w_hg.shape[1]
    return pl.pallas_call(
        _merge_kernel,
        grid=(t // tm, d // tn),
        in_specs=[
            pl.BlockSpec((tm, y_hg.shape[1]), lambda m, j: (m, 0)),
            pl.BlockSpec((tm, y_ssm.shape[1]), lambda m, j: (m, 0)),
            pl.BlockSpec((w_hg.shape[0], tn), lambda m, j: (0, j)),
            pl.BlockSpec((w_ssm.shape[0], tn), lambda m, j: (0, j)),
            pl.BlockSpec((tm, tn), lambda m, j: (m, j)),
            pl.BlockSpec((tm, tn), lambda m, j: (m, d // tn + j)),
        ],
        out_specs=pl.BlockSpec((tm, tn), lambda m, j: (m, j)),
        out_shape=jax.ShapeDtypeStruct((t, d), BF16),
        compiler_params=_params(("parallel", "arbitrary")),
        name="branch_merge",
    )(y_hg, y_ssm, w_hg, w_ssm, gates, gates)


def _out_kernel(a_ref, w_ref, x_ref, post_ref, pre_ref, x1_ref, h_ref):
    r = _dot(a_ref[...], w_ref[...])
    x1 = x_ref[...] + _rms(r, post_ref[...])
    x1_ref[...] = x1
    h_ref[...] = _rms(x1, pre_ref[...]).astype(BF16)


def _out_proj(mixed, w_out, x, post_w, pre_w, tm):
    t, d = x.shape
    c2 = lambda m: (0, 0)
    row = lambda m: (m, 0)
    return pl.pallas_call(
        _out_kernel,
        grid=(t // tm,),
        in_specs=[
            pl.BlockSpec((tm, d), row),
            pl.BlockSpec((d, d), c2),
            pl.BlockSpec((tm, d), row),
            pl.BlockSpec((1, d), c2),
            pl.BlockSpec((1, d), c2),
        ],
        out_specs=[pl.BlockSpec((tm, d), row), pl.BlockSpec((tm, d), row)],
        out_shape=[jax.ShapeDtypeStruct((t, d), F32), jax.ShapeDtypeStruct((t, d), BF16)],
        compiler_params=_params(("parallel",)),
        name="out_proj",
    )(mixed, w_out, x, post_w, pre_w)


def _ffn_up_kernel(h_ref, wg_ref, wu_ref, cw_ref, cb_ref, o_ref, ext_ref, wgb_ref, wub_ref):
    @pl.when(pl.program_id(1) == 0)
    def _():
        wgb_ref[...] = wg_ref[...].astype(BF16)
        wub_ref[...] = wu_ref[...].astype(BF16)
        ext_ref[0:CARRY, :] = jnp.zeros((CARRY, ext_ref.shape[1]), F32)

    h = h_ref[...]
    gate = _dot(h, wgb_ref[...])
    up = _dot(h, wub_ref[...])
    conv = _conv_taps(ext_ref, gate, cw_ref[...], cb_ref[...], FFN_CONV)
    o_ref[...] = (_gelu_tanh(conv) * up).astype(o_ref.dtype)


def _ffn_up(h, w_up, conv_w, conv_b, d_ff, tm, tn):
    t, d = h.shape
    k = conv_w.shape[0]
    return pl.pallas_call(
        _ffn_up_kernel,
        grid=(d_ff // tn, t // tm),
        in_specs=[
            pl.BlockSpec((tm, d), lambda j, m: (m, 0)),
            pl.BlockSpec((d, tn), lambda j, m: (0, j)),
            pl.BlockSpec((d, tn), lambda j, m: (0, d_ff // tn + j)),
            pl.BlockSpec((k, tn), lambda j, m: (0, j)),
            pl.BlockSpec((1, tn), lambda j, m: (0, j)),
        ],
        out_specs=pl.BlockSpec((tm, tn), lambda j, m: (m, j)),
        out_shape=jax.ShapeDtypeStruct((t, d_ff), BF16),
        scratch_shapes=[pltpu.VMEM((CARRY + tm, tn), F32),
                        pltpu.VMEM((d, tn), BF16), pltpu.VMEM((d, tn), BF16)],
        compiler_params=_params(("parallel", "arbitrary")),
        name="ffn_up",
    )(h, w_up, w_up, conv_w, conv_b)


def _ffn_down_kernel(a_ref, w_ref, x_ref, post_ref, o_ref):
    kstep = pl.program_id(1)

    @pl.when(kstep == 0)
    def _():
        o_ref[...] = jnp.zeros_like(o_ref)

    d = o_ref.shape[1]
    slabs = [slice(c, c + FFN_DOWN_SLAB) for c in range(0, d, FFN_DOWN_SLAB)]
    a = a_ref[...]
    for cs in slabs:
        o_ref[:, cs] += _dot(a, w_ref[:, cs])

    @pl.when(kstep == pl.num_programs(1) - 1)
    def _():
        ssq = sum(jnp.sum(o_ref[:, cs] * o_ref[:, cs], axis=-1, keepdims=True) for cs in slabs)
        inv = lax.rsqrt(ssq / d + EPS)
        for cs in slabs:
            o_ref[:, cs] = x_ref[:, cs] + o_ref[:, cs] * inv * post_ref[:, cs]


def _ffn_down(act, w_down, x1, post_w, tm, tk):
    t, d = x1.shape
    d_ff = act.shape[1]
    return pl.pallas_call(
        _ffn_down_kernel,
        grid=(t // tm, d_ff // tk),
        in_specs=[
            pl.BlockSpec((tm, tk), lambda m, k: (m, k)),
            pl.BlockSpec((tk, d), lambda m, k: (k, 0)),
            pl.BlockSpec((tm, d), lambda m, k: (m, 0)),
            pl.BlockSpec((1, d), lambda m, k: (0, 0)),
        ],
        out_specs=pl.BlockSpec((tm, d), lambda m, k: (m, 0)),
        out_shape=jax.ShapeDtypeStruct((t, d), F32),
        compiler_params=_params(("parallel", "arbitrary"), VMEM_LIMIT_LARGE),
        name="ffn_down",
    )(act, w_down, x1, post_w)


def _layer(x, w_in, mix_pre, mix_post, lb_table, hg_norm, conv_w, conv_b, dt_bias, a_log, d_skip,
           ssm_norm, w_hg, w_ssm, w_out, ffn_pre, ffn_post, w_up, ffn_cw, ffn_cb, w_down):
    t, d = x.shape
    hg_v = w_hg.shape[0]
    hg_heads = hg_v // HG_DV
    d_inner = w_ssm.shape[0]
    ssm_heads = a_log.shape[0]
    hpg = ssm_heads // SSM_GROUPS
    gw = d_inner // SSM_GROUPS
    d_ff = w_down.shape[0]
    conv_dim = conv_w.shape[1]

    main_n = 4 * hg_v + d_inner + conv_dim
    dt_off = main_n
    gate_off = dt_off + ssm_heads
    z_off = 4 * hg_v
    xbc_off = z_off + d_inner

    w_t = w_in.T
    h, dt_t = _norm_dt(x, mix_pre.reshape(1, d), w_t[dt_off:gate_off], tm=512)
    tm, tn = INPROJ_ROWS, INPROJ_TILE
    qh = _seg_proj(h, w_t, 0, hg_v, "qscale", (), tm, tn, "inproj_q")
    lf = _seg_proj(h, w_t, hg_v, hg_v, "logf", (lb_table,), tm, tn, "inproj_f")
    v = _proj_nt(h, w_t, 2 * hg_v, hg_v, tm, tn, "inproj_i")
    go = _seg_proj(h, w_t, 3 * hg_v, hg_v, "silu", (), tm, tn, "inproj_g")
    zs = _seg_proj(h, w_t, z_off, d_inner, "silu", (), tm, tn, "inproj_z")
    xbc = _seg_proj(h, w_t, xbc_off, conv_dim, "conv", (conv_w, conv_b.reshape(1, conv_dim)), tm, tn,
                    "inproj_xbc")
    gates = _proj_nt(h, w_t, gate_off, 2 * d, tm, tn, "inproj_gates")

    y_hg, w_hg16, w_ssm16, w_out16, w_down16 = _hgrn2(
        qh, lf, v, go, hg_norm.reshape(1, HG_DV), hg_heads, HG_BLOCK, HG_CHUNK, HG_HEADS_PER_STEP,
        cast_weights=(w_hg, w_ssm, w_out, w_down))

    dt_hm = dt_t.reshape(SSM_GROUPS, hpg, t)
    y_ssm = _ssd(
        zs, xbc, dt_hm,
        dt_bias.reshape(SSM_GROUPS, hpg, 1), a_log.reshape(SSM_GROUPS, hpg, 1),
        jnp.repeat(d_skip, SSM_HEADDIM).reshape(SSM_GROUPS, 1, gw),
        ssm_norm.reshape(SSM_GROUPS, 1, gw), d_inner)

    mixed = _merge(y_hg, y_ssm, w_hg16, w_ssm16, gates, tm=1024, tn=512)
    x1, h2 = _out_proj(mixed, w_out16, x, mix_post.reshape(1, d), ffn_pre.reshape(1, d), tm=512)
    act = _ffn_up(h2, w_up, ffn_cw, ffn_cb.reshape(1, d_ff), d_ff, tm=1024, tn=512)
    return _ffn_down(act, w_down16, x1, ffn_post.reshape(1, d), tm=1024, tk=FFN_DOWN_TK)


def kernel(x, w_in, mix_pre_norm, mix_post_norm, hg_lb_table, hg_out_norm, ssm_conv_w, ssm_conv_b,
           ssm_dt_bias, ssm_A_log, ssm_D, ssm_out_norm, w_branch_hg, w_branch_ssm, w_out,
           ffn_pre_norm, ffn_post_norm, ffn_w_up, ffn_conv_w, ffn_conv_b, ffn_w_down):
    bsz, t, d = x.shape
    depth = w_in.shape[0]
    assert depth == 1 and hg_lb_table.shape[0] == 2, "forget-gate lower bound is computed for one layer"
    outs = []
    for b in range(bsz):
        xb = x[b]
        for l in range(depth):
            xb = _layer(xb, w_in[l], mix_pre_norm[l], mix_post_norm[l], hg_lb_table, hg_out_norm[l],
                        ssm_conv_w[l], ssm_conv_b[l], ssm_dt_bias[l], ssm_A_log[l], ssm_D[l],
                        ssm_out_norm[l], w_branch_hg[l], w_branch_ssm[l], w_out[l],
                        ffn_pre_norm[l], ffn_post_norm[l], ffn_w_up[l], ffn_conv_w[l], ffn_conv_b[l],
                        ffn_w_down[l])
        outs.append(xb)
    return jnp.stack(outs, axis=0)
```

```python
import functools

import numpy as np
import jax
import jax.numpy as jnp
from jax import lax
from jax.experimental import pallas as pl
from jax.experimental.pallas import tpu as pltpu

F32 = jnp.float32
BF16 = jnp.bfloat16
EPS = 1e-6

HG_DK = 128
HG_DV = 128
SSM_HEADDIM = 64
SSM_GROUPS = 8
SSM_DSTATE = 128
SSM_CONV = 4
FFN_CONV = 3

INPROJ_TILE = 1024
INPROJ_ROWS = 1024
HG_CHUNK = 128
HG_BLOCK = 1024
HG_HEADS_PER_STEP = 8
HG_VPU_LEVEL_MIN = 8
SSD_CHUNK = 256
SSD_GROUPS_PER_STEP = 8
CARRY = 8
CONV_ROWS = 256
FFN_DOWN_TK = 1408
FFN_DOWN_SLAB = 512

VMEM_LIMIT = 48 * 1024 * 1024
VMEM_LIMIT_LARGE = 56 * 1024 * 1024
LOG2E = 1.4426950408889634


def _params(sem, vmem=VMEM_LIMIT):
    return pltpu.CompilerParams(dimension_semantics=sem, vmem_limit_bytes=vmem)


def _dot(a, b):
    return jnp.dot(a, b, preferred_element_type=F32)


def _dot_nt(a, b):
    return lax.dot_general(a, b, (((1,), (1,)), ((), ())), preferred_element_type=F32)


def _dot_tn(a, b):
    return lax.dot_general(a, b, (((0,), (0,)), ((), ())), preferred_element_type=F32)


def _split_bf16(a):
    hi = a.astype(BF16)
    lo = (a - hi.astype(F32)).astype(BF16)
    return hi, lo


def _sigmoid(a):
    return 0.5 * jnp.tanh(0.5 * a) + 0.5


def _silu(a):
    return a * _sigmoid(a)


def _gelu_tanh(a):
    c0 = float(np.sqrt(2.0 / np.pi))
    return a * (0.5 * jnp.tanh(a * (c0 + (c0 * 0.044715) * (a * a))) + 0.5)


def _rms(x, w):
    return x * lax.rsqrt(jnp.mean(x * x, axis=-1, keepdims=True) + EPS) * w


def _norm_dt_kernel(x_ref, nw_ref, wdt_ref, h_ref, dt_ref):
    h = _rms(x_ref[...], nw_ref[...]).astype(BF16)
    h_ref[...] = h
    dt_ref[...] = _dot_nt(wdt_ref[...].astype(BF16), h)


def _norm_dt(x, nw, wdt_t, tm):
    t, d = x.shape
    nh = wdt_t.shape[0]
    return pl.pallas_call(
        _norm_dt_kernel,
        grid=(t // tm,),
        in_specs=[
            pl.BlockSpec((tm, d), lambda m: (m, 0)),
            pl.BlockSpec((1, d), lambda m: (0, 0)),
            pl.BlockSpec((nh, d), lambda m: (0, 0)),
        ],
        out_specs=[pl.BlockSpec((tm, d), lambda m: (m, 0)),
                   pl.BlockSpec((nh, tm), lambda m: (0, m))],
        out_shape=[jax.ShapeDtypeStruct((t, d), BF16), jax.ShapeDtypeStruct((nh, t), F32)],
        compiler_params=_params(("parallel",)),
        name="norm_dt",
    )(x, nw, wdt_t)


def _proj_nt_kernel(h_ref, w_ref, o_ref, wb_ref):
    @pl.when(pl.program_id(1) == 0)
    def _():
        wb_ref[...] = w_ref[...].astype(BF16)

    o_ref[...] = _dot_nt(h_ref[...], wb_ref[...]).astype(o_ref.dtype)


def _proj_nt(h, w_t, row_off, n, tm, tn, name):
    t, d = h.shape
    if row_off % tn == 0:
        w_spec = pl.BlockSpec((tn, d), lambda j, m: (row_off // tn + j, 0))
    else:
        assert row_off % 8 == 0 and tn % 8 == 0
        w_spec = pl.BlockSpec((pl.Element(tn), pl.Element(d)),
                              lambda j, m: (pl.multiple_of(row_off + j * tn, 8), 0))
    return pl.pallas_call(
        _proj_nt_kernel,
        grid=(n // tn, t // tm),
        in_specs=[pl.BlockSpec((tm, d), lambda j, m: (m, 0)), w_spec],
        out_specs=pl.BlockSpec((tm, tn), lambda j, m: (m, j)),
        out_shape=jax.ShapeDtypeStruct((t, n), BF16),
        scratch_shapes=[pltpu.VMEM((tn, d), BF16)],
        compiler_params=_params(("parallel", "arbitrary")),
        name=name,
    )(h, w_t)


def _conv_rows(ext_ref, cur, r0, w, b, ksize):
    rows = cur.shape[0]
    ext_ref[CARRY + r0:CARRY + r0 + rows, :] = cur
    acc = cur * w[ksize - 1:ksize, :] + b
    for j in range(1, ksize):
        acc = acc + ext_ref[CARRY + r0 - j:CARRY + r0 - j + rows, :] * w[ksize - 1 - j:ksize - j, :]
    return acc


def _conv_keep_tail(ext_ref, q):
    ext_ref[0:CARRY, :] = ext_ref[q:q + CARRY, :]


_STAGE_EXTRAS = {"qscale": 0, "silu": 0, "logf": 1, "conv": 2}


def _seg_proj_kernel(h_ref, w_ref, *rest, kind):
    nx = _STAGE_EXTRAS[kind]
    extra = rest[:nx]
    o_ref, wb_ref = rest[nx:nx + 2]
    first = pl.program_id(1) == 0

    @pl.when(first)
    def _():
        wb_ref[...] = w_ref[...].astype(BF16)

    if kind == "conv":
        ext_ref = rest[nx + 2]

        @pl.when(first)
        def _():
            ext_ref[0:CARRY, :] = jnp.zeros((CARRY, ext_ref.shape[1]), F32)

    if kind == "logf":
        tab = extra[0][...]
        te = jnp.exp(tab - jnp.max(tab, axis=0, keepdims=True))
        lb = te[0:1, :] / jnp.sum(te, axis=0, keepdims=True)
    elif kind == "conv":
        w = extra[0][...]
        bias = extra[1][...]
    wb = wb_ref[...]
    for r0 in range(0, o_ref.shape[0], CONV_ROWS):
        acc = _dot_nt(h_ref[r0:r0 + CONV_ROWS, :], wb)
        if kind == "qscale":
            out = _silu(acc) * HG_DK ** -0.5
        elif kind == "silu":
            out = _silu(acc)
        elif kind == "logf":
            out = jnp.log(lb + (1.0 - lb) * _sigmoid(acc)) * LOG2E
        else:
            out = _silu(_conv_rows(ext_ref, acc, r0, w, bias, SSM_CONV))
        o_ref[r0:r0 + CONV_ROWS, :] = out.astype(o_ref.dtype)
    if kind == "conv":
        _conv_keep_tail(ext_ref, o_ref.shape[0])


def _seg_proj(h, w_t, row_off, n, kind, extras, tm, tn, name):
    t, d = h.shape
    assert row_off % tn == 0 and len(extras) == _STAGE_EXTRAS[kind]
    scratch = [pltpu.VMEM((tn, d), BF16)]
    if kind == "conv":
        scratch.append(pltpu.VMEM((CARRY + tm, tn), F32))
    return pl.pallas_call(
        functools.partial(_seg_proj_kernel, kind=kind),
        grid=(n // tn, t // tm),
        in_specs=[
            pl.BlockSpec((tm, d), lambda j, m: (m, 0)),
            pl.BlockSpec((tn, d), lambda j, m: (row_off // tn + j, 0)),
        ] + [pl.BlockSpec((e.shape[0], tn), lambda j, m: (0, j)) for e in extras],
        out_specs=pl.BlockSpec((tm, tn), lambda j, m: (m, j)),
        out_shape=jax.ShapeDtypeStruct((t, n), BF16),
        scratch_shapes=scratch,
        compiler_params=_params(("parallel", "arbitrary")),
        name=name,
    )(h, w_t, *extras)


def _hg_constants(c):
    levels = []
    h = c // 2
    while h >= 1:
        levels.append(h)
        h //= 2
    t = np.arange(c)[:, None]
    u = np.arange(c)[None, :]
    mats = [(u <= t)]
    masks = []
    for h in levels:
        blk = 2 * h
        mid = (t // blk) * blk + h
        second = (t % blk) >= h
        if h < HG_VPU_LEVEL_MIN:
            mats.append(np.where(second, (u >= mid) & (u <= t), (u > t) & (u < mid)))
        s = u
        masks.append(((t // blk) == (s // blk)) & second & ((s % blk) < h))
    masks.append(t == u)
    pm = np.concatenate(mats, axis=0).astype(np.float32)
    mk = np.stack(masks, axis=0).astype(np.float32)
    return levels, pm, mk


def _level_exponents(b, h):
    c = b.shape[0]
    parts = []
    for start in range(0, c, 2 * h):
        mid = start + h
        ref = b[mid - 1:mid, :]
        parts.append(ref - b[start:mid, :])
        parts.append(b[mid:mid + h, :] - ref)
    return jnp.concatenate(parts, axis=0)


def _blockdiag(a, b):
    top = jnp.concatenate([a, jnp.zeros((a.shape[0], b.shape[1]), a.dtype)], axis=1)
    bot = jnp.concatenate([jnp.zeros((b.shape[0], a.shape[1]), b.dtype), b], axis=1)
    return jnp.concatenate([top, bot], axis=0)


def _blockdiag_t(kpair):
    kt = kpair.T
    half = kt.shape[0] // 2
    return _blockdiag(kt[0:half], kt[half:])


def _hgrn2_kernel(q_ref, lf_ref, i_ref, g_ref, nw_ref, pm_ref, mk_ref, *rest, chunk, levels, ncast):
    c = chunk
    w_in_refs = rest[:ncast]
    o_ref = rest[ncast]
    w_out_refs = rest[ncast + 1:2 * ncast + 1]
    st_ref, ex_ref = rest[2 * ncast + 1:]
    for src, dst in zip(w_in_refs, w_out_refs):
        dst[...] = src[...].astype(BF16)

    @pl.when(pl.program_id(1) == 0)
    def _():
        st_ref[...] = jnp.zeros_like(st_ref)

    nw = nw_ref[...]
    pm = pm_ref[...]
    nh = st_ref.shape[0]
    dk = HG_DK

    def body(ci, carry):
        r0 = pl.multiple_of(ci * c, c)
        rows = pl.ds(r0, c)
        for p in range(nh // 2):
            ha, hb = 2 * p, 2 * p + 1
            sa = slice(ha * dk, (ha + 1) * dk)
            sb = slice(hb * dk, (hb + 1) * dk)
            ps = slice(ha * dk, (hb + 1) * dk)
            lo, hi = slice(0, dk), slice(dk, 2 * dk)
            qp = q_ref[rows, ps]
            lfp = lf_ref[rows, ps]
            kp = (1.0 - jnp.exp2(lfp.astype(F32))).astype(BF16)
            ex_ref[:, ps] = _dot(pm, lfp)
            b = ex_ref[0:c, ps]
            b_last = b[c - 1:c, :]
            st_a = st_ref[ha]
            st_b = st_ref[hb]
            o = _dot_nt(qp * jnp.exp2(b).astype(BF16),
                        _blockdiag(st_a.astype(BF16), st_b.astype(BF16)))
            sc = mk_ref[len(levels)] * _dot(qp, _blockdiag_t(kp)).astype(BF16)
            n_vpu = sum(h >= HG_VPU_LEVEL_MIN for h in levels)
            for l, h in enumerate(levels):
                if h >= HG_VPU_LEVEL_MIN:
                    ex = _level_exponents(b, h)
                else:
                    ex = ex_ref[(l - n_vpu + 1) * c:(l - n_vpu + 2) * c, ps]
                e = jnp.exp2(ex).astype(BF16)
                ke = kp * e
                s = _dot(qp * e, _blockdiag_t(ke))
                sc = sc + mk_ref[l] * s.astype(BF16)
            o = o + _dot(sc, _blockdiag(i_ref[rows, sa], i_ref[rows, sb]))
            kdec = kp * jnp.exp2(b_last - b).astype(BF16)
            sdec = jnp.exp2(b_last)
            st_ref[ha] = st_a * sdec[:, lo] + _dot_tn(i_ref[rows, sa], kdec[:, lo])
            st_ref[hb] = st_b * sdec[:, hi] + _dot_tn(i_ref[rows, sb], kdec[:, hi])
            o_ref[rows, sa] = (_rms(o[:, lo], nw) * g_ref[rows, sa].astype(F32)).astype(o_ref.dtype)
            o_ref[rows, sb] = (_rms(o[:, hi], nw) * g_ref[rows, sb].astype(F32)).astype(o_ref.dtype)
        return carry

    lax.fori_loop(0, q_ref.shape[0] // c, body, 0)


def _hgrn2(qh, lf, v, gate, norm_w, n_heads, tb, chunk, hps, cast_weights):
    t = qh.shape[0]
    gate_off = 0
    n_steps = (n_heads // hps) * (t // tb)
    for wgt in cast_weights:
        assert wgt.shape[0] % (16 * n_steps) == 0, "weight rows must split into bf16-tile-aligned slices"
    w_rows = [wgt.shape[0] // n_steps for wgt in cast_weights]
    w_specs = [pl.BlockSpec((r, wgt.shape[1]), lambda h, i: (h * (t // tb) + i, 0))
               for r, wgt in zip(w_rows, cast_weights)]
    levels, pm, mk = _hg_constants(chunk)
    pm = jnp.asarray(pm, BF16)
    mk = jnp.asarray(np.concatenate([mk, mk], axis=2), BF16)
    w = hps * HG_DK
    nblk = n_heads // hps
    col = lambda off: (lambda h, i: (i, off // w + h))
    const2 = lambda h, i: (0, 0)
    return pl.pallas_call(
        functools.partial(_hgrn2_kernel, chunk=chunk, levels=tuple(levels), ncast=len(cast_weights)),
        grid=(nblk, t // tb),
        in_specs=[
            pl.BlockSpec((tb, w), col(0)),
            pl.BlockSpec((tb, w), col(0)),
            pl.BlockSpec((tb, w), col(0)),
            pl.BlockSpec((tb, w), col(gate_off)),
            pl.BlockSpec((1, HG_DV), const2),
            pl.BlockSpec(pm.shape, const2),
            pl.BlockSpec(mk.shape, lambda h, i: (0, 0, 0)),
        ] + w_specs,
        out_specs=[pl.BlockSpec((tb, w), lambda h, i: (i, h))] + w_specs,
        out_shape=[jax.ShapeDtypeStruct((t, n_heads * HG_DV), BF16)]
        + [jax.ShapeDtypeStruct(wgt.shape, BF16) for wgt in cast_weights],
        scratch_shapes=[pltpu.VMEM((hps, HG_DV, HG_DK), F32),
                        pltpu.VMEM((pm.shape[0], w), F32)],
        compiler_params=_params(("parallel", "arbitrary")),
        name="hgrn2_scan",
    )(qh, lf, v, gate, norm_w, pm, mk, *cast_weights)


def _ssd_expand_mat(hpg):
    ea = np.zeros((128, hpg * SSM_HEADDIM), np.float32)
    for h in range(hpg):
        ea[h, h * SSM_HEADDIM:(h + 1) * SSM_HEADDIM] = 1.0
    return ea


def _ssd_kernel(z_ref, x_ref, b_ref, c_ref, dt_ref, dtb_ref, alog_ref, dskip_ref, nw_ref,
                triu_ref, ea_ref, o_ref, s_ref):
    @pl.when(pl.program_id(1) == 0)
    def _():
        s_ref[...] = jnp.zeros_like(s_ref)

    gw = s_ref.shape[2]
    n = s_ref.shape[1]
    for gi in range(s_ref.shape[0]):
        _ssd_group(z_ref.at[:, gi * gw:(gi + 1) * gw], x_ref.at[:, gi * gw:(gi + 1) * gw],
                   b_ref.at[:, gi * n:(gi + 1) * n], c_ref.at[:, gi * n:(gi + 1) * n],
                   dt_ref.at[gi], dtb_ref.at[gi], alog_ref.at[gi], dskip_ref.at[gi], nw_ref.at[gi],
                   triu_ref, ea_ref, o_ref.at[:, gi * gw:(gi + 1) * gw], s_ref.at[gi])


def _ssd_group(z_ref, x_ref, b_ref, c_ref, dt_ref, dtb_ref, alog_ref, dskip_ref, nw_ref,
               triu_ref, ea_ref, o_ref, s_ref):
    q = x_ref.shape[0]
    hpg = dt_ref.shape[0]
    xs = x_ref[...].astype(F32)
    bm16 = b_ref[...]
    cm16 = c_ref[...]

    dt_r = jax.nn.softplus(dt_ref[...] + dtb_ref[...])
    da_r = dt_r * (-jnp.exp(alog_ref[...]) * LOG2E)
    da_hi, da_lo = _split_bf16(da_r)
    triu = triu_ref[...]
    acum_r = _dot(da_hi, triu) + _dot(da_lo, triu)
    pad = jnp.zeros((128 - 2 * hpg, q), F32)
    cols = jnp.concatenate([acum_r, dt_r, pad], axis=0).T

    lane = lax.broadcasted_iota(jnp.int32, (q, 128), 1)
    acol = jnp.where(lane < hpg, cols, 0.0)
    a_last = acol[q - 1:q, :]
    dtwd = jnp.exp2(a_last - acol) * pltpu.roll(cols, 128 - hpg, axis=1)
    fx = _dot(jnp.concatenate([jnp.exp2(acol).astype(BF16), dtwd.astype(BF16)], axis=0), ea_ref[...])
    ea_x = fx[0:q]
    dtwd_x = fx[q:2 * q]
    sd_x = _dot(jnp.broadcast_to(jnp.exp2(a_last), (8, 128)).astype(BF16), ea_ref[...])[0:1, :]

    xs16 = x_ref[...]
    bmt = bm16.T
    cb16 = _dot(cm16, bmt).astype(BF16)
    cs = _dot(cm16, s_ref[...].astype(BF16))
    row = lax.broadcasted_iota(jnp.int32, (q, q), 0)
    colid = lax.broadcasted_iota(jnp.int32, (q, q), 1)
    causal = row >= colid
    lo_half = lane < SSM_HEADDIM

    y_parts = []
    for p in range(hpg // 2):
        ms = []
        for h in (2 * p, 2 * p + 1):
            dm = cols[:, h:h + 1] - acum_r[h:h + 1, :]
            lm16 = jnp.exp2(jnp.where(causal, dm, -1e30)).astype(BF16)
            ms.append(cb16 * lm16 * dt_r[h:h + 1, :].astype(BF16))
        yy = _dot(jnp.concatenate(ms, axis=0), xs16[:, p * 128:(p + 1) * 128])
        y_parts.append(jnp.where(lo_half, yy[0:q], yy[q:2 * q]))

    y = jnp.concatenate(y_parts, axis=1) + cs * ea_x
    s_ref[...] = s_ref[...] * sd_x + _dot(bmt, (xs * dtwd_x).astype(BF16))

    y = y + dskip_ref[...] * xs
    y = y * z_ref[...].astype(F32)
    o_ref[...] = _rms(y, nw_ref[...]).astype(o_ref.dtype)


def _ssd(zs, xbc, dt_hm, dt_bias, a_log, d_skip, norm_w, d_inner):
    t = xbc.shape[0]
    z_off = xbc_off = 0
    q = SSD_CHUNK
    g = SSM_GROUPS
    n = SSM_DSTATE
    gw = d_inner // g
    hpg = gw // SSM_HEADDIM
    triu = jnp.asarray(np.triu(np.ones((q, q), np.float32)), BF16)
    ea = jnp.asarray(_ssd_expand_mat(hpg), BF16)
    c2 = lambda gi, i: (0, 0)
    gs = SSD_GROUPS_PER_STEP
    assert g % gs == 0 and d_inner % (gs * n) == 0, "B / C column blocks must be block-aligned"
    return pl.pallas_call(
        _ssd_kernel,
        grid=(g // gs, t // q),
        in_specs=[
            pl.BlockSpec((q, gs * gw), lambda gi, i: (i, z_off // (gs * gw) + gi)),
            pl.BlockSpec((q, gs * gw), lambda gi, i: (i, xbc_off // (gs * gw) + gi)),
            pl.BlockSpec((q, gs * n), lambda gi, i: (i, (xbc_off + d_inner) // (gs * n) + gi)),
            pl.BlockSpec((q, gs * n), lambda gi, i: (i, (xbc_off + d_inner + g * n) // (gs * n) + gi)),
            pl.BlockSpec((gs, hpg, q), lambda gi, i: (gi, 0, i)),
            pl.BlockSpec((gs, hpg, 1), lambda gi, i: (gi, 0, 0)),
            pl.BlockSpec((gs, hpg, 1), lambda gi, i: (gi, 0, 0)),
            pl.BlockSpec((gs, 1, gw), lambda gi, i: (gi, 0, 0)),
            pl.BlockSpec((gs, 1, gw), lambda gi, i: (gi, 0, 0)),
            pl.BlockSpec((q, q), c2),
            pl.BlockSpec(ea.shape, c2),
        ],
        out_specs=pl.BlockSpec((q, gs * gw), lambda gi, i: (i, gi)),
        out_shape=jax.ShapeDtypeStruct((t, d_inner), BF16),
        scratch_shapes=[pltpu.VMEM((gs, n, gw), F32)],
        compiler_params=_params(("parallel", "arbitrary")),
        name="ssd_scan",
    )(zs, xbc, xbc, xbc, dt_hm, dt_bias, a_log, d_skip, norm_w, triu, ea)


def _merge_kernel(yh_ref, ys_ref, wh_ref, ws_ref, gh_ref, gs_ref, o_ref):
    a = _dot(yh_ref[...], wh_ref[...])
    b = _dot(ys_ref[...], ws_ref[...])
    gh = _sigmoid(gh_ref[...].astype(F32))
    gs = _sigmoid(gs_ref[...].astype(F32))
    o_ref[...] = (gh * a + gs * b).astype(o_ref.dtype)


def _merge(y_hg, y_ssm, w_hg, w_ssm, gates, tm, tn):
    t = y_hg.shape[0]
    d = w_hg.shape[1]
    return pl.pallas_call(
        _merge_kernel,
        grid=(t // tm, d // tn),
        in_specs=[
            pl.BlockSpec((tm, y_hg.shape[1]), lambda m, j: (m, 0)),
            pl.BlockSpec((tm, y_ssm.shape[1]), lambda m, j: (m, 0)),
            pl.BlockSpec((w_hg.shape[0], tn), lambda m, j: (0, j)),
            pl.BlockSpec((w_ssm.shape[0], tn), lambda m, j: (0, j)),
            pl.BlockSpec((tm, tn), lambda m, j: (m, j)),
            pl.BlockSpec((tm, tn), lambda m, j: (m, d // tn + j)),
        ],
        out_specs=pl.BlockSpec((tm, tn), lambda m, j: (m, j)),
        out_shape=jax.ShapeDtypeStruct((t, d), BF16),
        compiler_params=_params(("parallel", "arbitrary")),
        name="branch_merge",
    )(y_hg, y_ssm, w_hg, w_ssm, gates, gates)


def _out_kernel(a_ref, w_ref, x_ref, post_ref, pre_ref, x1_ref, h_ref):
    w = w_ref[...]
    for r0 in range(0, x_ref.shape[0], CONV_ROWS):
        rows = slice(r0, r0 + CONV_ROWS)
        x1 = x_ref[rows, :] + _rms(_dot(a_ref[rows, :], w), post_ref[...])
        x1_ref[rows, :] = x1
        h_ref[rows, :] = _rms(x1, pre_ref[...]).astype(BF16)


def _out_proj(mixed, w_out, x, post_w, pre_w, tm):
    t, d = x.shape
    c2 = lambda m: (0, 0)
    row = lambda m: (m, 0)
    return pl.pallas_call(
        _out_kernel,
        grid=(t // tm,),
        in_specs=[
            pl.BlockSpec((tm, d), row),
            pl.BlockSpec((d, d), c2),
            pl.BlockSpec((tm, d), row),
            pl.BlockSpec((1, d), c2),
            pl.BlockSpec((1, d), c2),
        ],
        out_specs=[pl.BlockSpec((tm, d), row), pl.BlockSpec((tm, d), row)],
        out_shape=[jax.ShapeDtypeStruct((t, d), F32), jax.ShapeDtypeStruct((t, d), BF16)],
        compiler_params=_params(("parallel",)),
        name="out_proj",
    )(mixed, w_out, x, post_w, pre_w)


def _ffn_up_kernel(h_ref, wg_ref, wu_ref, cw_ref, cb_ref, o_ref, ext_ref, wgb_ref, wub_ref):
    @pl.when(pl.program_id(1) == 0)
    def _():
        wgb_ref[...] = wg_ref[...].astype(BF16)
        wub_ref[...] = wu_ref[...].astype(BF16)
        ext_ref[0:CARRY, :] = jnp.zeros((CARRY, ext_ref.shape[1]), F32)

    w = cw_ref[...]
    bias = cb_ref[...]
    wg = wgb_ref[...]
    wu = wub_ref[...]
    for r0 in range(0, o_ref.shape[0], CONV_ROWS):
        h = h_ref[r0:r0 + CONV_ROWS, :]
        conv = _conv_rows(ext_ref, _dot(h, wg), r0, w, bias, FFN_CONV)
        o_ref[r0:r0 + CONV_ROWS, :] = (_gelu_tanh(conv) * _dot(h, wu)).astype(o_ref.dtype)
    _conv_keep_tail(ext_ref, o_ref.shape[0])


def _ffn_up(h, w_up, conv_w, conv_b, d_ff, tm, tn):
    t, d = h.shape
    k = conv_w.shape[0]
    return pl.pallas_call(
        _ffn_up_kernel,
        grid=(d_ff // tn, t // tm),
        in_specs=[
            pl.BlockSpec((tm, d), lambda j, m: (m, 0)),
            pl.BlockSpec((d, tn), lambda j, m: (0, j)),
            pl.BlockSpec((d, tn), lambda j, m: (0, d_ff // tn + j)),
            pl.BlockSpec((k, tn), lambda j, m: (0, j)),
            pl.BlockSpec((1, tn), lambda j, m: (0, j)),
        ],
        out_specs=pl.BlockSpec((tm, tn), lambda j, m: (m, j)),
        out_shape=jax.ShapeDtypeStruct((t, d_ff), BF16),
        scratch_shapes=[pltpu.VMEM((CARRY + tm, tn), F32),
                        pltpu.VMEM((d, tn), BF16), pltpu.VMEM((d, tn), BF16)],
        compiler_params=_params(("parallel", "arbitrary")),
        name="ffn_up",
    )(h, w_up, w_up, conv_w, conv_b)


def _ffn_down_kernel(a_ref, w_ref, x_ref, post_ref, o_ref):
    kstep = pl.program_id(1)

    @pl.when(kstep == 0)
    def _():
        o_ref[...] = jnp.zeros_like(o_ref)

    d = o_ref.shape[1]
    slabs = [slice(c, c + FFN_DOWN_SLAB) for c in range(0, d, FFN_DOWN_SLAB)]
    a = a_ref[...]
    for cs in slabs:
        o_ref[:, cs] += _dot(a, w_ref[:, cs])

    @pl.when(kstep == pl.num_programs(1) - 1)
    def _():
        ssq = sum(jnp.sum(o_ref[:, cs] * o_ref[:, cs], axis=-1, keepdims=True) for cs in slabs)
        inv = lax.rsqrt(ssq / d + EPS)
        for cs in slabs:
            o_ref[:, cs] = x_ref[:, cs] + o_ref[:, cs] * inv * post_ref[:, cs]


def _ffn_down(act, w_down, x1, post_w, tm, tk):
    t, d = x1.shape
    d_ff = act.shape[1]
    return pl.pallas_call(
        _ffn_down_kernel,
        grid=(t // tm, d_ff // tk),
        in_specs=[
            pl.BlockSpec((tm, tk), lambda m, k: (m, k)),
            pl.BlockSpec((tk, d), lambda m, k: (k, 0)),
            pl.BlockSpec((tm, d), lambda m, k: (m, 0)),
            pl.BlockSpec((1, d), lambda m, k: (0, 0)),
        ],
        out_specs=pl.BlockSpec((tm, d), lambda m, k: (m, 0)),
        out_shape=jax.ShapeDtypeStruct((t, d), F32),
        compiler_params=_params(("parallel", "arbitrary"), VMEM_LIMIT_LARGE),
        name="ffn_down",
    )(act, w_down, x1, post_w)


def _layer(x, w_in, mix_pre, mix_post, lb_table, hg_norm, conv_w, conv_b, dt_bias, a_log, d_skip,
           ssm_norm, w_hg, w_ssm, w_out, ffn_pre, ffn_post, w_up, ffn_cw, ffn_cb, w_down):
    t, d = x.shape
    hg_v = w_hg.shape[0]
    hg_heads = hg_v // HG_DV
    d_inner = w_ssm.shape[0]
    ssm_heads = a_log.shape[0]
    hpg = ssm_heads // SSM_GROUPS
    gw = d_inner // SSM_GROUPS
    d_ff = w_down.shape[0]
    conv_dim = conv_w.shape[1]

    main_n = 4 * hg_v + d_inner + conv_dim
    dt_off = main_n
    gate_off = dt_off + ssm_heads
    z_off = 4 * hg_v
    xbc_off = z_off + d_inner

    w_t = w_in.T
    h, dt_t = _norm_dt(x, mix_pre.reshape(1, d), w_t[dt_off:gate_off], tm=512)
    tm, tn = INPROJ_ROWS, INPROJ_TILE
    qh = _seg_proj(h, w_t, 0, hg_v, "qscale", (), tm, tn, "inproj_q")
    lf = _seg_proj(h, w_t, hg_v, hg_v, "logf", (lb_table,), tm, tn, "inproj_f")
    v = _proj_nt(h, w_t, 2 * hg_v, hg_v, tm, tn, "inproj_i")
    go = _seg_proj(h, w_t, 3 * hg_v, hg_v, "silu", (), tm, tn, "inproj_g")
    zs = _seg_proj(h, w_t, z_off, d_inner, "silu", (), tm, tn, "inproj_z")
    xbc = _seg_proj(h, w_t, xbc_off, conv_dim, "conv", (conv_w, conv_b.reshape(1, conv_dim)), tm, tn,
                    "inproj_xbc")
    gates = _proj_nt(h, w_t, gate_off, 2 * d, tm, tn, "inproj_gates")

    y_hg, w_hg16, w_ssm16, w_out16, w_down16 = _hgrn2(
        qh, lf, v, go, hg_norm.reshape(1, HG_DV), hg_heads, HG_BLOCK, HG_CHUNK, HG_HEADS_PER_STEP,
        cast_weights=(w_hg, w_ssm, w_out, w_down))

    dt_hm = dt_t.reshape(SSM_GROUPS, hpg, t)
    y_ssm = _ssd(
        zs, xbc, dt_hm,
        dt_bias.reshape(SSM_GROUPS, hpg, 1), a_log.reshape(SSM_GROUPS, hpg, 1),
        jnp.repeat(d_skip, SSM_HEADDIM).reshape(SSM_GROUPS, 1, gw),
        ssm_norm.reshape(SSM_GROUPS, 1, gw), d_inner)

    mixed = _merge(y_hg, y_ssm, w_hg16, w_ssm16, gates, tm=1024, tn=512)
    x1, h2 = _out_proj(mixed, w_out16, x, mix_post.reshape(1, d), ffn_pre.reshape(1, d), tm=512)
    act = _ffn_up(h2, w_up, ffn_cw, ffn_cb.reshape(1, d_ff), d_ff, tm=1024, tn=512)
    return _ffn_down(act, w_down16, x1, ffn_post.reshape(1, d), tm=1024, tk=FFN_DOWN_TK)


def kernel(x, w_in, mix_pre_norm, mix_post_norm, hg_lb_table, hg_out_norm, ssm_conv_w, ssm_conv_b,
           ssm_dt_bias, ssm_A_log, ssm_D, ssm_out_norm, w_branch_hg, w_branch_ssm, w_out,
           ffn_pre_norm, ffn_post_norm, ffn_w_up, ffn_conv_w, ffn_conv_b, ffn_w_down):
    bsz, t, d = x.shape
    depth = w_in.shape[0]
    assert depth == 1 and hg_lb_table.shape[0] == 2, "forget-gate lower bound is computed for one layer"
    outs = []
    for b in range(bsz):
        xb = x[b]
        for l in range(depth):
            xb = _layer(xb, w_in[l], mix_pre_norm[l], mix_post_norm[l], hg_lb_table, hg_out_norm[l],
                        ssm_conv_w[l], ssm_conv_b[l], ssm_dt_bias[l], ssm_A_log[l], ssm_D[l],
                        ssm_out_norm[l], w_branch_hg[l], w_branch_ssm[l], w_out[l],
                        ffn_pre_norm[l], ffn_post_norm[l], ffn_w_up[l], ffn_conv_w[l], ffn_conv_b[l],
                        ffn_w_down[l])
        outs.append(xb)
    return jnp.stack(outs, axis=0)
```

```python
import functools

import numpy as np
import jax
import jax.numpy as jnp
from jax import lax
from jax.experimental import pallas as pl
from jax.experimental.pallas import tpu as pltpu

F32 = jnp.float32
BF16 = jnp.bfloat16
EPS = 1e-6

HG_DK = 128
HG_DV = 128
SSM_HEADDIM = 64
SSM_GROUPS = 8
SSM_DSTATE = 128
SSM_CONV = 4
FFN_CONV = 3

INPROJ_TILE = 1024
INPROJ_ROWS = 1024
HG_CHUNK = 128
HG_BLOCK = 1024
HG_HEADS_PER_STEP = 8
HG_VPU_LEVEL_MIN = 8
SSD_CHUNK = 256
SSD_GROUPS_PER_STEP = 8
CARRY = 8
CONV_ROWS = 256
FFN_DOWN_SLAB = 512

VMEM_LIMIT = 48 * 1024 * 1024
VMEM_LIMIT_LARGE = 56 * 1024 * 1024
LOG2E = 1.4426950408889634


def _params(sem, vmem=VMEM_LIMIT):
    return pltpu.CompilerParams(dimension_semantics=sem, vmem_limit_bytes=vmem)


def _dot(a, b):
    return jnp.dot(a, b, preferred_element_type=F32)


def _dot_nt(a, b):
    return lax.dot_general(a, b, (((1,), (1,)), ((), ())), preferred_element_type=F32)


def _dot_tn(a, b):
    return lax.dot_general(a, b, (((0,), (0,)), ((), ())), preferred_element_type=F32)


def _split_bf16(a):
    hi = a.astype(BF16)
    lo = (a - hi.astype(F32)).astype(BF16)
    return hi, lo


def _sigmoid(a):
    return 0.5 * jnp.tanh(0.5 * a) + 0.5


def _silu(a):
    u = 0.5 * a
    return u * (jnp.tanh(u) + 1.0)


def _gelu_tanh(a):
    c0 = float(np.sqrt(2.0 / np.pi))
    return a * (0.5 * jnp.tanh(a * (c0 + (c0 * 0.044715) * (a * a))) + 0.5)


def _rms(x, w):
    return x * lax.rsqrt(jnp.mean(x * x, axis=-1, keepdims=True) + EPS) * w


def _norm_dt_kernel(x_ref, nw_ref, wdt_ref, h_ref, dt_ref):
    h = _rms(x_ref[...], nw_ref[...]).astype(BF16)
    h_ref[...] = h
    dt_ref[...] = _dot_nt(wdt_ref[...].astype(BF16), h)


def _norm_dt(x, nw, wdt_t, tm):
    t, d = x.shape
    nh = wdt_t.shape[0]
    return pl.pallas_call(
        _norm_dt_kernel,
        grid=(t // tm,),
        in_specs=[
            pl.BlockSpec((tm, d), lambda m: (m, 0)),
            pl.BlockSpec((1, d), lambda m: (0, 0)),
            pl.BlockSpec((nh, d), lambda m: (0, 0)),
        ],
        out_specs=[pl.BlockSpec((tm, d), lambda m: (m, 0)),
                   pl.BlockSpec((nh, tm), lambda m: (0, m))],
        out_shape=[jax.ShapeDtypeStruct((t, d), BF16), jax.ShapeDtypeStruct((nh, t), F32)],
        compiler_params=_params(("parallel",)),
        name="norm_dt",
    )(x, nw, wdt_t)


def _proj_nt_kernel(h_ref, w_ref, o_ref, wb_ref):
    @pl.when(pl.program_id(1) == 0)
    def _():
        wb_ref[...] = w_ref[...].astype(BF16)

    o_ref[...] = _dot_nt(h_ref[...], wb_ref[...]).astype(o_ref.dtype)


def _proj_nt(h, w_t, row_off, n, tm, tn, name):
    t, d = h.shape
    if row_off % tn == 0:
        w_spec = pl.BlockSpec((tn, d), lambda j, m: (row_off // tn + j, 0))
    else:
        assert row_off % 8 == 0 and tn % 8 == 0
        w_spec = pl.BlockSpec((pl.Element(tn), pl.Element(d)),
                              lambda j, m: (pl.multiple_of(row_off + j * tn, 8), 0))
    return pl.pallas_call(
        _proj_nt_kernel,
        grid=(n // tn, t // tm),
        in_specs=[pl.BlockSpec((tm, d), lambda j, m: (m, 0)), w_spec],
        out_specs=pl.BlockSpec((tm, tn), lambda j, m: (m, j)),
        out_shape=jax.ShapeDtypeStruct((t, n), BF16),
        scratch_shapes=[pltpu.VMEM((tn, d), BF16)],
        compiler_params=_params(("parallel", "arbitrary")),
        name=name,
    )(h, w_t)


def _conv_rows(ext_ref, cur, r0, w, b, ksize):
    rows = cur.shape[0]
    ext_ref[CARRY + r0:CARRY + r0 + rows, :] = cur
    acc = cur * w[ksize - 1:ksize, :] + b
    for j in range(1, ksize):
        acc = acc + ext_ref[CARRY + r0 - j:CARRY + r0 - j + rows, :] * w[ksize - 1 - j:ksize - j, :]
    return acc


def _conv_keep_tail(ext_ref, q):
    ext_ref[0:CARRY, :] = ext_ref[q:q + CARRY, :]


_STAGE_EXTRAS = {"qscale": 0, "silu": 0, "logf": 1, "conv": 2}


def _seg_proj_kernel(h_ref, w_ref, *rest, kind):
    nx = _STAGE_EXTRAS[kind]
    extra = rest[:nx]
    o_ref, wb_ref = rest[nx:nx + 2]
    first = pl.program_id(1) == 0

    @pl.when(first)
    def _():
        wb_ref[...] = w_ref[...].astype(BF16)

    if kind == "conv":
        ext_ref = rest[nx + 2]

        @pl.when(first)
        def _():
            ext_ref[0:CARRY, :] = jnp.zeros((CARRY, ext_ref.shape[1]), F32)

    if kind == "logf":
        tab = extra[0][...]
        te = jnp.exp(tab - jnp.max(tab, axis=0, keepdims=True))
        lb = te[0:1, :] / jnp.sum(te, axis=0, keepdims=True)
    elif kind == "conv":
        w = extra[0][...]
        bias = extra[1][...]
    wb = wb_ref[...]
    step = CONV_ROWS if kind == "conv" else o_ref.shape[0]
    for r0 in range(0, o_ref.shape[0], step):
        acc = _dot_nt(h_ref[r0:r0 + step, :], wb)
        if kind == "qscale":
            out = _silu(acc) * HG_DK ** -0.5
        elif kind == "silu":
            out = _silu(acc)
        elif kind == "logf":
            out = jnp.log(lb + (1.0 - lb) * _sigmoid(acc)) * LOG2E
        else:
            out = _silu(_conv_rows(ext_ref, acc, r0, w, bias, SSM_CONV))
        o_ref[r0:r0 + step, :] = out.astype(o_ref.dtype)
    if kind == "conv":
        _conv_keep_tail(ext_ref, o_ref.shape[0])


def _seg_proj(h, w_t, row_off, n, kind, extras, tm, tn, name):
    t, d = h.shape
    assert row_off % tn == 0 and len(extras) == _STAGE_EXTRAS[kind]
    scratch = [pltpu.VMEM((tn, d), BF16)]
    if kind == "conv":
        scratch.append(pltpu.VMEM((CARRY + tm, tn), F32))
    return pl.pallas_call(
        functools.partial(_seg_proj_kernel, kind=kind),
        grid=(n // tn, t // tm),
        in_specs=[
            pl.BlockSpec((tm, d), lambda j, m: (m, 0)),
            pl.BlockSpec((tn, d), lambda j, m: (row_off // tn + j, 0)),
        ] + [pl.BlockSpec((e.shape[0], tn), lambda j, m: (0, j)) for e in extras],
        out_specs=pl.BlockSpec((tm, tn), lambda j, m: (m, j)),
        out_shape=jax.ShapeDtypeStruct((t, n), BF16),
        scratch_shapes=scratch,
        compiler_params=_params(("parallel", "arbitrary")),
        name=name,
    )(h, w_t, *extras)


def _hg_constants(c):
    levels = []
    h = c // 2
    while h >= 1:
        levels.append(h)
        h //= 2
    t = np.arange(c)[:, None]
    u = np.arange(c)[None, :]
    mats = [(u <= t)]
    masks = []
    for h in levels:
        blk = 2 * h
        mid = (t // blk) * blk + h
        second = (t % blk) >= h
        if h < HG_VPU_LEVEL_MIN:
            mats.append(np.where(second, (u >= mid) & (u <= t), (u > t) & (u < mid)))
        s = u
        masks.append(((t // blk) == (s // blk)) & second & ((s % blk) < h))
    masks.append(t == u)
    pm = np.concatenate(mats, axis=0).astype(np.float32)
    mk = np.stack(masks, axis=0).astype(np.float32)
    return levels, pm, mk


def _level_exponents(b, h):
    c = b.shape[0]
    parts = []
    for start in range(0, c, 2 * h):
        mid = start + h
        ref = b[mid - 1:mid, :]
        parts.append(ref - b[start:mid, :])
        parts.append(b[mid:mid + h, :] - ref)
    return jnp.concatenate(parts, axis=0)


def _blockdiag(a, b):
    top = jnp.concatenate([a, jnp.zeros((a.shape[0], b.shape[1]), a.dtype)], axis=1)
    bot = jnp.concatenate([jnp.zeros((b.shape[0], a.shape[1]), b.dtype), b], axis=1)
    return jnp.concatenate([top, bot], axis=0)


def _blockdiag_t(kpair):
    kt = kpair.T
    half = kt.shape[0] // 2
    return _blockdiag(kt[0:half], kt[half:])


def _hgrn2_kernel(q_ref, lf_ref, i_ref, g_ref, nw_ref, pm_ref, mk_ref, *rest, chunk, levels, ncast):
    c = chunk
    w_in_refs = rest[:ncast]
    o_ref = rest[ncast]
    w_out_refs = rest[ncast + 1:2 * ncast + 1]
    st_ref, ex_ref = rest[2 * ncast + 1:]
    for src, dst in zip(w_in_refs, w_out_refs):
        dst[...] = src[...].astype(BF16)

    @pl.when(pl.program_id(1) == 0)
    def _():
        st_ref[...] = jnp.zeros_like(st_ref)

    nw = nw_ref[...]
    pm = pm_ref[...]
    nh = st_ref.shape[0]
    dk = HG_DK

    def body(ci, carry):
        r0 = pl.multiple_of(ci * c, c)
        rows = pl.ds(r0, c)
        for p in range(nh // 2):
            ha, hb = 2 * p, 2 * p + 1
            sa = slice(ha * dk, (ha + 1) * dk)
            sb = slice(hb * dk, (hb + 1) * dk)
            ps = slice(ha * dk, (hb + 1) * dk)
            lo, hi = slice(0, dk), slice(dk, 2 * dk)
            qp = q_ref[rows, ps]
            lfp = lf_ref[rows, ps]
            kp = (1.0 - jnp.exp2(lfp.astype(F32))).astype(BF16)
            ex_ref[:, ps] = _dot(pm, lfp)
            b = ex_ref[0:c, ps]
            b_last = b[c - 1:c, :]
            st_a = st_ref[ha]
            st_b = st_ref[hb]
            o = _dot_nt(qp * jnp.exp2(b).astype(BF16),
                        _blockdiag(st_a.astype(BF16), st_b.astype(BF16)))
            sc = mk_ref[len(levels)] * _dot(qp, _blockdiag_t(kp)).astype(BF16)
            n_vpu = sum(h >= HG_VPU_LEVEL_MIN for h in levels)
            for l, h in enumerate(levels):
                if h >= HG_VPU_LEVEL_MIN:
                    ex = _level_exponents(b, h)
                else:
                    ex = ex_ref[(l - n_vpu + 1) * c:(l - n_vpu + 2) * c, ps]
                e = jnp.exp2(ex).astype(BF16)
                ke = kp * e
                s = _dot(qp * e, _blockdiag_t(ke))
                sc = sc + mk_ref[l] * s.astype(BF16)
            o = o + _dot(sc, _blockdiag(i_ref[rows, sa], i_ref[rows, sb]))
            kdec = kp * jnp.exp2(b_last - b).astype(BF16)
            sdec = jnp.exp2(b_last)
            st_ref[ha] = st_a * sdec[:, lo] + _dot_tn(i_ref[rows, sa], kdec[:, lo])
            st_ref[hb] = st_b * sdec[:, hi] + _dot_tn(i_ref[rows, sb], kdec[:, hi])
            o_ref[rows, sa] = (_rms(o[:, lo], nw) * g_ref[rows, sa].astype(F32)).astype(o_ref.dtype)
            o_ref[rows, sb] = (_rms(o[:, hi], nw) * g_ref[rows, sb].astype(F32)).astype(o_ref.dtype)
        return carry

    lax.fori_loop(0, q_ref.shape[0] // c, body, 0)


def _hgrn2(qh, lf, v, gate, norm_w, n_heads, tb, chunk, hps, cast_weights):
    t = qh.shape[0]
    gate_off = 0
    n_steps = (n_heads // hps) * (t // tb)
    for wgt in cast_weights:
        assert wgt.shape[0] % (16 * n_steps) == 0, "weight rows must split into bf16-tile-aligned slices"
    w_rows = [wgt.shape[0] // n_steps for wgt in cast_weights]
    w_specs = [pl.BlockSpec((r, wgt.shape[1]), lambda h, i: (h * (t // tb) + i, 0))
               for r, wgt in zip(w_rows, cast_weights)]
    levels, pm, mk = _hg_constants(chunk)
    pm = jnp.asarray(pm, BF16)
    mk = jnp.asarray(np.concatenate([mk, mk], axis=2), BF16)
    w = hps * HG_DK
    nblk = n_heads // hps
    col = lambda off: (lambda h, i: (i, off // w + h))
    const2 = lambda h, i: (0, 0)
    return pl.pallas_call(
        functools.partial(_hgrn2_kernel, chunk=chunk, levels=tuple(levels), ncast=len(cast_weights)),
        grid=(nblk, t // tb),
        in_specs=[
            pl.BlockSpec((tb, w), col(0)),
            pl.BlockSpec((tb, w), col(0)),
            pl.BlockSpec((tb, w), col(0)),
            pl.BlockSpec((tb, w), col(gate_off)),
            pl.BlockSpec((1, HG_DV), const2),
            pl.BlockSpec(pm.shape, const2),
            pl.BlockSpec(mk.shape, lambda h, i: (0, 0, 0)),
        ] + w_specs,
        out_specs=[pl.BlockSpec((tb, w), lambda h, i: (i, h))] + w_specs,
        out_shape=[jax.ShapeDtypeStruct((t, n_heads * HG_DV), BF16)]
        + [jax.ShapeDtypeStruct(wgt.shape, BF16) for wgt in cast_weights],
        scratch_shapes=[pltpu.VMEM((hps, HG_DV, HG_DK), F32),
                        pltpu.VMEM((pm.shape[0], w), F32)],
        compiler_params=_params(("parallel", "arbitrary")),
        name="hgrn2_scan",
    )(qh, lf, v, gate, norm_w, pm, mk, *cast_weights)


def _ssd_expand_mat(hpg):
    ea = np.zeros((128, hpg * SSM_HEADDIM), np.float32)
    for h in range(hpg):
        ea[h, h * SSM_HEADDIM:(h + 1) * SSM_HEADDIM] = 1.0
    return ea


def _ssd_kernel(z_ref, x_ref, b_ref, c_ref, dt_ref, dtb_ref, alog_ref, dskip_ref, nw_ref,
                triu_ref, ea_ref, o_ref, s_ref):
    @pl.when(pl.program_id(1) == 0)
    def _():
        s_ref[...] = jnp.zeros_like(s_ref)

    gw = s_ref.shape[2]
    n = s_ref.shape[1]
    for gi in range(s_ref.shape[0]):
        _ssd_group(z_ref.at[:, gi * gw:(gi + 1) * gw], x_ref.at[:, gi * gw:(gi + 1) * gw],
                   b_ref.at[:, gi * n:(gi + 1) * n], c_ref.at[:, gi * n:(gi + 1) * n],
                   dt_ref.at[gi], dtb_ref.at[gi], alog_ref.at[gi], dskip_ref.at[gi], nw_ref.at[gi],
                   triu_ref, ea_ref, o_ref.at[:, gi * gw:(gi + 1) * gw], s_ref.at[gi])


def _ssd_group(z_ref, x_ref, b_ref, c_ref, dt_ref, dtb_ref, alog_ref, dskip_ref, nw_ref,
               triu_ref, ea_ref, o_ref, s_ref):
    q = x_ref.shape[0]
    hpg = dt_ref.shape[0]
    xs = x_ref[...].astype(F32)
    bm16 = b_ref[...]
    cm16 = c_ref[...]

    dt_r = jax.nn.softplus(dt_ref[...] + dtb_ref[...])
    da_r = dt_r * (-jnp.exp(alog_ref[...]) * LOG2E)
    da_hi, da_lo = _split_bf16(da_r)
    triu = triu_ref[...]
    acum_r = _dot(da_hi, triu) + _dot(da_lo, triu)
    pad = jnp.zeros((128 - 2 * hpg, q), F32)
    cols = jnp.concatenate([acum_r, dt_r, pad], axis=0).T

    lane = lax.broadcasted_iota(jnp.int32, (q, 128), 1)
    acol = jnp.where(lane < hpg, cols, 0.0)
    a_last = acol[q - 1:q, :]
    dtwd = jnp.exp2(a_last - acol) * pltpu.roll(cols, 128 - hpg, axis=1)
    fx = _dot(jnp.concatenate([jnp.exp2(acol).astype(BF16), dtwd.astype(BF16)], axis=0), ea_ref[...])
    ea_x = fx[0:q]
    dtwd_x = fx[q:2 * q]
    sd_x = _dot(jnp.broadcast_to(jnp.exp2(a_last), (8, 128)).astype(BF16), ea_ref[...])[0:1, :]

    xs16 = x_ref[...]
    bmt = bm16.T
    cb16 = _dot(cm16, bmt).astype(BF16)
    cs = _dot(cm16, s_ref[...].astype(BF16))
    row = lax.broadcasted_iota(jnp.int32, (q, q), 0)
    colid = lax.broadcasted_iota(jnp.int32, (q, q), 1)
    causal = row >= colid
    lo_half = lane < SSM_HEADDIM

    y_parts = []
    for p in range(hpg // 2):
        ms = []
        for h in (2 * p, 2 * p + 1):
            dm = cols[:, h:h + 1] - acum_r[h:h + 1, :]
            lm16 = jnp.exp2(jnp.where(causal, dm, -1e30)).astype(BF16)
            ms.append(cb16 * lm16 * dt_r[h:h + 1, :].astype(BF16))
        yy = _dot(jnp.concatenate(ms, axis=0), xs16[:, p * 128:(p + 1) * 128])
        y_parts.append(jnp.where(lo_half, yy[0:q], yy[q:2 * q]))

    y = jnp.concatenate(y_parts, axis=1) + cs * ea_x
    s_ref[...] = s_ref[...] * sd_x + _dot(bmt, (xs * dtwd_x).astype(BF16))

    y = y + dskip_ref[...] * xs
    y = y * z_ref[...].astype(F32)
    o_ref[...] = _rms(y, nw_ref[...]).astype(o_ref.dtype)


def _ssd(zs, xbc, dt_hm, dt_bias, a_log, d_skip, norm_w, d_inner):
    t = xbc.shape[0]
    z_off = xbc_off = 0
    q = SSD_CHUNK
    g = SSM_GROUPS
    n = SSM_DSTATE
    gw = d_inner // g
    hpg = gw // SSM_HEADDIM
    triu = jnp.asarray(np.triu(np.ones((q, q), np.float32)), BF16)
    ea = jnp.asarray(_ssd_expand_mat(hpg), BF16)
    c2 = lambda gi, i: (0, 0)
    gs = SSD_GROUPS_PER_STEP
    assert g % gs == 0 and d_inner % (gs * n) == 0, "B / C column blocks must be block-aligned"
    return pl.pallas_call(
        _ssd_kernel,
        grid=(g // gs, t // q),
        in_specs=[
            pl.BlockSpec((q, gs * gw), lambda gi, i: (i, z_off // (gs * gw) + gi)),
            pl.BlockSpec((q, gs * gw), lambda gi, i: (i, xbc_off // (gs * gw) + gi)),
            pl.BlockSpec((q, gs * n), lambda gi, i: (i, (xbc_off + d_inner) // (gs * n) + gi)),
            pl.BlockSpec((q, gs * n), lambda gi, i: (i, (xbc_off + d_inner + g * n) // (gs * n) + gi)),
            pl.BlockSpec((gs, hpg, q), lambda gi, i: (gi, 0, i)),
            pl.BlockSpec((gs, hpg, 1), lambda gi, i: (gi, 0, 0)),
            pl.BlockSpec((gs, hpg, 1), lambda gi, i: (gi, 0, 0)),
            pl.BlockSpec((gs, 1, gw), lambda gi, i: (gi, 0, 0)),
            pl.BlockSpec((gs, 1, gw), lambda gi, i: (gi, 0, 0)),
            pl.BlockSpec((q, q), c2),
            pl.BlockSpec(ea.shape, c2),
        ],
        out_specs=pl.BlockSpec((q, gs * gw), lambda gi, i: (i, gi)),
        out_shape=jax.ShapeDtypeStruct((t, d_inner), BF16),
        scratch_shapes=[pltpu.VMEM((gs, n, gw), F32)],
        compiler_params=_params(("parallel", "arbitrary")),
        name="ssd_scan",
    )(zs, xbc, xbc, xbc, dt_hm, dt_bias, a_log, d_skip, norm_w, triu, ea)


def _merge_kernel(yh_ref, ys_ref, wh_ref, ws_ref, gh_ref, gs_ref, o_ref):
    a = _dot(yh_ref[...], wh_ref[...])
    b = _dot(ys_ref[...], ws_ref[...])
    gh = _sigmoid(gh_ref[...].astype(F32))
    gs = _sigmoid(gs_ref[...].astype(F32))
    o_ref[...] = (gh * a + gs * b).astype(o_ref.dtype)


def _merge(y_hg, y_ssm, w_hg, w_ssm, gates, tm, tn):
    t = y_hg.shape[0]
    d = w_hg.shape[1]
    return pl.pallas_call(
        _merge_kernel,
        grid=(t // tm, d // tn),
        in_specs=[
            pl.BlockSpec((tm, y_hg.shape[1]), lambda m, j: (m, 0)),
            pl.BlockSpec((tm, y_ssm.shape[1]), lambda m, j: (m, 0)),
            pl.BlockSpec((w_hg.shape[0], tn), lambda m, j: (0, j)),
            pl.BlockSpec((w_ssm.shape[0], tn), lambda m, j: (0, j)),
            pl.BlockSpec((tm, tn), lambda m, j: (m, j)),
            pl.BlockSpec((tm, tn), lambda m, j: (m, d // tn + j)),
        ],
        out_specs=pl.BlockSpec((tm, tn), lambda m, j: (m, j)),
        out_shape=jax.ShapeDtypeStruct((t, d), BF16),
        compiler_params=_params(("parallel", "arbitrary")),
        name="branch_merge",
    )(y_hg, y_ssm, w_hg, w_ssm, gates, gates)


def _out_kernel(a_ref, w_ref, x_ref, post_ref, pre_ref, x1_ref, h_ref):
    w = w_ref[...]
    for r0 in range(0, x_ref.shape[0], CONV_ROWS):
        rows = slice(r0, r0 + CONV_ROWS)
        x1 = x_ref[rows, :] + _rms(_dot(a_ref[rows, :], w), post_ref[...])
        x1_ref[rows, :] = x1
        h_ref[rows, :] = _rms(x1, pre_ref[...]).astype(BF16)


def _out_proj(mixed, w_out, x, post_w, pre_w, tm):
    t, d = x.shape
    c2 = lambda m: (0, 0)
    row = lambda m: (m, 0)
    return pl.pallas_call(
        _out_kernel,
        grid=(t // tm,),
        in_specs=[
            pl.BlockSpec((tm, d), row),
            pl.BlockSpec((d, d), c2),
            pl.BlockSpec((tm, d), row),
            pl.BlockSpec((1, d), c2),
            pl.BlockSpec((1, d), c2),
        ],
        out_specs=[pl.BlockSpec((tm, d), row), pl.BlockSpec((tm, d), row)],
        out_shape=[jax.ShapeDtypeStruct((t, d), F32), jax.ShapeDtypeStruct((t, d), BF16)],
        compiler_params=_params(("parallel",)),
        name="out_proj",
    )(mixed, w_out, x, post_w, pre_w)


def _ffn_up_kernel(h_ref, wg_ref, wu_ref, cw_ref, cb_ref, o_ref, ext_ref, wgb_ref, wub_ref):
    @pl.when(pl.program_id(1) == 0)
    def _():
        wgb_ref[...] = wg_ref[...].astype(BF16)
        wub_ref[...] = wu_ref[...].astype(BF16)
        ext_ref[0:CARRY, :] = jnp.zeros((CARRY, ext_ref.shape[1]), F32)

    h = h_ref[...]
    gate = _dot(h, wgb_ref[...])
    up = _dot(h, wub_ref[...])
    conv = _conv_rows(ext_ref, gate, 0, cw_ref[...], cb_ref[...], FFN_CONV)
    _conv_keep_tail(ext_ref, o_ref.shape[0])
    o_ref[...] = (_gelu_tanh(conv) * up).astype(o_ref.dtype)


def _ffn_up(h, w_up, conv_w, conv_b, d_ff, tm, tn):
    t, d = h.shape
    k = conv_w.shape[0]
    return pl.pallas_call(
        _ffn_up_kernel,
        grid=(d_ff // tn, t // tm),
        in_specs=[
            pl.BlockSpec((tm, d), lambda j, m: (m, 0)),
            pl.BlockSpec((d, tn), lambda j, m: (0, j)),
            pl.BlockSpec((d, tn), lambda j, m: (0, d_ff // tn + j)),
            pl.BlockSpec((k, tn), lambda j, m: (0, j)),
            pl.BlockSpec((1, tn), lambda j, m: (0, j)),
        ],
        out_specs=pl.BlockSpec((tm, tn), lambda j, m: (m, j)),
        out_shape=jax.ShapeDtypeStruct((t, d_ff), BF16),
        scratch_shapes=[pltpu.VMEM((CARRY + tm, tn), F32),
                        pltpu.VMEM((d, tn), BF16), pltpu.VMEM((d, tn), BF16)],
        compiler_params=_params(("parallel", "arbitrary")),
        name="ffn_up",
    )(h, w_up, w_up, conv_w, conv_b)


def _ffn_down_kernel(a_ref, w_ref, x_ref, post_ref, o_ref):
    d = o_ref.shape[1]
    slabs = [slice(c, c + FFN_DOWN_SLAB) for c in range(0, d, FFN_DOWN_SLAB)]
    a = a_ref[...]
    for cs in slabs:
        o_ref[:, cs] = _dot(a, w_ref[:, cs])
    ssq = sum(jnp.sum(o_ref[:, cs] * o_ref[:, cs], axis=-1, keepdims=True) for cs in slabs)
    inv = lax.rsqrt(ssq / d + EPS)
    for cs in slabs:
        o_ref[:, cs] = x_ref[:, cs] + o_ref[:, cs] * inv * post_ref[:, cs]


def _ffn_down(act, w_down, x1, post_w, tm):
    t, d = x1.shape
    d_ff = act.shape[1]
    return pl.pallas_call(
        _ffn_down_kernel,
        grid=(t // tm,),
        in_specs=[
            pl.BlockSpec((tm, d_ff), lambda m: (m, 0)),
            pl.BlockSpec((d_ff, d), lambda m: (0, 0), pipeline_mode=pl.Buffered(1)),
            pl.BlockSpec((tm, d), lambda m: (m, 0)),
            pl.BlockSpec((1, d), lambda m: (0, 0)),
        ],
        out_specs=pl.BlockSpec((tm, d), lambda m: (m, 0)),
        out_shape=jax.ShapeDtypeStruct((t, d), F32),
        compiler_params=_params(("parallel",), VMEM_LIMIT_LARGE),
        name="ffn_down",
    )(act, w_down, x1, post_w)


def _layer(x, w_in, mix_pre, mix_post, lb_table, hg_norm, conv_w, conv_b, dt_bias, a_log, d_skip,
           ssm_norm, w_hg, w_ssm, w_out, ffn_pre, ffn_post, w_up, ffn_cw, ffn_cb, w_down):
    t, d = x.shape
    hg_v = w_hg.shape[0]
    hg_heads = hg_v // HG_DV
    d_inner = w_ssm.shape[0]
    ssm_heads = a_log.shape[0]
    hpg = ssm_heads // SSM_GROUPS
    gw = d_inner // SSM_GROUPS
    d_ff = w_down.shape[0]
    conv_dim = conv_w.shape[1]

    main_n = 4 * hg_v + d_inner + conv_dim
    dt_off = main_n
    gate_off = dt_off + ssm_heads
    z_off = 4 * hg_v
    xbc_off = z_off + d_inner

    w_t = w_in.T
    h, dt_t = _norm_dt(x, mix_pre.reshape(1, d), w_t[dt_off:gate_off], tm=512)
    tm, tn = INPROJ_ROWS, INPROJ_TILE
    qh = _seg_proj(h, w_t, 0, hg_v, "qscale", (), tm, tn, "inproj_q")
    lf = _seg_proj(h, w_t, hg_v, hg_v, "logf", (lb_table,), tm, tn, "inproj_f")
    v = _proj_nt(h, w_t, 2 * hg_v, hg_v, tm, tn, "inproj_i")
    go = _seg_proj(h, w_t, 3 * hg_v, hg_v, "silu", (), tm, tn, "inproj_g")
    zs = _seg_proj(h, w_t, z_off, d_inner, "silu", (), tm, tn, "inproj_z")
    xbc = _seg_proj(h, w_t, xbc_off, conv_dim, "conv", (conv_w, conv_b.reshape(1, conv_dim)), tm, tn,
                    "inproj_xbc")
    gates = _proj_nt(h, w_t, gate_off, 2 * d, tm, tn, "inproj_gates")

    y_hg, w_hg16, w_ssm16, w_out16, w_down16 = _hgrn2(
        qh, lf, v, go, hg_norm.reshape(1, HG_DV), hg_heads, HG_BLOCK, HG_CHUNK, HG_HEADS_PER_STEP,
        cast_weights=(w_hg, w_ssm, w_out, w_down))

    dt_hm = dt_t.reshape(SSM_GROUPS, hpg, t)
    y_ssm = _ssd(
        zs, xbc, dt_hm,
        dt_bias.reshape(SSM_GROUPS, hpg, 1), a_log.reshape(SSM_GROUPS, hpg, 1),
        jnp.repeat(d_skip, SSM_HEADDIM).reshape(SSM_GROUPS, 1, gw),
        ssm_norm.reshape(SSM_GROUPS, 1, gw), d_inner)

    mixed = _merge(y_hg, y_ssm, w_hg16, w_ssm16, gates, tm=1024, tn=512)
    x1, h2 = _out_proj(mixed, w_out16, x, mix_post.reshape(1, d), ffn_pre.reshape(1, d), tm=512)
    act = _ffn_up(h2, w_up, ffn_cw, ffn_cb.reshape(1, d_ff), d_ff, tm=1024, tn=512)
    return _ffn_down(act, w_down16, x1, ffn_post.reshape(1, d), tm=512)


def kernel(x, w_in, mix_pre_norm, mix_post_norm, hg_lb_table, hg_out_norm, ssm_conv_w, ssm_conv_b,
           ssm_dt_bias, ssm_A_log, ssm_D, ssm_out_norm, w_branch_hg, w_branch_ssm, w_out,
           ffn_pre_norm, ffn_post_norm, ffn_w_up, ffn_conv_w, ffn_conv_b, ffn_w_down):
    bsz, t, d = x.shape
    depth = w_in.shape[0]
    assert depth == 1 and hg_lb_table.shape[0] == 2, "forget-gate lower bound is computed for one layer"
    outs = []
    for b in range(bsz):
        xb = x[b]
        for l in range(depth):
            xb = _layer(xb, w_in[l], mix_pre_norm[l], mix_post_norm[l], hg_lb_table, hg_out_norm[l],
                        ssm_conv_w[l], ssm_conv_b[l], ssm_dt_bias[l], ssm_A_log[l], ssm_D[l],
                        ssm_out_norm[l], w_branch_hg[l], w_branch_ssm[l], w_out[l],
                        ffn_pre_norm[l], ffn_post_norm[l], ffn_w_up[l], ffn_conv_w[l], ffn_conv_b[l],
                        ffn_w_down[l])
        outs.append(xb)
    return jnp.stack(outs, axis=0)
```

```python
import functools

import numpy as np
import jax
import jax.numpy as jnp
from jax import lax
from jax.experimental import pallas as pl
from jax.experimental.pallas import tpu as pltpu

F32 = jnp.float32
BF16 = jnp.bfloat16
EPS = 1e-6

HG_DK = 128
HG_DV = 128
SSM_HEADDIM = 64
SSM_GROUPS = 8
SSM_DSTATE = 128
SSM_CONV = 4
FFN_CONV = 3

NORM_ROWS = 512
INPROJ_TILE = 1024
INPROJ_ROWS = 1024
HG_CHUNK = 128
HG_BLOCK = 1024
HG_HEADS_PER_STEP = 8
HG_VPU_LEVEL_MIN = 8
SSD_CHUNK = 256
SSD_GROUPS_PER_STEP = 8
MERGE_ROWS, MERGE_COLS = 1024, 512
OUT_ROWS = 512
FFN_UP_ROWS, FFN_UP_COLS = 1024, 512
FFN_DOWN_ROWS = 512
FFN_DOWN_SLAB = 512
CARRY = 8
CONV_ROWS = 256

VMEM_LIMIT = 48 * 1024 * 1024
VMEM_LIMIT_LARGE = 56 * 1024 * 1024
LOG2E = 1.4426950408889634


def _params(sem, vmem=VMEM_LIMIT):
    return pltpu.CompilerParams(dimension_semantics=sem, vmem_limit_bytes=vmem)


def _dot(a, b):
    return jnp.dot(a, b, preferred_element_type=F32)


def _dot_nt(a, b):
    return lax.dot_general(a, b, (((1,), (1,)), ((), ())), preferred_element_type=F32)


def _dot_tn(a, b):
    return lax.dot_general(a, b, (((0,), (0,)), ((), ())), preferred_element_type=F32)


def _split_bf16(a):
    hi = a.astype(BF16)
    lo = (a - hi.astype(F32)).astype(BF16)
    return hi, lo


def _sigmoid(a):
    return 0.5 * jnp.tanh(0.5 * a) + 0.5


def _silu(a):
    u = 0.5 * a
    return u * (jnp.tanh(u) + 1.0)


def _gelu_tanh(a):
    c0 = float(np.sqrt(2.0 / np.pi))
    return a * (0.5 * jnp.tanh(a * (c0 + (c0 * 0.044715) * (a * a))) + 0.5)


def _rms(x, w):
    return x * lax.rsqrt(jnp.mean(x * x, axis=-1, keepdims=True) + EPS) * w


def _norm_dt_kernel(x_ref, nw_ref, wdt_ref, h_ref, dt_ref):
    h = _rms(x_ref[...], nw_ref[...]).astype(BF16)
    h_ref[...] = h
    dt_ref[...] = _dot_nt(wdt_ref[...].astype(BF16), h)


def _norm_dt(x, nw, wdt_t, tm):
    t, d = x.shape
    nh = wdt_t.shape[0]
    return pl.pallas_call(
        _norm_dt_kernel,
        grid=(t // tm,),
        in_specs=[
            pl.BlockSpec((tm, d), lambda m: (m, 0)),
            pl.BlockSpec((1, d), lambda m: (0, 0)),
            pl.BlockSpec((nh, d), lambda m: (0, 0)),
        ],
        out_specs=[pl.BlockSpec((tm, d), lambda m: (m, 0)),
                   pl.BlockSpec((nh, tm), lambda m: (0, m))],
        out_shape=[jax.ShapeDtypeStruct((t, d), BF16), jax.ShapeDtypeStruct((nh, t), F32)],
        compiler_params=_params(("parallel",)),
        name="norm_dt",
    )(x, nw, wdt_t)


def _proj_nt_kernel(h_ref, w_ref, o_ref, wb_ref):
    @pl.when(pl.program_id(1) == 0)
    def _():
        wb_ref[...] = w_ref[...].astype(BF16)

    o_ref[...] = _dot_nt(h_ref[...], wb_ref[...]).astype(o_ref.dtype)


def _proj_nt(h, w_t, row_off, n, tm, tn, name):
    t, d = h.shape
    if row_off % tn == 0:
        w_spec = pl.BlockSpec((tn, d), lambda j, m: (row_off // tn + j, 0))
    else:
        assert row_off % 8 == 0 and tn % 8 == 0
        w_spec = pl.BlockSpec((pl.Element(tn), pl.Element(d)),
                              lambda j, m: (pl.multiple_of(row_off + j * tn, 8), 0))
    return pl.pallas_call(
        _proj_nt_kernel,
        grid=(n // tn, t // tm),
        in_specs=[pl.BlockSpec((tm, d), lambda j, m: (m, 0)), w_spec],
        out_specs=pl.BlockSpec((tm, tn), lambda j, m: (m, j)),
        out_shape=jax.ShapeDtypeStruct((t, n), BF16),
        scratch_shapes=[pltpu.VMEM((tn, d), BF16)],
        compiler_params=_params(("parallel", "arbitrary")),
        name=name,
    )(h, w_t)


def _conv_rows(ext_ref, cur, r0, w, b, ksize):
    rows = cur.shape[0]
    ext_ref[CARRY + r0:CARRY + r0 + rows, :] = cur
    acc = cur * w[ksize - 1:ksize, :] + b
    for j in range(1, ksize):
        acc = acc + ext_ref[CARRY + r0 - j:CARRY + r0 - j + rows, :] * w[ksize - 1 - j:ksize - j, :]
    return acc


def _conv_keep_tail(ext_ref, q):
    ext_ref[0:CARRY, :] = ext_ref[q:q + CARRY, :]


_STAGE_EXTRAS = {"qscale": 0, "silu": 0, "logf": 1, "conv": 2}


def _seg_proj_kernel(h_ref, w_ref, *rest, kind):
    nx = _STAGE_EXTRAS[kind]
    extra = rest[:nx]
    o_ref, wb_ref = rest[nx:nx + 2]
    first = pl.program_id(1) == 0

    @pl.when(first)
    def _():
        wb_ref[...] = w_ref[...].astype(BF16)

    if kind == "conv":
        ext_ref = rest[nx + 2]

        @pl.when(first)
        def _():
            ext_ref[0:CARRY, :] = jnp.zeros((CARRY, ext_ref.shape[1]), F32)

    if kind == "logf":
        tab = extra[0][...]
        te = jnp.exp(tab - jnp.max(tab, axis=0, keepdims=True))
        lb = te[0:1, :] / jnp.sum(te, axis=0, keepdims=True)
        f_amp = 0.5 * (1.0 - lb)
        f_mid = lb + f_amp
    elif kind == "conv":
        w = 0.5 * extra[0][...]
        bias = 0.5 * extra[1][...]
    wb = wb_ref[...]
    step = CONV_ROWS if kind == "conv" else o_ref.shape[0]
    for r0 in range(0, o_ref.shape[0], step):
        acc = _dot_nt(h_ref[r0:r0 + step, :], wb)
        if kind == "qscale":
            out = _silu(acc) * HG_DK ** -0.5
        elif kind == "silu":
            out = _silu(acc)
        elif kind == "logf":
            out = jnp.log(f_mid + f_amp * jnp.tanh(0.5 * acc)) * LOG2E
        else:
            u = _conv_rows(ext_ref, acc, r0, w, bias, SSM_CONV)
            out = u * (jnp.tanh(u) + 1.0)
        o_ref[r0:r0 + step, :] = out.astype(o_ref.dtype)
    if kind == "conv":
        _conv_keep_tail(ext_ref, o_ref.shape[0])


def _seg_proj(h, w_t, row_off, n, kind, extras, tm, tn, name):
    t, d = h.shape
    assert row_off % tn == 0 and len(extras) == _STAGE_EXTRAS[kind]
    scratch = [pltpu.VMEM((tn, d), BF16)]
    if kind == "conv":
        scratch.append(pltpu.VMEM((CARRY + tm, tn), F32))
    return pl.pallas_call(
        functools.partial(_seg_proj_kernel, kind=kind),
        grid=(n // tn, t // tm),
        in_specs=[
            pl.BlockSpec((tm, d), lambda j, m: (m, 0)),
            pl.BlockSpec((tn, d), lambda j, m: (row_off // tn + j, 0)),
        ] + [pl.BlockSpec((e.shape[0], tn), lambda j, m: (0, j)) for e in extras],
        out_specs=pl.BlockSpec((tm, tn), lambda j, m: (m, j)),
        out_shape=jax.ShapeDtypeStruct((t, n), BF16),
        scratch_shapes=scratch,
        compiler_params=_params(("parallel", "arbitrary")),
        name=name,
    )(h, w_t, *extras)


def _hg_constants(c):
    levels = []
    h = c // 2
    while h >= 1:
        levels.append(h)
        h //= 2
    t = np.arange(c)[:, None]
    u = np.arange(c)[None, :]
    mats = [(u <= t)]
    masks = []
    for h in levels:
        blk = 2 * h
        mid = (t // blk) * blk + h
        second = (t % blk) >= h
        if h < HG_VPU_LEVEL_MIN:
            mats.append(np.where(second, (u >= mid) & (u <= t), (u > t) & (u < mid)))
        s = u
        masks.append(((t // blk) == (s // blk)) & second & ((s % blk) < h))
    masks.append(t == u)
    pm = np.concatenate(mats, axis=0).astype(np.float32)
    mk = np.stack(masks, axis=0).astype(np.float32)
    return levels, pm, mk


def _level_exponents(b, h):
    c = b.shape[0]
    parts = []
    for start in range(0, c, 2 * h):
        mid = start + h
        ref = b[mid - 1:mid, :]
        parts.append(ref - b[start:mid, :])
        parts.append(b[mid:mid + h, :] - ref)
    return jnp.concatenate(parts, axis=0)


def _blockdiag(a, b):
    top = jnp.concatenate([a, jnp.zeros((a.shape[0], b.shape[1]), a.dtype)], axis=1)
    bot = jnp.concatenate([jnp.zeros((b.shape[0], a.shape[1]), b.dtype), b], axis=1)
    return jnp.concatenate([top, bot], axis=0)


def _blockdiag_t(kpair):
    kt = kpair.T
    half = kt.shape[0] // 2
    return _blockdiag(kt[0:half], kt[half:])


def _hgrn2_kernel(q_ref, lf_ref, i_ref, g_ref, nw_ref, pm_ref, mk_ref, *rest, chunk, levels, ncast):
    c = chunk
    w_in_refs = rest[:ncast]
    o_ref = rest[ncast]
    w_out_refs = rest[ncast + 1:2 * ncast + 1]
    st_ref, ex_ref = rest[2 * ncast + 1:]
    for src, dst in zip(w_in_refs, w_out_refs):
        dst[...] = src[...].astype(BF16)

    @pl.when(pl.program_id(1) == 0)
    def _():
        st_ref[...] = jnp.zeros_like(st_ref)

    nw = nw_ref[...]
    pm = pm_ref[...]
    nh = st_ref.shape[0]
    dk = HG_DK

    def body(ci, carry):
        r0 = pl.multiple_of(ci * c, c)
        rows = pl.ds(r0, c)
        for p in range(nh // 2):
            ha, hb = 2 * p, 2 * p + 1
            sa = slice(ha * dk, (ha + 1) * dk)
            sb = slice(hb * dk, (hb + 1) * dk)
            ps = slice(ha * dk, (hb + 1) * dk)
            lo, hi = slice(0, dk), slice(dk, 2 * dk)
            qp = q_ref[rows, ps]
            lfp = lf_ref[rows, ps]
            kp = (1.0 - jnp.exp2(lfp.astype(F32))).astype(BF16)
            ex_ref[:, ps] = _dot(pm, lfp)
            b = ex_ref[0:c, ps]
            b_last = b[c - 1:c, :]
            st_a = st_ref[ha]
            st_b = st_ref[hb]
            o = _dot_nt(qp * jnp.exp2(b).astype(BF16),
                        _blockdiag(st_a.astype(BF16), st_b.astype(BF16)))
            sc = mk_ref[len(levels)] * _dot(qp, _blockdiag_t(kp)).astype(BF16)
            n_vpu = sum(h >= HG_VPU_LEVEL_MIN for h in levels)
            for l, h in enumerate(levels):
                if h >= HG_VPU_LEVEL_MIN:
                    ex = _level_exponents(b, h)
                else:
                    ex = ex_ref[(l - n_vpu + 1) * c:(l - n_vpu + 2) * c, ps]
                e = jnp.exp2(ex).astype(BF16)
                ke = kp * e
                s = _dot(qp * e, _blockdiag_t(ke))
                sc = sc + mk_ref[l] * s.astype(BF16)
            o = o + _dot(sc, _blockdiag(i_ref[rows, sa], i_ref[rows, sb]))
            kdec = kp * jnp.exp2(b_last - b).astype(BF16)
            sdec = jnp.exp2(b_last)
            st_ref[ha] = st_a * sdec[:, lo] + _dot_tn(i_ref[rows, sa], kdec[:, lo])
            st_ref[hb] = st_b * sdec[:, hi] + _dot_tn(i_ref[rows, sb], kdec[:, hi])
            o_ref[rows, sa] = (_rms(o[:, lo], nw) * g_ref[rows, sa].astype(F32)).astype(o_ref.dtype)
            o_ref[rows, sb] = (_rms(o[:, hi], nw) * g_ref[rows, sb].astype(F32)).astype(o_ref.dtype)
        return carry

    lax.fori_loop(0, q_ref.shape[0] // c, body, 0)


def _hgrn2(qh, lf, v, gate, norm_w, n_heads, tb, chunk, hps, cast_weights):
    t = qh.shape[0]
    gate_off = 0
    n_steps = (n_heads // hps) * (t // tb)
    for wgt in cast_weights:
        assert wgt.shape[0] % (16 * n_steps) == 0, "weight rows must split into bf16-tile-aligned slices"
    w_rows = [wgt.shape[0] // n_steps for wgt in cast_weights]
    w_specs = [pl.BlockSpec((r, wgt.shape[1]), lambda h, i: (h * (t // tb) + i, 0))
               for r, wgt in zip(w_rows, cast_weights)]
    levels, pm, mk = _hg_constants(chunk)
    pm = jnp.asarray(pm, BF16)
    mk = jnp.asarray(np.concatenate([mk, mk], axis=2), BF16)
    w = hps * HG_DK
    nblk = n_heads // hps
    col = lambda off: (lambda h, i: (i, off // w + h))
    const2 = lambda h, i: (0, 0)
    return pl.pallas_call(
        functools.partial(_hgrn2_kernel, chunk=chunk, levels=tuple(levels), ncast=len(cast_weights)),
        grid=(nblk, t // tb),
        in_specs=[
            pl.BlockSpec((tb, w), col(0)),
            pl.BlockSpec((tb, w), col(0)),
            pl.BlockSpec((tb, w), col(0)),
            pl.BlockSpec((tb, w), col(gate_off)),
            pl.BlockSpec((1, HG_DV), const2),
            pl.BlockSpec(pm.shape, const2),
            pl.BlockSpec(mk.shape, lambda h, i: (0, 0, 0)),
        ] + w_specs,
        out_specs=[pl.BlockSpec((tb, w), lambda h, i: (i, h))] + w_specs,
        out_shape=[jax.ShapeDtypeStruct((t, n_heads * HG_DV), BF16)]
        + [jax.ShapeDtypeStruct(wgt.shape, BF16) for wgt in cast_weights],
        scratch_shapes=[pltpu.VMEM((hps, HG_DV, HG_DK), F32),
                        pltpu.VMEM((pm.shape[0], w), F32)],
        compiler_params=_params(("parallel", "arbitrary")),
        name="hgrn2_scan",
    )(qh, lf, v, gate, norm_w, pm, mk, *cast_weights)


def _ssd_expand_mat(hpg):
    ea = np.zeros((128, hpg * SSM_HEADDIM), np.float32)
    for h in range(hpg):
        ea[h, h * SSM_HEADDIM:(h + 1) * SSM_HEADDIM] = 1.0
    return ea


def _ssd_kernel(z_ref, x_ref, b_ref, c_ref, dt_ref, dtb_ref, alog_ref, dskip_ref, nw_ref,
                triu_ref, ea_ref, o_ref, s_ref):
    @pl.when(pl.program_id(1) == 0)
    def _():
        s_ref[...] = jnp.zeros_like(s_ref)

    gw = s_ref.shape[2]
    n = s_ref.shape[1]
    for gi in range(s_ref.shape[0]):
        _ssd_group(z_ref.at[:, gi * gw:(gi + 1) * gw], x_ref.at[:, gi * gw:(gi + 1) * gw],
                   b_ref.at[:, gi * n:(gi + 1) * n], c_ref.at[:, gi * n:(gi + 1) * n],
                   dt_ref.at[gi], dtb_ref.at[gi], alog_ref.at[gi], dskip_ref.at[gi], nw_ref.at[gi],
                   triu_ref, ea_ref, o_ref.at[:, gi * gw:(gi + 1) * gw], s_ref.at[gi])


def _ssd_group(z_ref, x_ref, b_ref, c_ref, dt_ref, dtb_ref, alog_ref, dskip_ref, nw_ref,
               triu_ref, ea_ref, o_ref, s_ref):
    q = x_ref.shape[0]
    hpg = dt_ref.shape[0]
    xs = x_ref[...].astype(F32)
    bm16 = b_ref[...]
    cm16 = c_ref[...]

    dt_r = jax.nn.softplus(dt_ref[...] + dtb_ref[...])
    da_r = dt_r * (-jnp.exp(alog_ref[...]) * LOG2E)
    da_hi, da_lo = _split_bf16(da_r)
    triu = triu_ref[...]
    acum_r = _dot(da_hi, triu) + _dot(da_lo, triu)
    pad = jnp.zeros((128 - 2 * hpg, q), F32)
    cols = jnp.concatenate([acum_r, dt_r, pad], axis=0).T

    lane = lax.broadcasted_iota(jnp.int32, (q, 128), 1)
    acol = jnp.where(lane < hpg, cols, 0.0)
    a_last = acol[q - 1:q, :]
    dtwd = jnp.exp2(a_last - acol) * pltpu.roll(cols, 128 - hpg, axis=1)
    fx = _dot(jnp.concatenate([jnp.exp2(acol).astype(BF16), dtwd.astype(BF16)], axis=0), ea_ref[...])
    ea_x = fx[0:q]
    dtwd_x = fx[q:2 * q]
    sd_x = _dot(jnp.broadcast_to(jnp.exp2(a_last), (8, 128)).astype(BF16), ea_ref[...])[0:1, :]

    xs16 = x_ref[...]
    bmt = bm16.T
    cb16 = _dot(cm16, bmt).astype(BF16)
    cs = _dot(cm16, s_ref[...].astype(BF16))
    row = lax.broadcasted_iota(jnp.int32, (q, q), 0)
    colid = lax.broadcasted_iota(jnp.int32, (q, q), 1)
    causal = row >= colid
    lo_half = lane < SSM_HEADDIM

    y_parts = []
    for p in range(hpg // 2):
        ms = []
        for h in (2 * p, 2 * p + 1):
            dm = cols[:, h:h + 1] - acum_r[h:h + 1, :]
            lm16 = jnp.exp2(jnp.where(causal, dm, -1e30)).astype(BF16)
            ms.append(cb16 * lm16 * dt_r[h:h + 1, :].astype(BF16))
        yy = _dot(jnp.concatenate(ms, axis=0), xs16[:, p * 128:(p + 1) * 128])
        y_parts.append(jnp.where(lo_half, yy[0:q], yy[q:2 * q]))

    y = jnp.concatenate(y_parts, axis=1) + cs * ea_x
    s_ref[...] = s_ref[...] * sd_x + _dot(bmt, (xs * dtwd_x).astype(BF16))

    y = y + dskip_ref[...] * xs
    y = y * z_ref[...].astype(F32)
    o_ref[...] = _rms(y, nw_ref[...]).astype(o_ref.dtype)


def _ssd(zs, xbc, dt_hm, dt_bias, a_log, d_skip, norm_w, d_inner):
    t = xbc.shape[0]
    z_off = xbc_off = 0
    q = SSD_CHUNK
    g = SSM_GROUPS
    n = SSM_DSTATE
    gw = d_inner // g
    hpg = gw // SSM_HEADDIM
    triu = jnp.asarray(np.triu(np.ones((q, q), np.float32)), BF16)
    ea = jnp.asarray(_ssd_expand_mat(hpg), BF16)
    c2 = lambda gi, i: (0, 0)
    gs = SSD_GROUPS_PER_STEP
    assert g % gs == 0 and d_inner % (gs * n) == 0, "B / C column blocks must be block-aligned"
    return pl.pallas_call(
        _ssd_kernel,
        grid=(g // gs, t // q),
        in_specs=[
            pl.BlockSpec((q, gs * gw), lambda gi, i: (i, z_off // (gs * gw) + gi)),
            pl.BlockSpec((q, gs * gw), lambda gi, i: (i, xbc_off // (gs * gw) + gi)),
            pl.BlockSpec((q, gs * n), lambda gi, i: (i, (xbc_off + d_inner) // (gs * n) + gi)),
            pl.BlockSpec((q, gs * n), lambda gi, i: (i, (xbc_off + d_inner + g * n) // (gs * n) + gi)),
            pl.BlockSpec((gs, hpg, q), lambda gi, i: (gi, 0, i)),
            pl.BlockSpec((gs, hpg, 1), lambda gi, i: (gi, 0, 0)),
            pl.BlockSpec((gs, hpg, 1), lambda gi, i: (gi, 0, 0)),
            pl.BlockSpec((gs, 1, gw), lambda gi, i: (gi, 0, 0)),
            pl.BlockSpec((gs, 1, gw), lambda gi, i: (gi, 0, 0)),
            pl.BlockSpec((q, q), c2),
            pl.BlockSpec(ea.shape, c2),
        ],
        out_specs=pl.BlockSpec((q, gs * gw), lambda gi, i: (i, gi)),
        out_shape=jax.ShapeDtypeStruct((t, d_inner), BF16),
        scratch_shapes=[pltpu.VMEM((gs, n, gw), F32)],
        compiler_params=_params(("parallel", "arbitrary")),
        name="ssd_scan",
    )(zs, xbc, xbc, xbc, dt_hm, dt_bias, a_log, d_skip, norm_w, triu, ea)


def _merge_kernel(yh_ref, ys_ref, wh_ref, ws_ref, gh_ref, gs_ref, o_ref):
    a = _dot(yh_ref[...], wh_ref[...])
    b = _dot(ys_ref[...], ws_ref[...])
    gh = _sigmoid(gh_ref[...].astype(F32))
    gs = _sigmoid(gs_ref[...].astype(F32))
    o_ref[...] = (gh * a + gs * b).astype(o_ref.dtype)


def _merge(y_hg, y_ssm, w_hg, w_ssm, gates, tm, tn):
    t = y_hg.shape[0]
    d = w_hg.shape[1]
    return pl.pallas_call(
        _merge_kernel,
        grid=(t // tm, d // tn),
        in_specs=[
            pl.BlockSpec((tm, y_hg.shape[1]), lambda m, j: (m, 0)),
            pl.BlockSpec((tm, y_ssm.shape[1]), lambda m, j: (m, 0)),
            pl.BlockSpec((w_hg.shape[0], tn), lambda m, j: (0, j)),
            pl.BlockSpec((w_ssm.shape[0], tn), lambda m, j: (0, j)),
            pl.BlockSpec((tm, tn), lambda m, j: (m, j)),
            pl.BlockSpec((tm, tn), lambda m, j: (m, d // tn + j)),
        ],
        out_specs=pl.BlockSpec((tm, tn), lambda m, j: (m, j)),
        out_shape=jax.ShapeDtypeStruct((t, d), BF16),
        compiler_params=_params(("parallel", "arbitrary")),
        name="branch_merge",
    )(y_hg, y_ssm, w_hg, w_ssm, gates, gates)


def _out_kernel(a_ref, w_ref, x_ref, post_ref, pre_ref, x1_ref, h_ref):
    w = w_ref[...]
    for r0 in range(0, x_ref.shape[0], CONV_ROWS):
        rows = slice(r0, r0 + CONV_ROWS)
        x1 = x_ref[rows, :] + _rms(_dot(a_ref[rows, :], w), post_ref[...])
        x1_ref[rows, :] = x1
        h_ref[rows, :] = _rms(x1, pre_ref[...]).astype(BF16)


def _out_proj(mixed, w_out, x, post_w, pre_w, tm):
    t, d = x.shape
    c2 = lambda m: (0, 0)
    row = lambda m: (m, 0)
    return pl.pallas_call(
        _out_kernel,
        grid=(t // tm,),
        in_specs=[
            pl.BlockSpec((tm, d), row),
            pl.BlockSpec((d, d), c2),
            pl.BlockSpec((tm, d), row),
            pl.BlockSpec((1, d), c2),
            pl.BlockSpec((1, d), c2),
        ],
        out_specs=[pl.BlockSpec((tm, d), row), pl.BlockSpec((tm, d), row)],
        out_shape=[jax.ShapeDtypeStruct((t, d), F32), jax.ShapeDtypeStruct((t, d), BF16)],
        compiler_params=_params(("parallel",)),
        name="out_proj",
    )(mixed, w_out, x, post_w, pre_w)


def _ffn_up_kernel(h_ref, wg_ref, wu_ref, cw_ref, cb_ref, o_ref, ext_ref, wgb_ref, wub_ref):
    @pl.when(pl.program_id(1) == 0)
    def _():
        wgb_ref[...] = wg_ref[...].astype(BF16)
        wub_ref[...] = wu_ref[...].astype(BF16)
        ext_ref[0:CARRY, :] = jnp.zeros((CARRY, ext_ref.shape[1]), F32)

    h = h_ref[...]
    gate = _dot(h, wgb_ref[...])
    up = _dot(h, wub_ref[...])
    conv = _conv_rows(ext_ref, gate, 0, cw_ref[...], cb_ref[...], FFN_CONV)
    _conv_keep_tail(ext_ref, o_ref.shape[0])
    o_ref[...] = (_gelu_tanh(conv) * up).astype(o_ref.dtype)


def _ffn_up(h, w_up, conv_w, conv_b, d_ff, tm, tn):
    t, d = h.shape
    k = conv_w.shape[0]
    return pl.pallas_call(
        _ffn_up_kernel,
        grid=(d_ff // tn, t // tm),
        in_specs=[
            pl.BlockSpec((tm, d), lambda j, m: (m, 0)),
            pl.BlockSpec((d, tn), lambda j, m: (0, j)),
            pl.BlockSpec((d, tn), lambda j, m: (0, d_ff // tn + j)),
            pl.BlockSpec((k, tn), lambda j, m: (0, j)),
            pl.BlockSpec((1, tn), lambda j, m: (0, j)),
        ],
        out_specs=pl.BlockSpec((tm, tn), lambda j, m: (m, j)),
        out_shape=jax.ShapeDtypeStruct((t, d_ff), BF16),
        scratch_shapes=[pltpu.VMEM((CARRY + tm, tn), F32),
                        pltpu.VMEM((d, tn), BF16), pltpu.VMEM((d, tn), BF16)],
        compiler_params=_params(("parallel", "arbitrary")),
        name="ffn_up",
    )(h, w_up, w_up, conv_w, conv_b)


def _ffn_down_kernel(a_ref, w_ref, x_ref, post_ref, o_ref):
    d = o_ref.shape[1]
    slabs = [slice(c, c + FFN_DOWN_SLAB) for c in range(0, d, FFN_DOWN_SLAB)]
    a = a_ref[...]
    for cs in slabs:
        o_ref[:, cs] = _dot(a, w_ref[:, cs])
    ssq = sum(jnp.sum(o_ref[:, cs] * o_ref[:, cs], axis=-1, keepdims=True) for cs in slabs)
    inv = lax.rsqrt(ssq / d + EPS)
    for cs in slabs:
        o_ref[:, cs] = x_ref[:, cs] + o_ref[:, cs] * inv * post_ref[:, cs]


def _ffn_down(act, w_down, x1, post_w, tm):
    t, d = x1.shape
    d_ff = act.shape[1]
    return pl.pallas_call(
        _ffn_down_kernel,
        grid=(t // tm,),
        in_specs=[
            pl.BlockSpec((tm, d_ff), lambda m: (m, 0)),
            pl.BlockSpec((d_ff, d), lambda m: (0, 0), pipeline_mode=pl.Buffered(1)),
            pl.BlockSpec((tm, d), lambda m: (m, 0)),
            pl.BlockSpec((1, d), lambda m: (0, 0)),
        ],
        out_specs=pl.BlockSpec((tm, d), lambda m: (m, 0)),
        out_shape=jax.ShapeDtypeStruct((t, d), F32),
        compiler_params=_params(("parallel",), VMEM_LIMIT_LARGE),
        name="ffn_down",
    )(act, w_down, x1, post_w)


def _layer(x, w_in, mix_pre, mix_post, lb_table, hg_norm, conv_w, conv_b, dt_bias, a_log, d_skip,
           ssm_norm, w_hg, w_ssm, w_out, ffn_pre, ffn_post, w_up, ffn_cw, ffn_cb, w_down):
    t, d = x.shape
    hg_v = w_hg.shape[0]
    hg_heads = hg_v // HG_DV
    d_inner = w_ssm.shape[0]
    ssm_heads = a_log.shape[0]
    hpg = ssm_heads // SSM_GROUPS
    gw = d_inner // SSM_GROUPS
    d_ff = w_down.shape[0]
    conv_dim = conv_w.shape[1]

    main_n = 4 * hg_v + d_inner + conv_dim
    dt_off = main_n
    gate_off = dt_off + ssm_heads
    z_off = 4 * hg_v
    xbc_off = z_off + d_inner

    w_t = w_in.T
    h, dt_t = _norm_dt(x, mix_pre.reshape(1, d), w_t[dt_off:gate_off], tm=NORM_ROWS)
    tm, tn = INPROJ_ROWS, INPROJ_TILE
    qh = _seg_proj(h, w_t, 0, hg_v, "qscale", (), tm, tn, "inproj_q")
    lf = _seg_proj(h, w_t, hg_v, hg_v, "logf", (lb_table,), tm, tn, "inproj_f")
    v = _proj_nt(h, w_t, 2 * hg_v, hg_v, tm, tn, "inproj_i")
    go = _seg_proj(h, w_t, 3 * hg_v, hg_v, "silu", (), tm, tn, "inproj_g")
    zs = _seg_proj(h, w_t, z_off, d_inner, "silu", (), tm, tn, "inproj_z")
    xbc = _seg_proj(h, w_t, xbc_off, conv_dim, "conv", (conv_w, conv_b.reshape(1, conv_dim)), tm, tn,
                    "inproj_xbc")
    gates = _proj_nt(h, w_t, gate_off, 2 * d, tm, tn, "inproj_gates")

    y_hg, w_hg16, w_ssm16, w_out16, w_down16 = _hgrn2(
        qh, lf, v, go, hg_norm.reshape(1, HG_DV), hg_heads, HG_BLOCK, HG_CHUNK, HG_HEADS_PER_STEP,
        cast_weights=(w_hg, w_ssm, w_out, w_down))

    dt_hm = dt_t.reshape(SSM_GROUPS, hpg, t)
    y_ssm = _ssd(
        zs, xbc, dt_hm,
        dt_bias.reshape(SSM_GROUPS, hpg, 1), a_log.reshape(SSM_GROUPS, hpg, 1),
        jnp.repeat(d_skip, SSM_HEADDIM).reshape(SSM_GROUPS, 1, gw),
        ssm_norm.reshape(SSM_GROUPS, 1, gw), d_inner)

    mixed = _merge(y_hg, y_ssm, w_hg16, w_ssm16, gates, tm=MERGE_ROWS, tn=MERGE_COLS)
    x1, h2 = _out_proj(mixed, w_out16, x, mix_post.reshape(1, d), ffn_pre.reshape(1, d), tm=OUT_ROWS)
    act = _ffn_up(h2, w_up, ffn_cw, ffn_cb.reshape(1, d_ff), d_ff, tm=FFN_UP_ROWS, tn=FFN_UP_COLS)
    return _ffn_down(act, w_down16, x1, ffn_post.reshape(1, d), tm=FFN_DOWN_ROWS)


def kernel(x, w_in, mix_pre_norm, mix_post_norm, hg_lb_table, hg_out_norm, ssm_conv_w, ssm_conv_b,
           ssm_dt_bias, ssm_A_log, ssm_D, ssm_out_norm, w_branch_hg, w_branch_ssm, w_out,
           ffn_pre_norm, ffn_post_norm, ffn_w_up, ffn_conv_w, ffn_conv_b, ffn_w_down):
    bsz, t, d = x.shape
    depth = w_in.shape[0]
    assert depth == 1 and hg_lb_table.shape[0] == 2, "forget-gate lower bound is computed for one layer"
    outs = []
    for b in range(bsz):
        xb = x[b]
        for l in range(depth):
            xb = _layer(xb, w_in[l], mix_pre_norm[l], mix_post_norm[l], hg_lb_table, hg_out_norm[l],
                        ssm_conv_w[l], ssm_conv_b[l], ssm_dt_bias[l], ssm_A_log[l], ssm_D[l],
                        ssm_out_norm[l], w_branch_hg[l], w_branch_ssm[l], w_out[l],
                        ffn_pre_norm[l], ffn_post_norm[l], ffn_w_up[l], ffn_conv_w[l], ffn_conv_b[l],
                        ffn_w_down[l])
        outs.append(xb)
    return jnp.stack(outs, axis=0)
```

```python
import functools

import numpy as np
import jax
import jax.numpy as jnp
from jax import lax
from jax.experimental import pallas as pl
from jax.experimental.pallas import tpu as pltpu

F32 = jnp.float32
BF16 = jnp.bfloat16
EPS = 1e-6

HG_DK = 128
HG_DV = 128
SSM_HEADDIM = 64
SSM_GROUPS = 8
SSM_DSTATE = 128
SSM_CONV = 4
FFN_CONV = 3

NORM_ROWS = 512
INPROJ_TILE = 1024
INPROJ_ROWS = 1024
HG_CHUNK = 128
HG_BLOCK = 1024
HG_HEADS_PER_STEP = 8
HG_VPU_LEVEL_MIN = 8
SSD_CHUNK = 256
SSD_GROUPS_PER_STEP = 8
MERGE_ROWS, MERGE_COLS = 1024, 512
OUT_ROWS = 512
FFN_UP_ROWS, FFN_UP_COLS = 1024, 512
FFN_DOWN_ROWS = 512
FFN_DOWN_SLAB = 512
CARRY = 8
CONV_ROWS = 256

VMEM_LIMIT = 48 * 1024 * 1024
VMEM_LIMIT_LARGE = 56 * 1024 * 1024
LOG2E = 1.4426950408889634


def _params(sem, vmem=VMEM_LIMIT):
    return pltpu.CompilerParams(dimension_semantics=sem, vmem_limit_bytes=vmem)


def _dot(a, b):
    return jnp.dot(a, b, preferred_element_type=F32)


def _dot_nt(a, b):
    return lax.dot_general(a, b, (((1,), (1,)), ((), ())), preferred_element_type=F32)


def _dot_tn(a, b):
    return lax.dot_general(a, b, (((0,), (0,)), ((), ())), preferred_element_type=F32)


def _split_bf16(a):
    hi = a.astype(BF16)
    lo = (a - hi.astype(F32)).astype(BF16)
    return hi, lo


def _sigmoid(a):
    return 0.5 * jnp.tanh(0.5 * a) + 0.5


def _silu(a):
    u = 0.5 * a
    return u * (jnp.tanh(u) + 1.0)


def _gelu_tanh(a):
    c0 = float(np.sqrt(2.0 / np.pi))
    return a * (0.5 * jnp.tanh(a * (c0 + (c0 * 0.044715) * (a * a))) + 0.5)


def _rms(x, w):
    return x * lax.rsqrt(jnp.mean(x * x, axis=-1, keepdims=True) + EPS) * w


def _norm_dt_kernel(x_ref, nw_ref, wdt_ref, h_ref, dt_ref):
    h = _rms(x_ref[...], nw_ref[...]).astype(BF16)
    h_ref[...] = h
    dt_ref[...] = _dot_nt(wdt_ref[...].astype(BF16), h)


def _norm_dt(x, nw, wdt_t, tm):
    t, d = x.shape
    nh = wdt_t.shape[0]
    return pl.pallas_call(
        _norm_dt_kernel,
        grid=(t // tm,),
        in_specs=[
            pl.BlockSpec((tm, d), lambda m: (m, 0)),
            pl.BlockSpec((1, d), lambda m: (0, 0)),
            pl.BlockSpec((nh, d), lambda m: (0, 0)),
        ],
        out_specs=[pl.BlockSpec((tm, d), lambda m: (m, 0)),
                   pl.BlockSpec((nh, tm), lambda m: (0, m))],
        out_shape=[jax.ShapeDtypeStruct((t, d), BF16), jax.ShapeDtypeStruct((nh, t), F32)],
        compiler_params=_params(("parallel",)),
        name="norm_dt",
    )(x, nw, wdt_t)


def _proj_nt_kernel(h_ref, w_ref, o_ref, wb_ref):
    @pl.when(pl.program_id(1) == 0)
    def _():
        wb_ref[...] = w_ref[...].astype(BF16).T

    o_ref[...] = _dot(h_ref[...], wb_ref[...]).astype(o_ref.dtype)


def _proj_nt(h, w_t, row_off, n, tm, tn, name):
    t, d = h.shape
    if row_off % tn == 0:
        w_spec = pl.BlockSpec((tn, d), lambda j, m: (row_off // tn + j, 0))
    else:
        assert row_off % 8 == 0 and tn % 8 == 0
        w_spec = pl.BlockSpec((pl.Element(tn), pl.Element(d)),
                              lambda j, m: (pl.multiple_of(row_off + j * tn, 8), 0))
    return pl.pallas_call(
        _proj_nt_kernel,
        grid=(n // tn, t // tm),
        in_specs=[pl.BlockSpec((tm, d), lambda j, m: (m, 0)), w_spec],
        out_specs=pl.BlockSpec((tm, tn), lambda j, m: (m, j)),
        out_shape=jax.ShapeDtypeStruct((t, n), BF16),
        scratch_shapes=[pltpu.VMEM((d, tn), BF16)],
        compiler_params=_params(("parallel", "arbitrary")),
        name=name,
    )(h, w_t)


def _conv_rows(ext_ref, cur, r0, w, b, ksize):
    rows = cur.shape[0]
    ext_ref[CARRY + r0:CARRY + r0 + rows, :] = cur
    acc = cur * w[ksize - 1:ksize, :] + b
    for j in range(1, ksize):
        acc = acc + ext_ref[CARRY + r0 - j:CARRY + r0 - j + rows, :] * w[ksize - 1 - j:ksize - j, :]
    return acc


def _conv_keep_tail(ext_ref, q):
    ext_ref[0:CARRY, :] = ext_ref[q:q + CARRY, :]


_STAGE_EXTRAS = {"qscale": 0, "silu": 0, "logf": 1, "conv": 2}


def _seg_proj_kernel(h_ref, w_ref, *rest, kind):
    nx = _STAGE_EXTRAS[kind]
    extra = rest[:nx]
    o_ref, wb_ref = rest[nx:nx + 2]
    first = pl.program_id(1) == 0

    @pl.when(first)
    def _():
        wb_ref[...] = w_ref[...].astype(BF16).T

    if kind == "conv":
        ext_ref = rest[nx + 2]

        @pl.when(first)
        def _():
            ext_ref[0:CARRY, :] = jnp.zeros((CARRY, ext_ref.shape[1]), F32)

    if kind == "logf":
        tab = extra[0][...]
        te = jnp.exp(tab - jnp.max(tab, axis=0, keepdims=True))
        lb = te[0:1, :] / jnp.sum(te, axis=0, keepdims=True)
        f_amp = 0.5 * (1.0 - lb)
        f_mid = lb + f_amp
    elif kind == "conv":
        w = 0.5 * extra[0][...]
        bias = 0.5 * extra[1][...]
    wb = wb_ref[...]
    step = CONV_ROWS if kind == "conv" else o_ref.shape[0]
    for r0 in range(0, o_ref.shape[0], step):
        acc = _dot(h_ref[r0:r0 + step, :], wb)
        if kind == "qscale":
            out = _silu(acc) * HG_DK ** -0.5
        elif kind == "silu":
            out = _silu(acc)
        elif kind == "logf":
            out = jnp.log(f_mid + f_amp * jnp.tanh(0.5 * acc)) * LOG2E
        else:
            u = _conv_rows(ext_ref, acc, r0, w, bias, SSM_CONV)
            out = u * (jnp.tanh(u) + 1.0)
        o_ref[r0:r0 + step, :] = out.astype(o_ref.dtype)
    if kind == "conv":
        _conv_keep_tail(ext_ref, o_ref.shape[0])


def _seg_proj(h, w_t, row_off, n, kind, extras, tm, tn, name):
    t, d = h.shape
    assert row_off % tn == 0 and len(extras) == _STAGE_EXTRAS[kind]
    scratch = [pltpu.VMEM((d, tn), BF16)]
    if kind == "conv":
        scratch.append(pltpu.VMEM((CARRY + tm, tn), F32))
    return pl.pallas_call(
        functools.partial(_seg_proj_kernel, kind=kind),
        grid=(n // tn, t // tm),
        in_specs=[
            pl.BlockSpec((tm, d), lambda j, m: (m, 0)),
            pl.BlockSpec((tn, d), lambda j, m: (row_off // tn + j, 0)),
        ] + [pl.BlockSpec((e.shape[0], tn), lambda j, m: (0, j)) for e in extras],
        out_specs=pl.BlockSpec((tm, tn), lambda j, m: (m, j)),
        out_shape=jax.ShapeDtypeStruct((t, n), BF16),
        scratch_shapes=scratch,
        compiler_params=_params(("parallel", "arbitrary")),
        name=name,
    )(h, w_t, *extras)


def _hg_constants(c):
    levels = []
    h = c // 2
    while h >= 1:
        levels.append(h)
        h //= 2
    t = np.arange(c)[:, None]
    u = np.arange(c)[None, :]
    mats = [(u <= t)]
    masks = []
    for h in levels:
        blk = 2 * h
        mid = (t // blk) * blk + h
        second = (t % blk) >= h
        if h < HG_VPU_LEVEL_MIN:
            mats.append(np.where(second, (u >= mid) & (u <= t), (u > t) & (u < mid)))
        s = u
        masks.append(((t // blk) == (s // blk)) & second & ((s % blk) < h))
    masks.append(t == u)
    pm = np.concatenate(mats, axis=0).astype(np.float32)
    mk = np.stack(masks, axis=0).astype(np.float32)
    return levels, pm, mk


def _level_exponents(b, h):
    c = b.shape[0]
    parts = []
    for start in range(0, c, 2 * h):
        mid = start + h
        ref = b[mid - 1:mid, :]
        parts.append(ref - b[start:mid, :])
        parts.append(b[mid:mid + h, :] - ref)
    return jnp.concatenate(parts, axis=0)


def _blockdiag(a, b):
    top = jnp.concatenate([a, jnp.zeros((a.shape[0], b.shape[1]), a.dtype)], axis=1)
    bot = jnp.concatenate([jnp.zeros((b.shape[0], a.shape[1]), b.dtype), b], axis=1)
    return jnp.concatenate([top, bot], axis=0)


def _blockdiag_t(kpair):
    kt = kpair.T
    half = kt.shape[0] // 2
    return _blockdiag(kt[0:half], kt[half:])


def _hgrn2_kernel(q_ref, lf_ref, i_ref, g_ref, nw_ref, pm_ref, mk_ref, *rest, chunk, levels, ncast):
    c = chunk
    w_in_refs = rest[:ncast]
    o_ref = rest[ncast]
    w_out_refs = rest[ncast + 1:2 * ncast + 1]
    st_ref, ex_ref = rest[2 * ncast + 1:]
    for src, dst in zip(w_in_refs, w_out_refs):
        dst[...] = src[...].astype(BF16)

    @pl.when(pl.program_id(1) == 0)
    def _():
        st_ref[...] = jnp.zeros_like(st_ref)

    nw = nw_ref[...]
    pm = pm_ref[...]
    nh = st_ref.shape[0]
    dk = HG_DK

    def body(ci, carry):
        r0 = pl.multiple_of(ci * c, c)
        rows = pl.ds(r0, c)
        for p in range(nh // 2):
            ha, hb = 2 * p, 2 * p + 1
            sa = slice(ha * dk, (ha + 1) * dk)
            sb = slice(hb * dk, (hb + 1) * dk)
            ps = slice(ha * dk, (hb + 1) * dk)
            lo, hi = slice(0, dk), slice(dk, 2 * dk)
            qp = q_ref[rows, ps]
            lfp = lf_ref[rows, ps]
            kp = (1.0 - jnp.exp2(lfp.astype(F32))).astype(BF16)
            ex_ref[:, ps] = _dot(pm, lfp)
            b = ex_ref[0:c, ps]
            b_last = b[c - 1:c, :]
            st_a = st_ref[ha]
            st_b = st_ref[hb]
            o = _dot_nt(qp * jnp.exp2(b).astype(BF16),
                        _blockdiag(st_a.astype(BF16), st_b.astype(BF16)))
            sc = mk_ref[len(levels)] * _dot(qp, _blockdiag_t(kp)).astype(BF16)
            n_vpu = sum(h >= HG_VPU_LEVEL_MIN for h in levels)
            for l, h in enumerate(levels):
                if h >= HG_VPU_LEVEL_MIN:
                    ex = _level_exponents(b, h)
                else:
                    ex = ex_ref[(l - n_vpu + 1) * c:(l - n_vpu + 2) * c, ps]
                e = jnp.exp2(ex).astype(BF16)
                ke = kp * e
                s = _dot(qp * e, _blockdiag_t(ke))
                sc = sc + mk_ref[l] * s.astype(BF16)
            o = o + _dot(sc, _blockdiag(i_ref[rows, sa], i_ref[rows, sb]))
            kdec = kp * jnp.exp2(b_last - b).astype(BF16)
            sdec = jnp.exp2(b_last)
            st_ref[ha] = st_a * sdec[:, lo] + _dot_tn(i_ref[rows, sa], kdec[:, lo])
            st_ref[hb] = st_b * sdec[:, hi] + _dot_tn(i_ref[rows, sb], kdec[:, hi])
            o_ref[rows, sa] = (_rms(o[:, lo], nw) * g_ref[rows, sa].astype(F32)).astype(o_ref.dtype)
            o_ref[rows, sb] = (_rms(o[:, hi], nw) * g_ref[rows, sb].astype(F32)).astype(o_ref.dtype)
        return carry

    lax.fori_loop(0, q_ref.shape[0] // c, body, 0)


def _hgrn2(qh, lf, v, gate, norm_w, n_heads, tb, chunk, hps, cast_weights):
    t = qh.shape[0]
    gate_off = 0
    n_steps = (n_heads // hps) * (t // tb)
    for wgt in cast_weights:
        assert wgt.shape[0] % (16 * n_steps) == 0, "weight rows must split into bf16-tile-aligned slices"
    w_rows = [wgt.shape[0] // n_steps for wgt in cast_weights]
    w_specs = [pl.BlockSpec((r, wgt.shape[1]), lambda h, i: (h * (t // tb) + i, 0))
               for r, wgt in zip(w_rows, cast_weights)]
    levels, pm, mk = _hg_constants(chunk)
    pm = jnp.asarray(pm, BF16)
    mk = jnp.asarray(np.concatenate([mk, mk], axis=2), BF16)
    w = hps * HG_DK
    nblk = n_heads // hps
    col = lambda off: (lambda h, i: (i, off // w + h))
    const2 = lambda h, i: (0, 0)
    return pl.pallas_call(
        functools.partial(_hgrn2_kernel, chunk=chunk, levels=tuple(levels), ncast=len(cast_weights)),
        grid=(nblk, t // tb),
        in_specs=[
            pl.BlockSpec((tb, w), col(0)),
            pl.BlockSpec((tb, w), col(0)),
            pl.BlockSpec((tb, w), col(0)),
            pl.BlockSpec((tb, w), col(gate_off)),
            pl.BlockSpec((1, HG_DV), const2),
            pl.BlockSpec(pm.shape, const2),
            pl.BlockSpec(mk.shape, lambda h, i: (0, 0, 0)),
        ] + w_specs,
        out_specs=[pl.BlockSpec((tb, w), lambda h, i: (i, h))] + w_specs,
        out_shape=[jax.ShapeDtypeStruct((t, n_heads * HG_DV), BF16)]
        + [jax.ShapeDtypeStruct(wgt.shape, BF16) for wgt in cast_weights],
        scratch_shapes=[pltpu.VMEM((hps, HG_DV, HG_DK), F32),
                        pltpu.VMEM((pm.shape[0], w), F32)],
        compiler_params=_params(("parallel", "arbitrary")),
        name="hgrn2_scan",
    )(qh, lf, v, gate, norm_w, pm, mk, *cast_weights)


def _ssd_expand_mat(hpg):
    ea = np.zeros((128, hpg * SSM_HEADDIM), np.float32)
    for h in range(hpg):
        ea[h, h * SSM_HEADDIM:(h + 1) * SSM_HEADDIM] = 1.0
    return ea


def _ssd_kernel(z_ref, x_ref, b_ref, c_ref, dt_ref, dtb_ref, alog_ref, dskip_ref, nw_ref,
                triu_ref, ea_ref, o_ref, s_ref):
    @pl.when(pl.program_id(1) == 0)
    def _():
        s_ref[...] = jnp.zeros_like(s_ref)

    gw = s_ref.shape[2]
    n = s_ref.shape[1]
    for gi in range(s_ref.shape[0]):
        _ssd_group(z_ref.at[:, gi * gw:(gi + 1) * gw], x_ref.at[:, gi * gw:(gi + 1) * gw],
                   b_ref.at[:, gi * n:(gi + 1) * n], c_ref.at[:, gi * n:(gi + 1) * n],
                   dt_ref.at[gi], dtb_ref.at[gi], alog_ref.at[gi], dskip_ref.at[gi], nw_ref.at[gi],
                   triu_ref, ea_ref, o_ref.at[:, gi * gw:(gi + 1) * gw], s_ref.at[gi])


def _ssd_group(z_ref, x_ref, b_ref, c_ref, dt_ref, dtb_ref, alog_ref, dskip_ref, nw_ref,
               triu_ref, ea_ref, o_ref, s_ref):
    q = x_ref.shape[0]
    hpg = dt_ref.shape[0]
    xs = x_ref[...].astype(F32)
    bm16 = b_ref[...]
    cm16 = c_ref[...]

    dt_r = jax.nn.softplus(dt_ref[...] + dtb_ref[...])
    da_r = dt_r * (-jnp.exp(alog_ref[...]) * LOG2E)
    da_hi, da_lo = _split_bf16(da_r)
    triu = triu_ref[...]
    acum_r = _dot(da_hi, triu) + _dot(da_lo, triu)
    pad = jnp.zeros((128 - 2 * hpg, q), F32)
    cols = jnp.concatenate([acum_r, dt_r, pad], axis=0).T

    lane = lax.broadcasted_iota(jnp.int32, (q, 128), 1)
    acol = jnp.where(lane < hpg, cols, 0.0)
    a_last = acol[q - 1:q, :]
    dtwd = jnp.exp2(a_last - acol) * pltpu.roll(cols, 128 - hpg, axis=1)
    fx = _dot(jnp.concatenate([jnp.exp2(acol).astype(BF16), dtwd.astype(BF16)], axis=0), ea_ref[...])
    ea_x = fx[0:q]
    dtwd_x = fx[q:2 * q]
    sd_x = _dot(jnp.broadcast_to(jnp.exp2(a_last), (8, 128)).astype(BF16), ea_ref[...])[0:1, :]

    xs16 = x_ref[...]
    bmt = bm16.T
    cb16 = _dot(cm16, bmt).astype(BF16)
    cs = _dot(cm16, s_ref[...].astype(BF16))
    row = lax.broadcasted_iota(jnp.int32, (q, q), 0)
    colid = lax.broadcasted_iota(jnp.int32, (q, q), 1)
    causal = row >= colid
    lo_half = lane < SSM_HEADDIM

    y_parts = []
    for p in range(hpg // 2):
        ms = []
        for h in (2 * p, 2 * p + 1):
            dm = cols[:, h:h + 1] - acum_r[h:h + 1, :]
            lm16 = jnp.exp2(jnp.where(causal, dm, -1e30)).astype(BF16)
            ms.append(cb16 * lm16 * dt_r[h:h + 1, :].astype(BF16))
        yy = _dot(jnp.concatenate(ms, axis=0), xs16[:, p * 128:(p + 1) * 128])
        y_parts.append(jnp.where(lo_half, yy[0:q], yy[q:2 * q]))

    y = jnp.concatenate(y_parts, axis=1) + cs * ea_x
    s_ref[...] = s_ref[...] * sd_x + _dot(bmt, (xs * dtwd_x).astype(BF16))

    y = y + dskip_ref[...] * xs
    y = y * z_ref[...].astype(F32)
    o_ref[...] = _rms(y, nw_ref[...]).astype(o_ref.dtype)


def _ssd(zs, xbc, dt_hm, dt_bias, a_log, d_skip, norm_w, d_inner):
    t = xbc.shape[0]
    z_off = xbc_off = 0
    q = SSD_CHUNK
    g = SSM_GROUPS
    n = SSM_DSTATE
    gw = d_inner // g
    hpg = gw // SSM_HEADDIM
    triu = jnp.asarray(np.triu(np.ones((q, q), np.float32)), BF16)
    ea = jnp.asarray(_ssd_expand_mat(hpg), BF16)
    c2 = lambda gi, i: (0, 0)
    gs = SSD_GROUPS_PER_STEP
    assert g % gs == 0 and d_inner % (gs * n) == 0, "B / C column blocks must be block-aligned"
    return pl.pallas_call(
        _ssd_kernel,
        grid=(g // gs, t // q),
        in_specs=[
            pl.BlockSpec((q, gs * gw), lambda gi, i: (i, z_off // (gs * gw) + gi)),
            pl.BlockSpec((q, gs * gw), lambda gi, i: (i, xbc_off // (gs * gw) + gi)),
            pl.BlockSpec((q, gs * n), lambda gi, i: (i, (xbc_off + d_inner) // (gs * n) + gi)),
            pl.BlockSpec((q, gs * n), lambda gi, i: (i, (xbc_off + d_inner + g * n) // (gs * n) + gi)),
            pl.BlockSpec((gs, hpg, q), lambda gi, i: (gi, 0, i)),
            pl.BlockSpec((gs, hpg, 1), lambda gi, i: (gi, 0, 0)),
            pl.BlockSpec((gs, hpg, 1), lambda gi, i: (gi, 0, 0)),
            pl.BlockSpec((gs, 1, gw), lambda gi, i: (gi, 0, 0)),
            pl.BlockSpec((gs, 1, gw), lambda gi, i: (gi, 0, 0)),
            pl.BlockSpec((q, q), c2),
            pl.BlockSpec(ea.shape, c2),
        ],
        out_specs=pl.BlockSpec((q, gs * gw), lambda gi, i: (i, gi)),
        out_shape=jax.ShapeDtypeStruct((t, d_inner), BF16),
        scratch_shapes=[pltpu.VMEM((gs, n, gw), F32)],
        compiler_params=_params(("parallel", "arbitrary")),
        name="ssd_scan",
    )(zs, xbc, xbc, xbc, dt_hm, dt_bias, a_log, d_skip, norm_w, triu, ea)


def _merge_kernel(yh_ref, ys_ref, wh_ref, ws_ref, gh_ref, gs_ref, o_ref):
    a = _dot(yh_ref[...], wh_ref[...])
    b = _dot(ys_ref[...], ws_ref[...])
    gh = _sigmoid(gh_ref[...].astype(F32))
    gs = _sigmoid(gs_ref[...].astype(F32))
    o_ref[...] = (gh * a + gs * b).astype(o_ref.dtype)


def _merge(y_hg, y_ssm, w_hg, w_ssm, gates, tm, tn):
    t = y_hg.shape[0]
    d = w_hg.shape[1]
    return pl.pallas_call(
        _merge_kernel,
        grid=(t // tm, d // tn),
        in_specs=[
            pl.BlockSpec((tm, y_hg.shape[1]), lambda m, j: (m, 0)),
            pl.BlockSpec((tm, y_ssm.shape[1]), lambda m, j: (m, 0)),
            pl.BlockSpec((w_hg.shape[0], tn), lambda m, j: (0, j)),
            pl.BlockSpec((w_ssm.shape[0], tn), lambda m, j: (0, j)),
            pl.BlockSpec((tm, tn), lambda m, j: (m, j)),
            pl.BlockSpec((tm, tn), lambda m, j: (m, d // tn + j)),
        ],
        out_specs=pl.BlockSpec((tm, tn), lambda m, j: (m, j)),
        out_shape=jax.ShapeDtypeStruct((t, d), BF16),
        compiler_params=_params(("parallel", "arbitrary")),
        name="branch_merge",
    )(y_hg, y_ssm, w_hg, w_ssm, gates, gates)


def _out_kernel(a_ref, w_ref, x_ref, post_ref, pre_ref, x1_ref, h_ref):
    w = w_ref[...]
    for r0 in range(0, x_ref.shape[0], CONV_ROWS):
        rows = slice(r0, r0 + CONV_ROWS)
        x1 = x_ref[rows, :] + _rms(_dot(a_ref[rows, :], w), post_ref[...])
        x1_ref[rows, :] = x1
        h_ref[rows, :] = _rms(x1, pre_ref[...]).astype(BF16)


def _out_proj(mixed, w_out, x, post_w, pre_w, tm):
    t, d = x.shape
    c2 = lambda m: (0, 0)
    row = lambda m: (m, 0)
    return pl.pallas_call(
        _out_kernel,
        grid=(t // tm,),
        in_specs=[
            pl.BlockSpec((tm, d), row),
            pl.BlockSpec((d, d), c2),
            pl.BlockSpec((tm, d), row),
            pl.BlockSpec((1, d), c2),
            pl.BlockSpec((1, d), c2),
        ],
        out_specs=[pl.BlockSpec((tm, d), row), pl.BlockSpec((tm, d), row)],
        out_shape=[jax.ShapeDtypeStruct((t, d), F32), jax.ShapeDtypeStruct((t, d), BF16)],
        compiler_params=_params(("parallel",)),
        name="out_proj",
    )(mixed, w_out, x, post_w, pre_w)


def _ffn_up_kernel(h_ref, wg_ref, wu_ref, cw_ref, cb_ref, o_ref, ext_ref, wgb_ref, wub_ref):
    @pl.when(pl.program_id(1) == 0)
    def _():
        wgb_ref[...] = wg_ref[...].astype(BF16)
        wub_ref[...] = wu_ref[...].astype(BF16)
        ext_ref[0:CARRY, :] = jnp.zeros((CARRY, ext_ref.shape[1]), F32)

    h = h_ref[...]
    gate = _dot(h, wgb_ref[...])
    up = _dot(h, wub_ref[...])
    conv = _conv_rows(ext_ref, gate, 0, cw_ref[...], cb_ref[...], FFN_CONV)
    _conv_keep_tail(ext_ref, o_ref.shape[0])
    o_ref[...] = (_gelu_tanh(conv) * up).astype(o_ref.dtype)


def _ffn_up(h, w_up, conv_w, conv_b, d_ff, tm, tn):
    t, d = h.shape
    k = conv_w.shape[0]
    return pl.pallas_call(
        _ffn_up_kernel,
        grid=(d_ff // tn, t // tm),
        in_specs=[
            pl.BlockSpec((tm, d), lambda j, m: (m, 0)),
            pl.BlockSpec((d, tn), lambda j, m: (0, j)),
            pl.BlockSpec((d, tn), lambda j, m: (0, d_ff // tn + j)),
            pl.BlockSpec((k, tn), lambda j, m: (0, j)),
            pl.BlockSpec((1, tn), lambda j, m: (0, j)),
        ],
        out_specs=pl.BlockSpec((tm, tn), lambda j, m: (m, j)),
        out_shape=jax.ShapeDtypeStruct((t, d_ff), BF16),
        scratch_shapes=[pltpu.VMEM((CARRY + tm, tn), F32),
                        pltpu.VMEM((d, tn), BF16), pltpu.VMEM((d, tn), BF16)],
        compiler_params=_params(("parallel", "arbitrary")),
        name="ffn_up",
    )(h, w_up, w_up, conv_w, conv_b)


def _ffn_down_kernel(a_ref, w_ref, x_ref, post_ref, o_ref):
    d = o_ref.shape[1]
    slabs = [slice(c, c + FFN_DOWN_SLAB) for c in range(0, d, FFN_DOWN_SLAB)]
    a = a_ref[...]
    for cs in slabs:
        o_ref[:, cs] = _dot(a, w_ref[:, cs])
    ssq = sum(jnp.sum(o_ref[:, cs] * o_ref[:, cs], axis=-1, keepdims=True) for cs in slabs)
    inv = lax.rsqrt(ssq / d + EPS)
    for cs in slabs:
        o_ref[:, cs] = x_ref[:, cs] + o_ref[:, cs] * inv * post_ref[:, cs]


def _ffn_down(act, w_down, x1, post_w, tm):
    t, d = x1.shape
    d_ff = act.shape[1]
    return pl.pallas_call(
        _ffn_down_kernel,
        grid=(t // tm,),
        in_specs=[
            pl.BlockSpec((tm, d_ff), lambda m: (m, 0)),
            pl.BlockSpec((d_ff, d), lambda m: (0, 0), pipeline_mode=pl.Buffered(1)),
            pl.BlockSpec((tm, d), lambda m: (m, 0)),
            pl.BlockSpec((1, d), lambda m: (0, 0)),
        ],
        out_specs=pl.BlockSpec((tm, d), lambda m: (m, 0)),
        out_shape=jax.ShapeDtypeStruct((t, d), F32),
        compiler_params=_params(("parallel",), VMEM_LIMIT_LARGE),
        name="ffn_down",
    )(act, w_down, x1, post_w)


def _layer(x, w_in, mix_pre, mix_post, lb_table, hg_norm, conv_w, conv_b, dt_bias, a_log, d_skip,
           ssm_norm, w_hg, w_ssm, w_out, ffn_pre, ffn_post, w_up, ffn_cw, ffn_cb, w_down):
    t, d = x.shape
    hg_v = w_hg.shape[0]
    hg_heads = hg_v // HG_DV
    d_inner = w_ssm.shape[0]
    ssm_heads = a_log.shape[0]
    hpg = ssm_heads // SSM_GROUPS
    gw = d_inner // SSM_GROUPS
    d_ff = w_down.shape[0]
    conv_dim = conv_w.shape[1]

    main_n = 4 * hg_v + d_inner + conv_dim
    dt_off = main_n
    gate_off = dt_off + ssm_heads
    z_off = 4 * hg_v
    xbc_off = z_off + d_inner

    w_t = w_in.T
    h, dt_t = _norm_dt(x, mix_pre.reshape(1, d), w_t[dt_off:gate_off], tm=NORM_ROWS)
    tm, tn = INPROJ_ROWS, INPROJ_TILE
    qh = _seg_proj(h, w_t, 0, hg_v, "qscale", (), tm, tn, "inproj_q")
    lf = _seg_proj(h, w_t, hg_v, hg_v, "logf", (lb_table,), tm, tn, "inproj_f")
    v = _proj_nt(h, w_t, 2 * hg_v, hg_v, tm, tn, "inproj_i")
    go = _seg_proj(h, w_t, 3 * hg_v, hg_v, "silu", (), tm, tn, "inproj_g")
    zs = _seg_proj(h, w_t, z_off, d_inner, "silu", (), tm, tn, "inproj_z")
    xbc = _seg_proj(h, w_t, xbc_off, conv_dim, "conv", (conv_w, conv_b.reshape(1, conv_dim)), tm, tn,
                    "inproj_xbc")
    gates = _proj_nt(h, w_t, gate_off, 2 * d, tm, tn, "inproj_gates")

    y_hg, w_hg16, w_ssm16, w_out16, w_down16 = _hgrn2(
        qh, lf, v, go, hg_norm.reshape(1, HG_DV), hg_heads, HG_BLOCK, HG_CHUNK, HG_HEADS_PER_STEP,
        cast_weights=(w_hg, w_ssm, w_out, w_down))

    dt_hm = dt_t.reshape(SSM_GROUPS, hpg, t)
    y_ssm = _ssd(
        zs, xbc, dt_hm,
        dt_bias.reshape(SSM_GROUPS, hpg, 1), a_log.reshape(SSM_GROUPS, hpg, 1),
        jnp.repeat(d_skip, SSM_HEADDIM).reshape(SSM_GROUPS, 1, gw),
        ssm_norm.reshape(SSM_GROUPS, 1, gw), d_inner)

    mixed = _merge(y_hg, y_ssm, w_hg16, w_ssm16, gates, tm=MERGE_ROWS, tn=MERGE_COLS)
    x1, h2 = _out_proj(mixed, w_out16, x, mix_post.reshape(1, d), ffn_pre.reshape(1, d), tm=OUT_ROWS)
    act = _ffn_up(h2, w_up, ffn_cw, ffn_cb.reshape(1, d_ff), d_ff, tm=FFN_UP_ROWS, tn=FFN_UP_COLS)
    return _ffn_down(act, w_down16, x1, ffn_post.reshape(1, d), tm=FFN_DOWN_ROWS)


def kernel(x, w_in, mix_pre_norm, mix_post_norm, hg_lb_table, hg_out_norm, ssm_conv_w, ssm_conv_b,
           ssm_dt_bias, ssm_A_log, ssm_D, ssm_out_norm, w_branch_hg, w_branch_ssm, w_out,
           ffn_pre_norm, ffn_post_norm, ffn_w_up, ffn_conv_w, ffn_conv_b, ffn_w_down):
    bsz, t, d = x.shape
    depth = w_in.shape[0]
    assert depth == 1 and hg_lb_table.shape[0] == 2, "forget-gate lower bound is computed for one layer"
    outs = []
    for b in range(bsz):
        xb = x[b]
        for l in range(depth):
            xb = _layer(xb, w_in[l], mix_pre_norm[l], mix_post_norm[l], hg_lb_table, hg_out_norm[l],
                        ssm_conv_w[l], ssm_conv_b[l], ssm_dt_bias[l], ssm_A_log[l], ssm_D[l],
                        ssm_out_norm[l], w_branch_hg[l], w_branch_ssm[l], w_out[l],
                        ffn_pre_norm[l], ffn_post_norm[l], ffn_w_up[l], ffn_conv_w[l], ffn_conv_b[l],
                        ffn_w_down[l])
        outs.append(xb)
    return jnp.stack(outs, axis=0)
```

```python
import functools

import numpy as np
import jax
import jax.numpy as jnp
from jax import lax
from jax.experimental import pallas as pl
from jax.experimental.pallas import tpu as pltpu

F32 = jnp.float32
BF16 = jnp.bfloat16
EPS = 1e-6

HG_DK = 128
HG_DV = 128
SSM_HEADDIM = 64
SSM_GROUPS = 8
SSM_DSTATE = 128
SSM_CONV = 4
FFN_CONV = 3

NORM_ROWS = 512
INPROJ_TILE = 1024
INPROJ_ROWS = 1024
HG_CHUNK = 128
HG_BLOCK = 512
HG_HEADS_PER_STEP = 16
HG_VPU_LEVEL_MIN = 8
SSD_CHUNK = 256
SSD_GROUPS_PER_STEP = 8
MERGE_ROWS, MERGE_COLS = 1024, 512
OUT_ROWS = 512
FFN_UP_ROWS, FFN_UP_COLS = 1024, 512
FFN_DOWN_ROWS = 512
FFN_DOWN_SLAB = 512
CARRY = 8
CONV_ROWS = 256

VMEM_LIMIT = 48 * 1024 * 1024
VMEM_LIMIT_LARGE = 56 * 1024 * 1024
LOG2E = 1.4426950408889634


def _params(sem, vmem=VMEM_LIMIT):
    return pltpu.CompilerParams(dimension_semantics=sem, vmem_limit_bytes=vmem)


def _dot(a, b):
    return jnp.dot(a, b, preferred_element_type=F32)


def _dot_nt(a, b):
    return lax.dot_general(a, b, (((1,), (1,)), ((), ())), preferred_element_type=F32)


def _dot_tn(a, b):
    return lax.dot_general(a, b, (((0,), (0,)), ((), ())), preferred_element_type=F32)


def _split_bf16(a):
    hi = a.astype(BF16)
    lo = (a - hi.astype(F32)).astype(BF16)
    return hi, lo


def _sigmoid(a):
    return 0.5 * jnp.tanh(0.5 * a) + 0.5


def _silu(a):
    u = 0.5 * a
    return u * (jnp.tanh(u) + 1.0)


def _gelu_tanh(a):
    c0 = float(np.sqrt(2.0 / np.pi))
    return a * (0.5 * jnp.tanh(a * (c0 + (c0 * 0.044715) * (a * a))) + 0.5)


def _rms(x, w):
    return x * lax.rsqrt(jnp.mean(x * x, axis=-1, keepdims=True) + EPS) * w


def _norm_dt_kernel(x_ref, nw_ref, wdt_ref, h_ref, dt_ref):
    h = _rms(x_ref[...], nw_ref[...]).astype(BF16)
    h_ref[...] = h
    dt_ref[...] = _dot_nt(wdt_ref[...].astype(BF16), h)


def _norm_dt(x, nw, wdt_t, tm):
    t, d = x.shape
    nh = wdt_t.shape[0]
    return pl.pallas_call(
        _norm_dt_kernel,
        grid=(t // tm,),
        in_specs=[
            pl.BlockSpec((tm, d), lambda m: (m, 0)),
            pl.BlockSpec((1, d), lambda m: (0, 0)),
            pl.BlockSpec((nh, d), lambda m: (0, 0)),
        ],
        out_specs=[pl.BlockSpec((tm, d), lambda m: (m, 0)),
                   pl.BlockSpec((nh, tm), lambda m: (0, m))],
        out_shape=[jax.ShapeDtypeStruct((t, d), BF16), jax.ShapeDtypeStruct((nh, t), F32)],
        compiler_params=_params(("parallel",)),
        name="norm_dt",
    )(x, nw, wdt_t)


def _proj_nt_kernel(h_ref, w_ref, o_ref, wb_ref):
    @pl.when(pl.program_id(1) == 0)
    def _():
        wb_ref[...] = w_ref[...].astype(BF16)

    o_ref[...] = _dot_nt(h_ref[...], wb_ref[...]).astype(o_ref.dtype)


def _proj_nt(h, w_t, row_off, n, tm, tn, name):
    t, d = h.shape
    if row_off % tn == 0:
        w_spec = pl.BlockSpec((tn, d), lambda j, m: (row_off // tn + j, 0))
    else:
        assert row_off % 8 == 0 and tn % 8 == 0
        w_spec = pl.BlockSpec((pl.Element(tn), pl.Element(d)),
                              lambda j, m: (pl.multiple_of(row_off + j * tn, 8), 0))
    return pl.pallas_call(
        _proj_nt_kernel,
        grid=(n // tn, t // tm),
        in_specs=[pl.BlockSpec((tm, d), lambda j, m: (m, 0)), w_spec],
        out_specs=pl.BlockSpec((tm, tn), lambda j, m: (m, j)),
        out_shape=jax.ShapeDtypeStruct((t, n), BF16),
        scratch_shapes=[pltpu.VMEM((tn, d), BF16)],
        compiler_params=_params(("parallel", "arbitrary")),
        name=name,
    )(h, w_t)


def _conv_rows(ext_ref, cur, r0, w, b, ksize):
    rows = cur.shape[0]
    ext_ref[CARRY + r0:CARRY + r0 + rows, :] = cur
    acc = cur * w[ksize - 1:ksize, :] + b
    for j in range(1, ksize):
        acc = acc + ext_ref[CARRY + r0 - j:CARRY + r0 - j + rows, :] * w[ksize - 1 - j:ksize - j, :]
    return acc


def _conv_keep_tail(ext_ref, q):
    ext_ref[0:CARRY, :] = ext_ref[q:q + CARRY, :]


_STAGE_EXTRAS = {"qscale": 0, "silu": 0, "logf": 1, "conv": 2}


def _seg_proj_kernel(h_ref, w_ref, *rest, kind):
    nx = _STAGE_EXTRAS[kind]
    extra = rest[:nx]
    o_ref, wb_ref = rest[nx:nx + 2]
    first = pl.program_id(1) == 0

    @pl.when(first)
    def _():
        wb_ref[...] = w_ref[...].astype(BF16)

    if kind == "conv":
        ext_ref = rest[nx + 2]

        @pl.when(first)
        def _():
            ext_ref[0:CARRY, :] = jnp.zeros((CARRY, ext_ref.shape[1]), F32)

    if kind == "logf":
        tab = extra[0][...]
        te = jnp.exp(tab - jnp.max(tab, axis=0, keepdims=True))
        lb = te[0:1, :] / jnp.sum(te, axis=0, keepdims=True)
        f_amp = 0.5 * (1.0 - lb)
        f_mid = lb + f_amp
    elif kind == "conv":
        w = 0.5 * extra[0][...]
        bias = 0.5 * extra[1][...]
    wb = wb_ref[...]
    step = CONV_ROWS if kind == "conv" else o_ref.shape[0]
    for r0 in range(0, o_ref.shape[0], step):
        acc = _dot_nt(h_ref[r0:r0 + step, :], wb)
        if kind == "qscale":
            out = _silu(acc) * HG_DK ** -0.5
        elif kind == "silu":
            out = _silu(acc)
        elif kind == "logf":
            out = jnp.log(f_mid + f_amp * jnp.tanh(0.5 * acc)) * LOG2E
        else:
            u = _conv_rows(ext_ref, acc, r0, w, bias, SSM_CONV)
            out = u * (jnp.tanh(u) + 1.0)
        o_ref[r0:r0 + step, :] = out.astype(o_ref.dtype)
    if kind == "conv":
        _conv_keep_tail(ext_ref, o_ref.shape[0])


def _seg_proj(h, w_t, row_off, n, kind, extras, tm, tn, name):
    t, d = h.shape
    assert row_off % tn == 0 and len(extras) == _STAGE_EXTRAS[kind]
    scratch = [pltpu.VMEM((tn, d), BF16)]
    if kind == "conv":
        scratch.append(pltpu.VMEM((CARRY + tm, tn), F32))
    return pl.pallas_call(
        functools.partial(_seg_proj_kernel, kind=kind),
        grid=(n // tn, t // tm),
        in_specs=[
            pl.BlockSpec((tm, d), lambda j, m: (m, 0)),
            pl.BlockSpec((tn, d), lambda j, m: (row_off // tn + j, 0)),
        ] + [pl.BlockSpec((e.shape[0], tn), lambda j, m: (0, j)) for e in extras],
        out_specs=pl.BlockSpec((tm, tn), lambda j, m: (m, j)),
        out_shape=jax.ShapeDtypeStruct((t, n), BF16),
        scratch_shapes=scratch,
        compiler_params=_params(("parallel", "arbitrary")),
        name=name,
    )(h, w_t, *extras)


def _hg_constants(c):
    levels = []
    h = c // 2
    while h >= 1:
        levels.append(h)
        h //= 2
    t = np.arange(c)[:, None]
    u = np.arange(c)[None, :]
    mats = [(u <= t)]
    masks = []
    for h in levels:
        blk = 2 * h
        mid = (t // blk) * blk + h
        second = (t % blk) >= h
        if h < HG_VPU_LEVEL_MIN:
            mats.append(np.where(second, (u >= mid) & (u <= t), (u > t) & (u < mid)))
        s = u
        masks.append(((t // blk) == (s // blk)) & second & ((s % blk) < h))
    masks.append(t == u)
    pm = np.concatenate(mats, axis=0).astype(np.float32)
    mk = np.stack(masks, axis=0).astype(np.float32)
    return levels, pm, mk


def _level_exponents(b, h):
    c = b.shape[0]
    parts = []
    for start in range(0, c, 2 * h):
        mid = start + h
        ref = b[mid - 1:mid, :]
        parts.append(ref - b[start:mid, :])
        parts.append(b[mid:mid + h, :] - ref)
    return jnp.concatenate(parts, axis=0)


def _blockdiag(a, b):
    top = jnp.concatenate([a, jnp.zeros((a.shape[0], b.shape[1]), a.dtype)], axis=1)
    bot = jnp.concatenate([jnp.zeros((b.shape[0], a.shape[1]), b.dtype), b], axis=1)
    return jnp.concatenate([top, bot], axis=0)


def _blockdiag_t(kpair):
    kt = kpair.T
    half = kt.shape[0] // 2
    return _blockdiag(kt[0:half], kt[half:])


def _hgrn2_kernel(q_ref, lf_ref, i_ref, g_ref, nw_ref, pm_ref, mk_ref, *rest, chunk, levels, ncast):
    c = chunk
    w_in_refs = rest[:ncast]
    o_ref = rest[ncast]
    w_out_refs = rest[ncast + 1:2 * ncast + 1]
    st_ref, ex_ref = rest[2 * ncast + 1:]
    for src, dst in zip(w_in_refs, w_out_refs):
        dst[...] = src[...].astype(BF16)

    @pl.when(pl.program_id(1) == 0)
    def _():
        st_ref[...] = jnp.zeros_like(st_ref)

    nw = nw_ref[...]
    pm = pm_ref[...]
    nh = st_ref.shape[0]
    dk = HG_DK

    def body(ci, carry):
        r0 = pl.multiple_of(ci * c, c)
        rows = pl.ds(r0, c)
        for p in range(nh // 2):
            ha, hb = 2 * p, 2 * p + 1
            sa = slice(ha * dk, (ha + 1) * dk)
            sb = slice(hb * dk, (hb + 1) * dk)
            ps = slice(ha * dk, (hb + 1) * dk)
            lo, hi = slice(0, dk), slice(dk, 2 * dk)
            qp = q_ref[rows, ps]
            lfp = lf_ref[rows, ps]
            kp = (1.0 - jnp.exp2(lfp.astype(F32))).astype(BF16)
            ex_ref[:, ps] = _dot(pm, lfp)
            b = ex_ref[0:c, ps]
            b_last = b[c - 1:c, :]
            st_a = st_ref[ha]
            st_b = st_ref[hb]
            o = _dot_nt(qp * jnp.exp2(b).astype(BF16),
                        _blockdiag(st_a.astype(BF16), st_b.astype(BF16)))
            sc = mk_ref[len(levels)] * _dot(qp, _blockdiag_t(kp)).astype(BF16)
            n_vpu = sum(h >= HG_VPU_LEVEL_MIN for h in levels)
            for l, h in enumerate(levels):
                if h >= HG_VPU_LEVEL_MIN:
                    ex = _level_exponents(b, h)
                else:
                    ex = ex_ref[(l - n_vpu + 1) * c:(l - n_vpu + 2) * c, ps]
                e = jnp.exp2(ex).astype(BF16)
                ke = kp * e
                s = _dot(qp * e, _blockdiag_t(ke))
                sc = sc + mk_ref[l] * s.astype(BF16)
            o = o + _dot(sc, _blockdiag(i_ref[rows, sa], i_ref[rows, sb]))
            kdec = kp * jnp.exp2(b_last - b).astype(BF16)
            sdec = jnp.exp2(b_last)
            st_ref[ha] = st_a * sdec[:, lo] + _dot_tn(i_ref[rows, sa], kdec[:, lo])
            st_ref[hb] = st_b * sdec[:, hi] + _dot_tn(i_ref[rows, sb], kdec[:, hi])
            o_ref[rows, sa] = (_rms(o[:, lo], nw) * g_ref[rows, sa].astype(F32)).astype(o_ref.dtype)
            o_ref[rows, sb] = (_rms(o[:, hi], nw) * g_ref[rows, sb].astype(F32)).astype(o_ref.dtype)
        return carry

    lax.fori_loop(0, q_ref.shape[0] // c, body, 0)


def _hgrn2(qh, lf, v, gate, norm_w, n_heads, tb, chunk, hps, cast_weights):
    t = qh.shape[0]
    gate_off = 0
    n_steps = (n_heads // hps) * (t // tb)
    for wgt in cast_weights:
        assert wgt.shape[0] % (16 * n_steps) == 0, "weight rows must split into bf16-tile-aligned slices"
    w_rows = [wgt.shape[0] // n_steps for wgt in cast_weights]
    w_specs = [pl.BlockSpec((r, wgt.shape[1]), lambda h, i: (h * (t // tb) + i, 0))
               for r, wgt in zip(w_rows, cast_weights)]
    levels, pm, mk = _hg_constants(chunk)
    pm = jnp.asarray(pm, BF16)
    mk = jnp.asarray(np.concatenate([mk, mk], axis=2), BF16)
    w = hps * HG_DK
    nblk = n_heads // hps
    col = lambda off: (lambda h, i: (i, off // w + h))
    const2 = lambda h, i: (0, 0)
    return pl.pallas_call(
        functools.partial(_hgrn2_kernel, chunk=chunk, levels=tuple(levels), ncast=len(cast_weights)),
        grid=(nblk, t // tb),
        in_specs=[
            pl.BlockSpec((tb, w), col(0)),
            pl.BlockSpec((tb, w), col(0)),
            pl.BlockSpec((tb, w), col(0)),
            pl.BlockSpec((tb, w), col(gate_off)),
            pl.BlockSpec((1, HG_DV), const2),
            pl.BlockSpec(pm.shape, const2),
            pl.BlockSpec(mk.shape, lambda h, i: (0, 0, 0)),
        ] + w_specs,
        out_specs=[pl.BlockSpec((tb, w), lambda h, i: (i, h))] + w_specs,
        out_shape=[jax.ShapeDtypeStruct((t, n_heads * HG_DV), BF16)]
        + [jax.ShapeDtypeStruct(wgt.shape, BF16) for wgt in cast_weights],
        scratch_shapes=[pltpu.VMEM((hps, HG_DV, HG_DK), F32),
                        pltpu.VMEM((pm.shape[0], w), F32)],
        compiler_params=_params(("parallel", "arbitrary")),
        name="hgrn2_scan",
    )(qh, lf, v, gate, norm_w, pm, mk, *cast_weights)


def _ssd_expand_mat(hpg):
    ea = np.zeros((128, hpg * SSM_HEADDIM), np.float32)
    for h in range(hpg):
        ea[h, h * SSM_HEADDIM:(h + 1) * SSM_HEADDIM] = 1.0
    return ea


def _ssd_kernel(z_ref, x_ref, b_ref, c_ref, dt_ref, dtb_ref, alog_ref, dskip_ref, nw_ref,
                triu_ref, ea_ref, o_ref, s_ref):
    @pl.when(pl.program_id(1) == 0)
    def _():
        s_ref[...] = jnp.zeros_like(s_ref)

    gw = s_ref.shape[2]
    n = s_ref.shape[1]
    for gi in range(s_ref.shape[0]):
        _ssd_group(z_ref.at[:, gi * gw:(gi + 1) * gw], x_ref.at[:, gi * gw:(gi + 1) * gw],
                   b_ref.at[:, gi * n:(gi + 1) * n], c_ref.at[:, gi * n:(gi + 1) * n],
                   dt_ref.at[gi], dtb_ref.at[gi], alog_ref.at[gi], dskip_ref.at[gi], nw_ref.at[gi],
                   triu_ref, ea_ref, o_ref.at[:, gi * gw:(gi + 1) * gw], s_ref.at[gi])


def _ssd_group(z_ref, x_ref, b_ref, c_ref, dt_ref, dtb_ref, alog_ref, dskip_ref, nw_ref,
               triu_ref, ea_ref, o_ref, s_ref):
    q = x_ref.shape[0]
    hpg = dt_ref.shape[0]
    xs = x_ref[...].astype(F32)
    bm16 = b_ref[...]
    cm16 = c_ref[...]

    dt_r = jax.nn.softplus(dt_ref[...] + dtb_ref[...])
    da_r = dt_r * (-jnp.exp(alog_ref[...]) * LOG2E)
    da_hi, da_lo = _split_bf16(da_r)
    triu = triu_ref[...]
    acum_r = _dot(da_hi, triu) + _dot(da_lo, triu)
    pad = jnp.zeros((128 - 2 * hpg, q), F32)
    cols = jnp.concatenate([acum_r, dt_r, pad], axis=0).T

    lane = lax.broadcasted_iota(jnp.int32, (q, 128), 1)
    acol = jnp.where(lane < hpg, cols, 0.0)
    a_last = acol[q - 1:q, :]
    dtwd = jnp.exp2(a_last - acol) * pltpu.roll(cols, 128 - hpg, axis=1)
    fx = _dot(jnp.concatenate([jnp.exp2(acol).astype(BF16), dtwd.astype(BF16)], axis=0), ea_ref[...])
    ea_x = fx[0:q]
    dtwd_x = fx[q:2 * q]
    sd_x = _dot(jnp.broadcast_to(jnp.exp2(a_last), (8, 128)).astype(BF16), ea_ref[...])[0:1, :]

    xs16 = x_ref[...]
    bmt = bm16.T
    cb16 = _dot(cm16, bmt).astype(BF16)
    cs = _dot(cm16, s_ref[...].astype(BF16))
    row = lax.broadcasted_iota(jnp.int32, (q, q), 0)
    colid = lax.broadcasted_iota(jnp.int32, (q, q), 1)
    causal = row >= colid
    lo_half = lane < SSM_HEADDIM

    y_parts = []
    for p in range(hpg // 2):
        ms = []
        for h in (2 * p, 2 * p + 1):
            dm = cols[:, h:h + 1] - acum_r[h:h + 1, :]
            lm16 = jnp.exp2(jnp.where(causal, dm, -1e30)).astype(BF16)
            ms.append(cb16 * lm16 * dt_r[h:h + 1, :].astype(BF16))
        yy = _dot(jnp.concatenate(ms, axis=0), xs16[:, p * 128:(p + 1) * 128])
        y_parts.append(jnp.where(lo_half, yy[0:q], yy[q:2 * q]))

    y = jnp.concatenate(y_parts, axis=1) + cs * ea_x
    s_ref[...] = s_ref[...] * sd_x + _dot(bmt, (xs * dtwd_x).astype(BF16))

    y = y + dskip_ref[...] * xs
    y = y * z_ref[...].astype(F32)
    o_ref[...] = _rms(y, nw_ref[...]).astype(o_ref.dtype)


def _ssd(zs, xbc, dt_hm, dt_bias, a_log, d_skip, norm_w, d_inner):
    t = xbc.shape[0]
    z_off = xbc_off = 0
    q = SSD_CHUNK
    g = SSM_GROUPS
    n = SSM_DSTATE
    gw = d_inner // g
    hpg = gw // SSM_HEADDIM
    triu = jnp.asarray(np.triu(np.ones((q, q), np.float32)), BF16)
    ea = jnp.asarray(_ssd_expand_mat(hpg), BF16)
    c2 = lambda gi, i: (0, 0)
    gs = SSD_GROUPS_PER_STEP
    assert g % gs == 0 and d_inner % (gs * n) == 0, "B / C column blocks must be block-aligned"
    return pl.pallas_call(
        _ssd_kernel,
        grid=(g // gs, t // q),
        in_specs=[
            pl.BlockSpec((q, gs * gw), lambda gi, i: (i, z_off // (gs * gw) + gi)),
            pl.BlockSpec((q, gs * gw), lambda gi, i: (i, xbc_off // (gs * gw) + gi)),
            pl.BlockSpec((q, gs * n), lambda gi, i: (i, (xbc_off + d_inner) // (gs * n) + gi)),
            pl.BlockSpec((q, gs * n), lambda gi, i: (i, (xbc_off + d_inner + g * n) // (gs * n) + gi)),
            pl.BlockSpec((gs, hpg, q), lambda gi, i: (gi, 0, i)),
            pl.BlockSpec((gs, hpg, 1), lambda gi, i: (gi, 0, 0)),
            pl.BlockSpec((gs, hpg, 1), lambda gi, i: (gi, 0, 0)),
            pl.BlockSpec((gs, 1, gw), lambda gi, i: (gi, 0, 0)),
            pl.BlockSpec((gs, 1, gw), lambda gi, i: (gi, 0, 0)),
            pl.BlockSpec((q, q), c2),
            pl.BlockSpec(ea.shape, c2),
        ],
        out_specs=pl.BlockSpec((q, gs * gw), lambda gi, i: (i, gi)),
        out_shape=jax.ShapeDtypeStruct((t, d_inner), BF16),
        scratch_shapes=[pltpu.VMEM((gs, n, gw), F32)],
        compiler_params=_params(("parallel", "arbitrary")),
        name="ssd_scan",
    )(zs, xbc, xbc, xbc, dt_hm, dt_bias, a_log, d_skip, norm_w, triu, ea)


def _merge_kernel(yh_ref, ys_ref, wh_ref, ws_ref, gh_ref, gs_ref, o_ref):
    a = _dot(yh_ref[...], wh_ref[...])
    b = _dot(ys_ref[...], ws_ref[...])
    gh = _sigmoid(gh_ref[...].astype(F32))
    gs = _sigmoid(gs_ref[...].astype(F32))
    o_ref[...] = (gh * a + gs * b).astype(o_ref.dtype)


def _merge(y_hg, y_ssm, w_hg, w_ssm, gates, tm, tn):
    t = y_hg.shape[0]
    d = w_hg.shape[1]
    return pl.pallas_call(
        _merge_kernel,
        grid=(t // tm, d // tn),
        in_specs=[
            pl.BlockSpec((tm, y_hg.shape[1]), lambda m, j: (m, 0)),
            pl.BlockSpec((tm, y_ssm.shape[1]), lambda m, j: (m, 0)),
            pl.BlockSpec((w_hg.shape[0], tn), lambda m, j: (0, j)),
            pl.BlockSpec((w_ssm.shape[0], tn), lambda m, j: (0, j)),
            pl.BlockSpec((tm, tn), lambda m, j: (m, j)),
            pl.BlockSpec((tm, tn), lambda m, j: (m, d // tn + j)),
        ],
        out_specs=pl.BlockSpec((tm, tn), lambda m, j: (m, j)),
        out_shape=jax.ShapeDtypeStruct((t, d), BF16),
        compiler_params=_params(("parallel", "arbitrary")),
        name="branch_merge",
    )(y_hg, y_ssm, w_hg, w_ssm, gates, gates)


def _out_kernel(a_ref, w_ref, x_ref, post_ref, pre_ref, x1_ref, h_ref):
    w = w_ref[...]
    for r0 in range(0, x_ref.shape[0], CONV_ROWS):
        rows = slice(r0, r0 + CONV_ROWS)
        x1 = x_ref[rows, :] + _rms(_dot(a_ref[rows, :], w), post_ref[...])
        x1_ref[rows, :] = x1
        h_ref[rows, :] = _rms(x1, pre_ref[...]).astype(BF16)


def _out_proj(mixed, w_out, x, post_w, pre_w, tm):
    t, d = x.shape
    c2 = lambda m: (0, 0)
    row = lambda m: (m, 0)
    return pl.pallas_call(
        _out_kernel,
        grid=(t // tm,),
        in_specs=[
            pl.BlockSpec((tm, d), row),
            pl.BlockSpec((d, d), c2),
            pl.BlockSpec((tm, d), row),
            pl.BlockSpec((1, d), c2),
            pl.BlockSpec((1, d), c2),
        ],
        out_specs=[pl.BlockSpec((tm, d), row), pl.BlockSpec((tm, d), row)],
        out_shape=[jax.ShapeDtypeStruct((t, d), F32), jax.ShapeDtypeStruct((t, d), BF16)],
        compiler_params=_params(("parallel",)),
        name="out_proj",
    )(mixed, w_out, x, post_w, pre_w)


def _ffn_up_kernel(h_ref, wg_ref, wu_ref, cw_ref, cb_ref, o_ref, ext_ref, wgb_ref, wub_ref):
    @pl.when(pl.program_id(1) == 0)
    def _():
        wgb_ref[...] = wg_ref[...].astype(BF16)
        wub_ref[...] = wu_ref[...].astype(BF16)
        ext_ref[0:CARRY, :] = jnp.zeros((CARRY, ext_ref.shape[1]), F32)

    h = h_ref[...]
    gate = _dot(h, wgb_ref[...])
    up = _dot(h, wub_ref[...])
    conv = _conv_rows(ext_ref, gate, 0, cw_ref[...], cb_ref[...], FFN_CONV)
    _conv_keep_tail(ext_ref, o_ref.shape[0])
    o_ref[...] = (_gelu_tanh(conv) * up).astype(o_ref.dtype)


def _ffn_up(h, w_up, conv_w, conv_b, d_ff, tm, tn):
    t, d = h.shape
    k = conv_w.shape[0]
    return pl.pallas_call(
        _ffn_up_kernel,
        grid=(d_ff // tn, t // tm),
        in_specs=[
            pl.BlockSpec((tm, d), lambda j, m: (m, 0)),
            pl.BlockSpec((d, tn), lambda j, m: (0, j)),
            pl.BlockSpec((d, tn), lambda j, m: (0, d_ff // tn + j)),
            pl.BlockSpec((k, tn), lambda j, m: (0, j)),
            pl.BlockSpec((1, tn), lambda j, m: (0, j)),
        ],
        out_specs=pl.BlockSpec((tm, tn), lambda j, m: (m, j)),
        out_shape=jax.ShapeDtypeStruct((t, d_ff), BF16),
        scratch_shapes=[pltpu.VMEM((CARRY + tm, tn), F32),
                        pltpu.VMEM((d, tn), BF16), pltpu.VMEM((d, tn), BF16)],
        compiler_params=_params(("parallel", "arbitrary")),
        name="ffn_up",
    )(h, w_up, w_up, conv_w, conv_b)


def _ffn_down_kernel(a_ref, w_ref, x_ref, post_ref, o_ref):
    d = o_ref.shape[1]
    slabs = [slice(c, c + FFN_DOWN_SLAB) for c in range(0, d, FFN_DOWN_SLAB)]
    a = a_ref[...]
    for cs in slabs:
        o_ref[:, cs] = _dot(a, w_ref[:, cs])
    ssq = sum(jnp.sum(o_ref[:, cs] * o_ref[:, cs], axis=-1, keepdims=True) for cs in slabs)
    inv = lax.rsqrt(ssq / d + EPS)
    for cs in slabs:
        o_ref[:, cs] = x_ref[:, cs] + o_ref[:, cs] * inv * post_ref[:, cs]


def _ffn_down(act, w_down, x1, post_w, tm):
    t, d = x1.shape
    d_ff = act.shape[1]
    return pl.pallas_call(
        _ffn_down_kernel,
        grid=(t // tm,),
        in_specs=[
            pl.BlockSpec((tm, d_ff), lambda m: (m, 0)),
            pl.BlockSpec((d_ff, d), lambda m: (0, 0), pipeline_mode=pl.Buffered(1)),
            pl.BlockSpec((tm, d), lambda m: (m, 0)),
            pl.BlockSpec((1, d), lambda m: (0, 0)),
        ],
        out_specs=pl.BlockSpec((tm, d), lambda m: (m, 0)),
        out_shape=jax.ShapeDtypeStruct((t, d), F32),
        compiler_params=_params(("parallel",), VMEM_LIMIT_LARGE),
        name="ffn_down",
    )(act, w_down, x1, post_w)


def _layer(x, w_in, mix_pre, mix_post, lb_table, hg_norm, conv_w, conv_b, dt_bias, a_log, d_skip,
           ssm_norm, w_hg, w_ssm, w_out, ffn_pre, ffn_post, w_up, ffn_cw, ffn_cb, w_down):
    t, d = x.shape
    hg_v = w_hg.shape[0]
    hg_heads = hg_v // HG_DV
    d_inner = w_ssm.shape[0]
    ssm_heads = a_log.shape[0]
    hpg = ssm_heads // SSM_GROUPS
    gw = d_inner // SSM_GROUPS
    d_ff = w_down.shape[0]
    conv_dim = conv_w.shape[1]

    main_n = 4 * hg_v + d_inner + conv_dim
    dt_off = main_n
    gate_off = dt_off + ssm_heads
    z_off = 4 * hg_v
    xbc_off = z_off + d_inner

    w_t = w_in.T
    h, dt_t = _norm_dt(x, mix_pre.reshape(1, d), w_t[dt_off:gate_off], tm=NORM_ROWS)
    tm, tn = INPROJ_ROWS, INPROJ_TILE
    qh = _seg_proj(h, w_t, 0, hg_v, "qscale", (), tm, tn, "inproj_q")
    lf = _seg_proj(h, w_t, hg_v, hg_v, "logf", (lb_table,), tm, tn, "inproj_f")
    v = _proj_nt(h, w_t, 2 * hg_v, hg_v, tm, tn, "inproj_i")
    go = _seg_proj(h, w_t, 3 * hg_v, hg_v, "silu", (), tm, tn, "inproj_g")
    zs = _seg_proj(h, w_t, z_off, d_inner, "silu", (), tm, tn, "inproj_z")
    xbc = _seg_proj(h, w_t, xbc_off, conv_dim, "conv", (conv_w, conv_b.reshape(1, conv_dim)), tm, tn,
                    "inproj_xbc")
    gates = _proj_nt(h, w_t, gate_off, 2 * d, tm, tn, "inproj_gates")

    y_hg, w_hg16, w_ssm16, w_out16, w_down16 = _hgrn2(
        qh, lf, v, go, hg_norm.reshape(1, HG_DV), hg_heads, HG_BLOCK, HG_CHUNK, HG_HEADS_PER_STEP,
        cast_weights=(w_hg, w_ssm, w_out, w_down))

    dt_hm = dt_t.reshape(SSM_GROUPS, hpg, t)
    y_ssm = _ssd(
        zs, xbc, dt_hm,
        dt_bias.reshape(SSM_GROUPS, hpg, 1), a_log.reshape(SSM_GROUPS, hpg, 1),
        jnp.repeat(d_skip, SSM_HEADDIM).reshape(SSM_GROUPS, 1, gw),
        ssm_norm.reshape(SSM_GROUPS, 1, gw), d_inner)

    mixed = _merge(y_hg, y_ssm, w_hg16, w_ssm16, gates, tm=MERGE_ROWS, tn=MERGE_COLS)
    x1, h2 = _out_proj(mixed, w_out16, x, mix_post.reshape(1, d), ffn_pre.reshape(1, d), tm=OUT_ROWS)
    act = _ffn_up(h2, w_up, ffn_cw, ffn_cb.reshape(1, d_ff), d_ff, tm=FFN_UP_ROWS, tn=FFN_UP_COLS)
    return _ffn_down(act, w_down16, x1, ffn_post.reshape(1, d), tm=FFN_DOWN_ROWS)


def kernel(x, w_in, mix_pre_norm, mix_post_norm, hg_lb_table, hg_out_norm, ssm_conv_w, ssm_conv_b,
           ssm_dt_bias, ssm_A_log, ssm_D, ssm_out_norm, w_branch_hg, w_branch_ssm, w_out,
           ffn_pre_norm, ffn_post_norm, ffn_w_up, ffn_conv_w, ffn_conv_b, ffn_w_down):
    bsz, t, d = x.shape
    depth = w_in.shape[0]
    assert depth == 1 and hg_lb_table.shape[0] == 2, "forget-gate lower bound is computed for one layer"
    outs = []
    for b in range(bsz):
        xb = x[b]
        for l in range(depth):
            xb = _layer(xb, w_in[l], mix_pre_norm[l], mix_post_norm[l], hg_lb_table, hg_out_norm[l],
                        ssm_conv_w[l], ssm_conv_b[l], ssm_dt_bias[l], ssm_A_log[l], ssm_D[l],
                        ssm_out_norm[l], w_branch_hg[l], w_branch_ssm[l], w_out[l],
                        ffn_pre_norm[l], ffn_post_norm[l], ffn_w_up[l], ffn_conv_w[l], ffn_conv_b[l],
                        ffn_w_down[l])
        outs.append(xb)
    return jnp.stack(outs, axis=0)
```

```python
import functools

import numpy as np
import jax
import jax.numpy as jnp
from jax import lax
from jax.experimental import pallas as pl
from jax.experimental.pallas import tpu as pltpu

F32 = jnp.float32
BF16 = jnp.bfloat16
EPS = 1e-6

HG_DK = 128
HG_DV = 128
SSM_HEADDIM = 64
SSM_GROUPS = 8
SSM_DSTATE = 128
SSM_CONV = 4
FFN_CONV = 3

NORM_ROWS = 512
INPROJ_TILE = 1024
INPROJ_ROWS = 1024
HG_CHUNK = 128
HG_BLOCK = 512
HG_HEADS_PER_STEP = 16
HG_VPU_LEVEL_MIN = 8
SSD_CHUNK = 256
SSD_CHUNKS_PER_STEP = 2
SSD_GROUPS_PER_STEP = 8
MERGE_ROWS, MERGE_COLS = 1024, 512
OUT_ROWS = 512
FFN_UP_ROWS, FFN_UP_COLS = 1024, 512
FFN_DOWN_ROWS = 512
FFN_DOWN_SLAB = 512
CARRY = 8
CONV_ROWS = 256

VMEM_LIMIT = 48 * 1024 * 1024
VMEM_LIMIT_LARGE = 56 * 1024 * 1024
LOG2E = 1.4426950408889634


def _params(sem, vmem=VMEM_LIMIT):
    return pltpu.CompilerParams(dimension_semantics=sem, vmem_limit_bytes=vmem)


def _dot(a, b):
    return jnp.dot(a, b, preferred_element_type=F32)


def _dot_nt(a, b):
    return lax.dot_general(a, b, (((1,), (1,)), ((), ())), preferred_element_type=F32)


def _dot_tn(a, b):
    return lax.dot_general(a, b, (((0,), (0,)), ((), ())), preferred_element_type=F32)


def _split_bf16(a):
    hi = a.astype(BF16)
    lo = (a - hi.astype(F32)).astype(BF16)
    return hi, lo


def _sigmoid(a):
    return 0.5 * jnp.tanh(0.5 * a) + 0.5


def _silu(a):
    u = 0.5 * a
    return u * (jnp.tanh(u) + 1.0)


def _gelu_tanh(a):
    c0 = float(np.sqrt(2.0 / np.pi))
    return a * (0.5 * jnp.tanh(a * (c0 + (c0 * 0.044715) * (a * a))) + 0.5)


def _rms(x, w):
    return x * lax.rsqrt(jnp.mean(x * x, axis=-1, keepdims=True) + EPS) * w


def _norm_dt_kernel(x_ref, nw_ref, wdt_ref, h_ref, dt_ref):
    h = _rms(x_ref[...], nw_ref[...]).astype(BF16)
    h_ref[...] = h
    dt_ref[...] = _dot_nt(wdt_ref[...].astype(BF16), h)


def _norm_dt(x, nw, wdt_t, tm):
    t, d = x.shape
    nh = wdt_t.shape[0]
    return pl.pallas_call(
        _norm_dt_kernel,
        grid=(t // tm,),
        in_specs=[
            pl.BlockSpec((tm, d), lambda m: (m, 0)),
            pl.BlockSpec((1, d), lambda m: (0, 0)),
            pl.BlockSpec((nh, d), lambda m: (0, 0)),
        ],
        out_specs=[pl.BlockSpec((tm, d), lambda m: (m, 0)),
                   pl.BlockSpec((nh, tm), lambda m: (0, m))],
        out_shape=[jax.ShapeDtypeStruct((t, d), BF16), jax.ShapeDtypeStruct((nh, t), F32)],
        compiler_params=_params(("parallel",)),
        name="norm_dt",
    )(x, nw, wdt_t)


def _proj_nt_kernel(h_ref, w_ref, o_ref, wb_ref):
    @pl.when(pl.program_id(1) == 0)
    def _():
        wb_ref[...] = w_ref[...].astype(BF16)

    o_ref[...] = _dot_nt(h_ref[...], wb_ref[...]).astype(o_ref.dtype)


def _proj_nt(h, w_t, row_off, n, tm, tn, name):
    t, d = h.shape
    if row_off % tn == 0:
        w_spec = pl.BlockSpec((tn, d), lambda j, m: (row_off // tn + j, 0))
    else:
        assert row_off % 8 == 0 and tn % 8 == 0
        w_spec = pl.BlockSpec((pl.Element(tn), pl.Element(d)),
                              lambda j, m: (pl.multiple_of(row_off + j * tn, 8), 0))
    return pl.pallas_call(
        _proj_nt_kernel,
        grid=(n // tn, t // tm),
        in_specs=[pl.BlockSpec((tm, d), lambda j, m: (m, 0)), w_spec],
        out_specs=pl.BlockSpec((tm, tn), lambda j, m: (m, j)),
        out_shape=jax.ShapeDtypeStruct((t, n), BF16),
        scratch_shapes=[pltpu.VMEM((tn, d), BF16)],
        compiler_params=_params(("parallel", "arbitrary")),
        name=name,
    )(h, w_t)


def _conv_rows(ext_ref, cur, r0, w, b, ksize):
    rows = cur.shape[0]
    ext_ref[CARRY + r0:CARRY + r0 + rows, :] = cur
    acc = cur * w[ksize - 1:ksize, :] + b
    for j in range(1, ksize):
        acc = acc + ext_ref[CARRY + r0 - j:CARRY + r0 - j + rows, :] * w[ksize - 1 - j:ksize - j, :]
    return acc


def _conv_keep_tail(ext_ref, q):
    ext_ref[0:CARRY, :] = ext_ref[q:q + CARRY, :]


_STAGE_EXTRAS = {"qscale": 0, "silu": 0, "logf": 1, "conv": 2}


def _seg_proj_kernel(h_ref, w_ref, *rest, kind):
    nx = _STAGE_EXTRAS[kind]
    extra = rest[:nx]
    o_ref, wb_ref = rest[nx:nx + 2]
    first = pl.program_id(1) == 0

    @pl.when(first)
    def _():
        wb_ref[...] = w_ref[...].astype(BF16)

    if kind == "conv":
        ext_ref = rest[nx + 2]

        @pl.when(first)
        def _():
            ext_ref[0:CARRY, :] = jnp.zeros((CARRY, ext_ref.shape[1]), F32)

    if kind == "logf":
        tab = extra[0][...]
        te = jnp.exp(tab - jnp.max(tab, axis=0, keepdims=True))
        lb = te[0:1, :] / jnp.sum(te, axis=0, keepdims=True)
        f_amp = 0.5 * (1.0 - lb)
        f_mid = lb + f_amp
    elif kind == "conv":
        w = 0.5 * extra[0][...]
        bias = 0.5 * extra[1][...]
    wb = wb_ref[...]
    step = CONV_ROWS if kind == "conv" else o_ref.shape[0]
    for r0 in range(0, o_ref.shape[0], step):
        acc = _dot_nt(h_ref[r0:r0 + step, :], wb)
        if kind == "qscale":
            out = _silu(acc) * HG_DK ** -0.5
        elif kind == "silu":
            out = _silu(acc)
        elif kind == "logf":
            out = jnp.log(f_mid + f_amp * jnp.tanh(0.5 * acc)) * LOG2E
        else:
            u = _conv_rows(ext_ref, acc, r0, w, bias, SSM_CONV)
            out = u * (jnp.tanh(u) + 1.0)
        o_ref[r0:r0 + step, :] = out.astype(o_ref.dtype)
    if kind == "conv":
        _conv_keep_tail(ext_ref, o_ref.shape[0])


def _seg_proj(h, w_t, row_off, n, kind, extras, tm, tn, name):
    t, d = h.shape
    assert row_off % tn == 0 and len(extras) == _STAGE_EXTRAS[kind]
    scratch = [pltpu.VMEM((tn, d), BF16)]
    if kind == "conv":
        scratch.append(pltpu.VMEM((CARRY + tm, tn), F32))
    return pl.pallas_call(
        functools.partial(_seg_proj_kernel, kind=kind),
        grid=(n // tn, t // tm),
        in_specs=[
            pl.BlockSpec((tm, d), lambda j, m: (m, 0)),
            pl.BlockSpec((tn, d), lambda j, m: (row_off // tn + j, 0)),
        ] + [pl.BlockSpec((e.shape[0], tn), lambda j, m: (0, j)) for e in extras],
        out_specs=pl.BlockSpec((tm, tn), lambda j, m: (m, j)),
        out_shape=jax.ShapeDtypeStruct((t, n), BF16),
        scratch_shapes=scratch,
        compiler_params=_params(("parallel", "arbitrary")),
        name=name,
    )(h, w_t, *extras)


def _hg_constants(c):
    levels = []
    h = c // 2
    while h >= 1:
        levels.append(h)
        h //= 2
    t = np.arange(c)[:, None]
    u = np.arange(c)[None, :]
    mats = [(u <= t)]
    masks = []
    for h in levels:
        blk = 2 * h
        mid = (t // blk) * blk + h
        second = (t % blk) >= h
        if h < HG_VPU_LEVEL_MIN:
            mats.append(np.where(second, (u >= mid) & (u <= t), (u > t) & (u < mid)))
        s = u
        masks.append(((t // blk) == (s // blk)) & second & ((s % blk) < h))
    masks.append(t == u)
    pm = np.concatenate(mats, axis=0).astype(np.float32)
    mk = np.stack(masks, axis=0).astype(np.float32)
    return levels, pm, mk


def _level_exponents(b, h):
    c = b.shape[0]
    parts = []
    for start in range(0, c, 2 * h):
        mid = start + h
        ref = b[mid - 1:mid, :]
        parts.append(ref - b[start:mid, :])
        parts.append(b[mid:mid + h, :] - ref)
    return jnp.concatenate(parts, axis=0)


def _blockdiag(a, b):
    top = jnp.concatenate([a, jnp.zeros((a.shape[0], b.shape[1]), a.dtype)], axis=1)
    bot = jnp.concatenate([jnp.zeros((b.shape[0], a.shape[1]), b.dtype), b], axis=1)
    return jnp.concatenate([top, bot], axis=0)


def _blockdiag_t(kpair):
    kt = kpair.T
    half = kt.shape[0] // 2
    return _blockdiag(kt[0:half], kt[half:])


def _hgrn2_kernel(q_ref, lf_ref, i_ref, g_ref, nw_ref, pm_ref, mk_ref, *rest, chunk, levels, ncast):
    c = chunk
    w_in_refs = rest[:ncast]
    o_ref = rest[ncast]
    w_out_refs = rest[ncast + 1:2 * ncast + 1]
    st_ref, ex_ref = rest[2 * ncast + 1:]
    for src, dst in zip(w_in_refs, w_out_refs):
        dst[...] = src[...].astype(BF16)

    @pl.when(pl.program_id(1) == 0)
    def _():
        st_ref[...] = jnp.zeros_like(st_ref)

    nw = nw_ref[...]
    pm = pm_ref[...]
    nh = st_ref.shape[0]
    dk = HG_DK

    def body(ci, carry):
        r0 = pl.multiple_of(ci * c, c)
        rows = pl.ds(r0, c)
        for p in range(nh // 2):
            ha, hb = 2 * p, 2 * p + 1
            sa = slice(ha * dk, (ha + 1) * dk)
            sb = slice(hb * dk, (hb + 1) * dk)
            ps = slice(ha * dk, (hb + 1) * dk)
            lo, hi = slice(0, dk), slice(dk, 2 * dk)
            qp = q_ref[rows, ps]
            lfp = lf_ref[rows, ps]
            kp = (1.0 - jnp.exp2(lfp.astype(F32))).astype(BF16)
            ex_ref[:, ps] = _dot(pm, lfp)
            b = ex_ref[0:c, ps]
            b_last = b[c - 1:c, :]
            st_a = st_ref[ha]
            st_b = st_ref[hb]
            o = _dot_nt(qp * jnp.exp2(b).astype(BF16),
                        _blockdiag(st_a.astype(BF16), st_b.astype(BF16)))
            sc = mk_ref[len(levels)] * _dot(qp, _blockdiag_t(kp)).astype(BF16)
            n_vpu = sum(h >= HG_VPU_LEVEL_MIN for h in levels)
            for l, h in enumerate(levels):
                if h >= HG_VPU_LEVEL_MIN:
                    ex = _level_exponents(b, h)
                else:
                    ex = ex_ref[(l - n_vpu + 1) * c:(l - n_vpu + 2) * c, ps]
                e = jnp.exp2(ex).astype(BF16)
                ke = kp * e
                s = _dot(qp * e, _blockdiag_t(ke))
                sc = sc + mk_ref[l] * s.astype(BF16)
            o = o + _dot(sc, _blockdiag(i_ref[rows, sa], i_ref[rows, sb]))
            kdec = kp * jnp.exp2(b_last - b).astype(BF16)
            sdec = jnp.exp2(b_last)
            st_ref[ha] = st_a * sdec[:, lo] + _dot_tn(i_ref[rows, sa], kdec[:, lo])
            st_ref[hb] = st_b * sdec[:, hi] + _dot_tn(i_ref[rows, sb], kdec[:, hi])
            o_ref[rows, sa] = (_rms(o[:, lo], nw) * g_ref[rows, sa].astype(F32)).astype(o_ref.dtype)
            o_ref[rows, sb] = (_rms(o[:, hi], nw) * g_ref[rows, sb].astype(F32)).astype(o_ref.dtype)
        return carry

    lax.fori_loop(0, q_ref.shape[0] // c, body, 0)


def _hgrn2(qh, lf, v, gate, norm_w, n_heads, tb, chunk, hps, cast_weights):
    t = qh.shape[0]
    gate_off = 0
    n_steps = (n_heads // hps) * (t // tb)
    for wgt in cast_weights:
        assert wgt.shape[0] % (16 * n_steps) == 0, "weight rows must split into bf16-tile-aligned slices"
    w_rows = [wgt.shape[0] // n_steps for wgt in cast_weights]
    w_specs = [pl.BlockSpec((r, wgt.shape[1]), lambda h, i: (h * (t // tb) + i, 0))
               for r, wgt in zip(w_rows, cast_weights)]
    levels, pm, mk = _hg_constants(chunk)
    pm = jnp.asarray(pm, BF16)
    mk = jnp.asarray(np.concatenate([mk, mk], axis=2), BF16)
    w = hps * HG_DK
    nblk = n_heads // hps
    col = lambda off: (lambda h, i: (i, off // w + h))
    const2 = lambda h, i: (0, 0)
    return pl.pallas_call(
        functools.partial(_hgrn2_kernel, chunk=chunk, levels=tuple(levels), ncast=len(cast_weights)),
        grid=(nblk, t // tb),
        in_specs=[
            pl.BlockSpec((tb, w), col(0)),
            pl.BlockSpec((tb, w), col(0)),
            pl.BlockSpec((tb, w), col(0)),
            pl.BlockSpec((tb, w), col(gate_off)),
            pl.BlockSpec((1, HG_DV), const2),
            pl.BlockSpec(pm.shape, const2),
            pl.BlockSpec(mk.shape, lambda h, i: (0, 0, 0)),
        ] + w_specs,
        out_specs=[pl.BlockSpec((tb, w), lambda h, i: (i, h))] + w_specs,
        out_shape=[jax.ShapeDtypeStruct((t, n_heads * HG_DV), BF16)]
        + [jax.ShapeDtypeStruct(wgt.shape, BF16) for wgt in cast_weights],
        scratch_shapes=[pltpu.VMEM((hps, HG_DV, HG_DK), F32),
                        pltpu.VMEM((pm.shape[0], w), F32)],
        compiler_params=_params(("parallel", "arbitrary")),
        name="hgrn2_scan",
    )(qh, lf, v, gate, norm_w, pm, mk, *cast_weights)


def _ssd_expand_mat(hpg):
    ea = np.zeros((128, hpg * SSM_HEADDIM), np.float32)
    for h in range(hpg):
        ea[h, h * SSM_HEADDIM:(h + 1) * SSM_HEADDIM] = 1.0
    return ea


def _ssd_kernel(z_ref, x_ref, b_ref, c_ref, dt_ref, dtb_ref, alog_ref, dskip_ref, nw_ref,
                triu_ref, ea_ref, o_ref, s_ref):
    @pl.when(pl.program_id(1) == 0)
    def _():
        s_ref[...] = jnp.zeros_like(s_ref)

    gw = s_ref.shape[2]
    n = s_ref.shape[1]
    q = triu_ref.shape[0]
    for r0 in range(0, x_ref.shape[0], q):
        rows = slice(r0, r0 + q)
        for gi in range(s_ref.shape[0]):
            cg = slice(gi * gw, (gi + 1) * gw)
            cn = slice(gi * n, (gi + 1) * n)
            _ssd_group(z_ref.at[rows, cg], x_ref.at[rows, cg], b_ref.at[rows, cn], c_ref.at[rows, cn],
                       dt_ref.at[gi, :, rows], dtb_ref.at[gi], alog_ref.at[gi], dskip_ref.at[gi],
                       nw_ref.at[gi], triu_ref, ea_ref, o_ref.at[rows, cg], s_ref.at[gi])


def _ssd_group(z_ref, x_ref, b_ref, c_ref, dt_ref, dtb_ref, alog_ref, dskip_ref, nw_ref,
               triu_ref, ea_ref, o_ref, s_ref):
    q = x_ref.shape[0]
    hpg = dt_ref.shape[0]
    xs = x_ref[...].astype(F32)
    bm16 = b_ref[...]
    cm16 = c_ref[...]

    dt_r = jax.nn.softplus(dt_ref[...] + dtb_ref[...])
    da_r = dt_r * (-jnp.exp(alog_ref[...]) * LOG2E)
    da_hi, da_lo = _split_bf16(da_r)
    triu = triu_ref[...]
    acum_r = _dot(da_hi, triu) + _dot(da_lo, triu)
    pad = jnp.zeros((128 - 2 * hpg, q), F32)
    cols = jnp.concatenate([acum_r, dt_r, pad], axis=0).T

    lane = lax.broadcasted_iota(jnp.int32, (q, 128), 1)
    acol = jnp.where(lane < hpg, cols, 0.0)
    a_last = acol[q - 1:q, :]
    dtwd = jnp.exp2(a_last - acol) * pltpu.roll(cols, 128 - hpg, axis=1)
    fx = _dot(jnp.concatenate([jnp.exp2(acol).astype(BF16), dtwd.astype(BF16)], axis=0), ea_ref[...])
    ea_x = fx[0:q]
    dtwd_x = fx[q:2 * q]
    sd_x = _dot(jnp.broadcast_to(jnp.exp2(a_last), (8, 128)).astype(BF16), ea_ref[...])[0:1, :]

    xs16 = x_ref[...]
    bmt = bm16.T
    cb16 = _dot(cm16, bmt).astype(BF16)
    cs = _dot(cm16, s_ref[...].astype(BF16))
    row = lax.broadcasted_iota(jnp.int32, (q, q), 0)
    colid = lax.broadcasted_iota(jnp.int32, (q, q), 1)
    causal = row >= colid
    lo_half = lane < SSM_HEADDIM

    y_parts = []
    for p in range(hpg // 2):
        ms = []
        for h in (2 * p, 2 * p + 1):
            dm = cols[:, h:h + 1] - acum_r[h:h + 1, :]
            lm16 = jnp.exp2(jnp.where(causal, dm, -1e30)).astype(BF16)
            ms.append(cb16 * lm16 * dt_r[h:h + 1, :].astype(BF16))
        yy = _dot(jnp.concatenate(ms, axis=0), xs16[:, p * 128:(p + 1) * 128])
        y_parts.append(jnp.where(lo_half, yy[0:q], yy[q:2 * q]))

    y = jnp.concatenate(y_parts, axis=1) + cs * ea_x
    s_ref[...] = s_ref[...] * sd_x + _dot(bmt, (xs * dtwd_x).astype(BF16))

    y = y + dskip_ref[...] * xs
    y = y * z_ref[...].astype(F32)
    o_ref[...] = _rms(y, nw_ref[...]).astype(o_ref.dtype)


def _ssd(zs, xbc, dt_hm, dt_bias, a_log, d_skip, norm_w, d_inner):
    t = xbc.shape[0]
    z_off = xbc_off = 0
    q = SSD_CHUNK
    g = SSM_GROUPS
    n = SSM_DSTATE
    gw = d_inner // g
    hpg = gw // SSM_HEADDIM
    triu = jnp.asarray(np.triu(np.ones((q, q), np.float32)), BF16)
    ea = jnp.asarray(_ssd_expand_mat(hpg), BF16)
    c2 = lambda gi, i: (0, 0)
    gs = SSD_GROUPS_PER_STEP
    assert g % gs == 0 and d_inner % (gs * n) == 0, "B / C column blocks must be block-aligned"
    tb = q * SSD_CHUNKS_PER_STEP
    return pl.pallas_call(
        _ssd_kernel,
        grid=(g // gs, t // tb),
        in_specs=[
            pl.BlockSpec((tb, gs * gw), lambda gi, i: (i, z_off // (gs * gw) + gi)),
            pl.BlockSpec((tb, gs * gw), lambda gi, i: (i, xbc_off // (gs * gw) + gi)),
            pl.BlockSpec((tb, gs * n), lambda gi, i: (i, (xbc_off + d_inner) // (gs * n) + gi)),
            pl.BlockSpec((tb, gs * n), lambda gi, i: (i, (xbc_off + d_inner + g * n) // (gs * n) + gi)),
            pl.BlockSpec((gs, hpg, tb), lambda gi, i: (gi, 0, i)),
            pl.BlockSpec((gs, hpg, 1), lambda gi, i: (gi, 0, 0)),
            pl.BlockSpec((gs, hpg, 1), lambda gi, i: (gi, 0, 0)),
            pl.BlockSpec((gs, 1, gw), lambda gi, i: (gi, 0, 0)),
            pl.BlockSpec((gs, 1, gw), lambda gi, i: (gi, 0, 0)),
            pl.BlockSpec((q, q), c2),
            pl.BlockSpec(ea.shape, c2),
        ],
        out_specs=pl.BlockSpec((tb, gs * gw), lambda gi, i: (i, gi)),
        out_shape=jax.ShapeDtypeStruct((t, d_inner), BF16),
        scratch_shapes=[pltpu.VMEM((gs, n, gw), F32)],
        compiler_params=_params(("parallel", "arbitrary")),
        name="ssd_scan",
    )(zs, xbc, xbc, xbc, dt_hm, dt_bias, a_log, d_skip, norm_w, triu, ea)


def _merge_kernel(yh_ref, ys_ref, wh_ref, ws_ref, gh_ref, gs_ref, o_ref):
    a = _dot(yh_ref[...], wh_ref[...])
    b = _dot(ys_ref[...], ws_ref[...])
    gh = _sigmoid(gh_ref[...].astype(F32))
    gs = _sigmoid(gs_ref[...].astype(F32))
    o_ref[...] = (gh * a + gs * b).astype(o_ref.dtype)


def _merge(y_hg, y_ssm, w_hg, w_ssm, gates, tm, tn):
    t = y_hg.shape[0]
    d = w_hg.shape[1]
    return pl.pallas_call(
        _merge_kernel,
        grid=(t // tm, d // tn),
        in_specs=[
            pl.BlockSpec((tm, y_hg.shape[1]), lambda m, j: (m, 0)),
            pl.BlockSpec((tm, y_ssm.shape[1]), lambda m, j: (m, 0)),
            pl.BlockSpec((w_hg.shape[0], tn), lambda m, j: (0, j)),
            pl.BlockSpec((w_ssm.shape[0], tn), lambda m, j: (0, j)),
            pl.BlockSpec((tm, tn), lambda m, j: (m, j)),
            pl.BlockSpec((tm, tn), lambda m, j: (m, d // tn + j)),
        ],
        out_specs=pl.BlockSpec((tm, tn), lambda m, j: (m, j)),
        out_shape=jax.ShapeDtypeStruct((t, d), BF16),
        compiler_params=_params(("parallel", "arbitrary")),
        name="branch_merge",
    )(y_hg, y_ssm, w_hg, w_ssm, gates, gates)


def _out_kernel(a_ref, w_ref, x_ref, post_ref, pre_ref, x1_ref, h_ref):
    w = w_ref[...]
    for r0 in range(0, x_ref.shape[0], CONV_ROWS):
        rows = slice(r0, r0 + CONV_ROWS)
        x1 = x_ref[rows, :] + _rms(_dot(a_ref[rows, :], w), post_ref[...])
        x1_ref[rows, :] = x1
        h_ref[rows, :] = _rms(x1, pre_ref[...]).astype(BF16)


def _out_proj(mixed, w_out, x, post_w, pre_w, tm):
    t, d = x.shape
    c2 = lambda m: (0, 0)
    row = lambda m: (m, 0)
    return pl.pallas_call(
        _out_kernel,
        grid=(t // tm,),
        in_specs=[
            pl.BlockSpec((tm, d), row),
            pl.BlockSpec((d, d), c2),
            pl.BlockSpec((tm, d), row),
            pl.BlockSpec((1, d), c2),
            pl.BlockSpec((1, d), c2),
        ],
        out_specs=[pl.BlockSpec((tm, d), row), pl.BlockSpec((tm, d), row)],
        out_shape=[jax.ShapeDtypeStruct((t, d), F32), jax.ShapeDtypeStruct((t, d), BF16)],
        compiler_params=_params(("parallel",)),
        name="out_proj",
    )(mixed, w_out, x, post_w, pre_w)


def _ffn_up_kernel(h_ref, wg_ref, wu_ref, cw_ref, cb_ref, o_ref, ext_ref, wgb_ref, wub_ref):
    @pl.when(pl.program_id(1) == 0)
    def _():
        wgb_ref[...] = wg_ref[...].astype(BF16)
        wub_ref[...] = wu_ref[...].astype(BF16)
        ext_ref[0:CARRY, :] = jnp.zeros((CARRY, ext_ref.shape[1]), F32)

    h = h_ref[...]
    gate = _dot(h, wgb_ref[...])
    up = _dot(h, wub_ref[...])
    conv = _conv_rows(ext_ref, gate, 0, cw_ref[...], cb_ref[...], FFN_CONV)
    _conv_keep_tail(ext_ref, o_ref.shape[0])
    o_ref[...] = (_gelu_tanh(conv) * up).astype(o_ref.dtype)


def _ffn_up(h, w_up, conv_w, conv_b, d_ff, tm, tn):
    t, d = h.shape
    k = conv_w.shape[0]
    return pl.pallas_call(
        _ffn_up_kernel,
        grid=(d_ff // tn, t // tm),
        in_specs=[
            pl.BlockSpec((tm, d), lambda j, m: (m, 0)),
            pl.BlockSpec((d, tn), lambda j, m: (0, j)),
            pl.BlockSpec((d, tn), lambda j, m: (0, d_ff // tn + j)),
            pl.BlockSpec((k, tn), lambda j, m: (0, j)),
            pl.BlockSpec((1, tn), lambda j, m: (0, j)),
        ],
        out_specs=pl.BlockSpec((tm, tn), lambda j, m: (m, j)),
        out_shape=jax.ShapeDtypeStruct((t, d_ff), BF16),
        scratch_shapes=[pltpu.VMEM((CARRY + tm, tn), F32),
                        pltpu.VMEM((d, tn), BF16), pltpu.VMEM((d, tn), BF16)],
        compiler_params=_params(("parallel", "arbitrary")),
        name="ffn_up",
    )(h, w_up, w_up, conv_w, conv_b)


def _ffn_down_kernel(a_ref, w_ref, x_ref, post_ref, o_ref):
    d = o_ref.shape[1]
    slabs = [slice(c, c + FFN_DOWN_SLAB) for c in range(0, d, FFN_DOWN_SLAB)]
    a = a_ref[...]
    for cs in slabs:
        o_ref[:, cs] = _dot(a, w_ref[:, cs])
    ssq = sum(jnp.sum(o_ref[:, cs] * o_ref[:, cs], axis=-1, keepdims=True) for cs in slabs)
    inv = lax.rsqrt(ssq / d + EPS)
    for cs in slabs:
        o_ref[:, cs] = x_ref[:, cs] + o_ref[:, cs] * inv * post_ref[:, cs]


def _ffn_down(act, w_down, x1, post_w, tm):
    t, d = x1.shape
    d_ff = act.shape[1]
    return pl.pallas_call(
        _ffn_down_kernel,
        grid=(t // tm,),
        in_specs=[
            pl.BlockSpec((tm, d_ff), lambda m: (m, 0)),
            pl.BlockSpec((d_ff, d), lambda m: (0, 0), pipeline_mode=pl.Buffered(1)),
            pl.BlockSpec((tm, d), lambda m: (m, 0)),
            pl.BlockSpec((1, d), lambda m: (0, 0)),
        ],
        out_specs=pl.BlockSpec((tm, d), lambda m: (m, 0)),
        out_shape=jax.ShapeDtypeStruct((t, d), F32),
        compiler_params=_params(("parallel",), VMEM_LIMIT_LARGE),
        name="ffn_down",
    )(act, w_down, x1, post_w)


def _layer(x, w_in, mix_pre, mix_post, lb_table, hg_norm, conv_w, conv_b, dt_bias, a_log, d_skip,
           ssm_norm, w_hg, w_ssm, w_out, ffn_pre, ffn_post, w_up, ffn_cw, ffn_cb, w_down):
    t, d = x.shape
    hg_v = w_hg.shape[0]
    hg_heads = hg_v // HG_DV
    d_inner = w_ssm.shape[0]
    ssm_heads = a_log.shape[0]
    hpg = ssm_heads // SSM_GROUPS
    gw = d_inner // SSM_GROUPS
    d_ff = w_down.shape[0]
    conv_dim = conv_w.shape[1]

    main_n = 4 * hg_v + d_inner + conv_dim
    dt_off = main_n
    gate_off = dt_off + ssm_heads
    z_off = 4 * hg_v
    xbc_off = z_off + d_inner

    w_t = w_in.T
    h, dt_t = _norm_dt(x, mix_pre.reshape(1, d), w_t[dt_off:gate_off], tm=NORM_ROWS)
    tm, tn = INPROJ_ROWS, INPROJ_TILE
    qh = _seg_proj(h, w_t, 0, hg_v, "qscale", (), tm, tn, "inproj_q")
    lf = _seg_proj(h, w_t, hg_v, hg_v, "logf", (lb_table,), tm, tn, "inproj_f")
    v = _proj_nt(h, w_t, 2 * hg_v, hg_v, tm, tn, "inproj_i")
    go = _seg_proj(h, w_t, 3 * hg_v, hg_v, "silu", (), tm, tn, "inproj_g")
    zs = _seg_proj(h, w_t, z_off, d_inner, "silu", (), tm, tn, "inproj_z")
    xbc = _seg_proj(h, w_t, xbc_off, conv_dim, "conv", (conv_w, conv_b.reshape(1, conv_dim)), tm, tn,
                    "inproj_xbc")
    gates = _proj_nt(h, w_t, gate_off, 2 * d, tm, tn, "inproj_gates")

    y_hg, w_hg16, w_ssm16, w_out16, w_down16 = _hgrn2(
        qh, lf, v, go, hg_norm.reshape(1, HG_DV), hg_heads, HG_BLOCK, HG_CHUNK, HG_HEADS_PER_STEP,
        cast_weights=(w_hg, w_ssm, w_out, w_down))

    dt_hm = dt_t.reshape(SSM_GROUPS, hpg, t)
    y_ssm = _ssd(
        zs, xbc, dt_hm,
        dt_bias.reshape(SSM_GROUPS, hpg, 1), a_log.reshape(SSM_GROUPS, hpg, 1),
        jnp.repeat(d_skip, SSM_HEADDIM).reshape(SSM_GROUPS, 1, gw),
        ssm_norm.reshape(SSM_GROUPS, 1, gw), d_inner)

    mixed = _merge(y_hg, y_ssm, w_hg16, w_ssm16, gates, tm=MERGE_ROWS, tn=MERGE_COLS)
    x1, h2 = _out_proj(mixed, w_out16, x, mix_post.reshape(1, d), ffn_pre.reshape(1, d), tm=OUT_ROWS)
    act = _ffn_up(h2, w_up, ffn_cw, ffn_cb.reshape(1, d_ff), d_ff, tm=FFN_UP_ROWS, tn=FFN_UP_COLS)
    return _ffn_down(act, w_down16, x1, ffn_post.reshape(1, d), tm=FFN_DOWN_ROWS)


def kernel(x, w_in, mix_pre_norm, mix_post_norm, hg_lb_table, hg_out_norm, ssm_conv_w, ssm_conv_b,
           ssm_dt_bias, ssm_A_log, ssm_D, ssm_out_norm, w_branch_hg, w_branch_ssm, w_out,
           ffn_pre_norm, ffn_post_norm, ffn_w_up, ffn_conv_w, ffn_conv_b, ffn_w_down):
    bsz, t, d = x.shape
    depth = w_in.shape[0]
    assert depth == 1 and hg_lb_table.shape[0] == 2, "forget-gate lower bound is computed for one layer"
    outs = []
    for b in range(bsz):
        xb = x[b]
        for l in range(depth):
            xb = _layer(xb, w_in[l], mix_pre_norm[l], mix_post_norm[l], hg_lb_table, hg_out_norm[l],
                        ssm_conv_w[l], ssm_conv_b[l], ssm_dt_bias[l], ssm_A_log[l], ssm_D[l],
                        ssm_out_norm[l], w_branch_hg[l], w_branch_ssm[l], w_out[l],
                        ffn_pre_norm[l], ffn_post_norm[l], ffn_w_up[l], ffn_conv_w[l], ffn_conv_b[l],
                        ffn_w_down[l])
        outs.append(xb)
    return jnp.stack(outs, axis=0)
```

```python
import functools

import numpy as np
import jax
import jax.numpy as jnp
from jax import lax
from jax.experimental import pallas as pl
from jax.experimental.pallas import tpu as pltpu

F32 = jnp.float32
BF16 = jnp.bfloat16
EPS = 1e-6

HG_DK = 128
HG_DV = 128
SSM_HEADDIM = 64
SSM_GROUPS = 8
SSM_DSTATE = 128
SSM_CONV = 4
FFN_CONV = 3

NORM_ROWS = 512
INPROJ_TILE = 1024
INPROJ_ROWS = 1024
HG_CHUNK = 128
HG_BLOCK = 512
HG_HEADS_PER_STEP = 16
HG_VPU_LEVEL_MIN = 8
SSD_CHUNK = 256
SSD_CHUNKS_PER_STEP = 2
SSD_GROUPS_PER_STEP = 8
MERGE_ROWS, MERGE_COLS = 1024, 512
OUT_ROWS = 512
FFN_UP_ROWS, FFN_UP_COLS = 1024, 512
FFN_DOWN_ROWS = 512
FFN_DOWN_SLAB = 512
CARRY = 8
CONV_ROWS = 256

VMEM_LIMIT = 48 * 1024 * 1024
VMEM_LIMIT_LARGE = 56 * 1024 * 1024
LOG2E = 1.4426950408889634


def _params(sem, vmem=VMEM_LIMIT):
    return pltpu.CompilerParams(dimension_semantics=sem, vmem_limit_bytes=vmem)


def _dot(a, b):
    return jnp.dot(a, b, preferred_element_type=F32)


def _dot_nt(a, b):
    return lax.dot_general(a, b, (((1,), (1,)), ((), ())), preferred_element_type=F32)


def _dot_tn(a, b):
    return lax.dot_general(a, b, (((0,), (0,)), ((), ())), preferred_element_type=F32)


def _split_bf16(a):
    hi = a.astype(BF16)
    lo = (a - hi.astype(F32)).astype(BF16)
    return hi, lo


def _sigmoid(a):
    return 0.5 * jnp.tanh(0.5 * a) + 0.5


def _silu(a):
    u = 0.5 * a
    return u * (jnp.tanh(u) + 1.0)


def _gelu_tanh(a):
    c0 = float(np.sqrt(2.0 / np.pi))
    return a * (0.5 * jnp.tanh(a * (c0 + (c0 * 0.044715) * (a * a))) + 0.5)


def _rms(x, w):
    return x * lax.rsqrt(jnp.mean(x * x, axis=-1, keepdims=True) + EPS) * w


def _norm_dt_kernel(x_ref, nw_ref, wdt_ref, h_ref, dt_ref):
    h = _rms(x_ref[...], nw_ref[...]).astype(BF16)
    h_ref[...] = h
    dt_ref[...] = _dot_nt(wdt_ref[...].astype(BF16), h)


def _norm_dt(x, nw, wdt_t, tm):
    t, d = x.shape
    nh = wdt_t.shape[0]
    return pl.pallas_call(
        _norm_dt_kernel,
        grid=(t // tm,),
        in_specs=[
            pl.BlockSpec((tm, d), lambda m: (m, 0)),
            pl.BlockSpec((1, d), lambda m: (0, 0)),
            pl.BlockSpec((nh, d), lambda m: (0, 0)),
        ],
        out_specs=[pl.BlockSpec((tm, d), lambda m: (m, 0)),
                   pl.BlockSpec((nh, tm), lambda m: (0, m))],
        out_shape=[jax.ShapeDtypeStruct((t, d), BF16), jax.ShapeDtypeStruct((nh, t), F32)],
        compiler_params=_params(("parallel",)),
        name="norm_dt",
    )(x, nw, wdt_t)


def _proj_nt_kernel(h_ref, w_ref, o_ref, wb_ref):
    @pl.when(pl.program_id(1) == 0)
    def _():
        wb_ref[...] = w_ref[...].astype(BF16)

    o_ref[...] = _dot_nt(h_ref[...], wb_ref[...]).astype(o_ref.dtype)


def _proj_nt(h, w_t, row_off, n, tm, tn, name):
    t, d = h.shape
    if row_off % tn == 0:
        w_spec = pl.BlockSpec((tn, d), lambda j, m: (row_off // tn + j, 0))
    else:
        assert row_off % 8 == 0 and tn % 8 == 0
        w_spec = pl.BlockSpec((pl.Element(tn), pl.Element(d)),
                              lambda j, m: (pl.multiple_of(row_off + j * tn, 8), 0))
    return pl.pallas_call(
        _proj_nt_kernel,
        grid=(n // tn, t // tm),
        in_specs=[pl.BlockSpec((tm, d), lambda j, m: (m, 0)), w_spec],
        out_specs=pl.BlockSpec((tm, tn), lambda j, m: (m, j)),
        out_shape=jax.ShapeDtypeStruct((t, n), BF16),
        scratch_shapes=[pltpu.VMEM((tn, d), BF16)],
        compiler_params=_params(("parallel", "arbitrary")),
        name=name,
    )(h, w_t)


def _conv_rows(ext_ref, cur, r0, w, b, ksize):
    rows = cur.shape[0]
    ext_ref[CARRY + r0:CARRY + r0 + rows, :] = cur
    acc = cur * w[ksize - 1:ksize, :] + b
    for j in range(1, ksize):
        acc = acc + ext_ref[CARRY + r0 - j:CARRY + r0 - j + rows, :] * w[ksize - 1 - j:ksize - j, :]
    return acc


def _conv_keep_tail(ext_ref, q):
    ext_ref[0:CARRY, :] = ext_ref[q:q + CARRY, :]


_STAGE_EXTRAS = {"qscale": 0, "silu": 0, "logf": 1, "conv": 2}


def _seg_proj_kernel(h_ref, w_ref, *rest, kind):
    nx = _STAGE_EXTRAS[kind]
    extra = rest[:nx]
    o_ref, wb_ref = rest[nx:nx + 2]
    first = pl.program_id(1) == 0

    @pl.when(first)
    def _():
        wb_ref[...] = w_ref[...].astype(BF16)

    if kind == "conv":
        ext_ref = rest[nx + 2]

        @pl.when(first)
        def _():
            ext_ref[0:CARRY, :] = jnp.zeros((CARRY, ext_ref.shape[1]), F32)

    if kind == "logf":
        tab = extra[0][...]
        te = jnp.exp(tab - jnp.max(tab, axis=0, keepdims=True))
        lb = te[0:1, :] / jnp.sum(te, axis=0, keepdims=True)
        f_amp = 0.5 * (1.0 - lb)
        f_mid = lb + f_amp
    elif kind == "conv":
        w = 0.5 * extra[0][...]
        bias = 0.5 * extra[1][...]
    wb = wb_ref[...]
    step = CONV_ROWS if kind == "conv" else o_ref.shape[0]
    for r0 in range(0, o_ref.shape[0], step):
        acc = _dot_nt(h_ref[r0:r0 + step, :], wb)
        if kind == "qscale":
            out = _silu(acc) * HG_DK ** -0.5
        elif kind == "silu":
            out = _silu(acc)
        elif kind == "logf":
            out = jnp.log(f_mid + f_amp * jnp.tanh(0.5 * acc)) * LOG2E
        else:
            u = _conv_rows(ext_ref, acc, r0, w, bias, SSM_CONV)
            out = u * (jnp.tanh(u) + 1.0)
        o_ref[r0:r0 + step, :] = out.astype(o_ref.dtype)
    if kind == "conv":
        _conv_keep_tail(ext_ref, o_ref.shape[0])


def _seg_proj(h, w_t, row_off, n, kind, extras, tm, tn, name):
    t, d = h.shape
    assert row_off % tn == 0 and len(extras) == _STAGE_EXTRAS[kind]
    scratch = [pltpu.VMEM((tn, d), BF16)]
    if kind == "conv":
        scratch.append(pltpu.VMEM((CARRY + tm, tn), F32))
    return pl.pallas_call(
        functools.partial(_seg_proj_kernel, kind=kind),
        grid=(n // tn, t // tm),
        in_specs=[
            pl.BlockSpec((tm, d), lambda j, m: (m, 0)),
            pl.BlockSpec((tn, d), lambda j, m: (row_off // tn + j, 0)),
        ] + [pl.BlockSpec((e.shape[0], tn), lambda j, m: (0, j)) for e in extras],
        out_specs=pl.BlockSpec((tm, tn), lambda j, m: (m, j)),
        out_shape=jax.ShapeDtypeStruct((t, n), BF16),
        scratch_shapes=scratch,
        compiler_params=_params(("parallel", "arbitrary")),
        name=name,
    )(h, w_t, *extras)


def _hg_constants(c):
    levels = []
    h = c // 2
    while h >= 1:
        levels.append(h)
        h //= 2
    t = np.arange(c)[:, None]
    u = np.arange(c)[None, :]
    mats = [(u <= t)]
    masks = []
    for h in levels:
        blk = 2 * h
        mid = (t // blk) * blk + h
        second = (t % blk) >= h
        if h < HG_VPU_LEVEL_MIN:
            mats.append(np.where(second, (u >= mid) & (u <= t), (u > t) & (u < mid)))
        s = u
        masks.append(((t // blk) == (s // blk)) & second & ((s % blk) < h))
    masks.append(t == u)
    pm = np.concatenate(mats, axis=0).astype(np.float32)
    mk = np.stack(masks, axis=0).astype(np.float32)
    return levels, pm, mk


def _level_exponents(b, h):
    c = b.shape[0]
    parts = []
    for start in range(0, c, 2 * h):
        mid = start + h
        ref = b[mid - 1:mid, :]
        parts.append(ref - b[start:mid, :])
        parts.append(b[mid:mid + h, :] - ref)
    return jnp.concatenate(parts, axis=0)


def _blockdiag(a, b):
    top = jnp.concatenate([a, jnp.zeros((a.shape[0], b.shape[1]), a.dtype)], axis=1)
    bot = jnp.concatenate([jnp.zeros((b.shape[0], a.shape[1]), b.dtype), b], axis=1)
    return jnp.concatenate([top, bot], axis=0)


def _blockdiag_t(kpair):
    kt = kpair.T
    half = kt.shape[0] // 2
    return _blockdiag(kt[0:half], kt[half:])


def _hgrn2_kernel(q_ref, lf_ref, i_ref, g_ref, nw_ref, pm_ref, mk_ref, *rest, chunk, levels, ncast):
    c = chunk
    w_in_refs = rest[:ncast]
    o_ref = rest[ncast]
    w_out_refs = rest[ncast + 1:2 * ncast + 1]
    st_ref, ex_ref = rest[2 * ncast + 1:]
    for src, dst in zip(w_in_refs, w_out_refs):
        dst[...] = src[...].astype(BF16)

    @pl.when(pl.program_id(1) == 0)
    def _():
        st_ref[...] = jnp.zeros_like(st_ref)

    nw = nw_ref[...]
    pm = pm_ref[...]
    nh = st_ref.shape[0]
    dk = HG_DK

    def body(ci, carry):
        r0 = pl.multiple_of(ci * c, c)
        rows = pl.ds(r0, c)
        for p in range(nh // 2):
            ha, hb = 2 * p, 2 * p + 1
            sa = slice(ha * dk, (ha + 1) * dk)
            sb = slice(hb * dk, (hb + 1) * dk)
            ps = slice(ha * dk, (hb + 1) * dk)
            lo, hi = slice(0, dk), slice(dk, 2 * dk)
            qp = q_ref[rows, ps]
            lfp = lf_ref[rows, ps]
            kp = (1.0 - jnp.exp2(lfp.astype(F32))).astype(BF16)
            ex_ref[:, ps] = _dot(pm, lfp)
            b = ex_ref[0:c, ps]
            b_last = b[c - 1:c, :]
            st_a = st_ref[ha]
            st_b = st_ref[hb]
            o = _dot_nt(qp * jnp.exp2(b).astype(BF16),
                        _blockdiag(st_a.astype(BF16), st_b.astype(BF16)))
            sc = mk_ref[len(levels)] * _dot(qp, _blockdiag_t(kp)).astype(BF16)
            n_vpu = sum(h >= HG_VPU_LEVEL_MIN for h in levels)
            for l, h in enumerate(levels):
                if h >= HG_VPU_LEVEL_MIN:
                    ex = _level_exponents(b, h)
                else:
                    ex = ex_ref[(l - n_vpu + 1) * c:(l - n_vpu + 2) * c, ps]
                e = jnp.exp2(ex).astype(BF16)
                ke = kp * e
                s = _dot(qp * e, _blockdiag_t(ke))
                sc = sc + mk_ref[l] * s.astype(BF16)
            o = o + _dot(sc, _blockdiag(i_ref[rows, sa], i_ref[rows, sb]))
            kdec = kp * jnp.exp2(b_last - b).astype(BF16)
            sdec = jnp.exp2(b_last)
            st_ref[ha] = st_a * sdec[:, lo] + _dot_tn(i_ref[rows, sa], kdec[:, lo])
            st_ref[hb] = st_b * sdec[:, hi] + _dot_tn(i_ref[rows, sb], kdec[:, hi])
            o_ref[rows, sa] = (_rms(o[:, lo], nw) * g_ref[rows, sa].astype(F32)).astype(o_ref.dtype)
            o_ref[rows, sb] = (_rms(o[:, hi], nw) * g_ref[rows, sb].astype(F32)).astype(o_ref.dtype)
        return carry

    lax.fori_loop(0, q_ref.shape[0] // c, body, 0)


def _hgrn2(qh, lf, v, gate, norm_w, n_heads, tb, chunk, hps, cast_weights):
    t = qh.shape[0]
    gate_off = 0
    n_steps = (n_heads // hps) * (t // tb)
    for wgt in cast_weights:
        assert wgt.shape[0] % (16 * n_steps) == 0, "weight rows must split into bf16-tile-aligned slices"
    w_rows = [wgt.shape[0] // n_steps for wgt in cast_weights]
    w_specs = [pl.BlockSpec((r, wgt.shape[1]), lambda h, i: (h * (t // tb) + i, 0))
               for r, wgt in zip(w_rows, cast_weights)]
    levels, pm, mk = _hg_constants(chunk)
    pm = jnp.asarray(pm, BF16)
    mk = jnp.asarray(np.concatenate([mk, mk], axis=2), BF16)
    w = hps * HG_DK
    nblk = n_heads // hps
    col = lambda off: (lambda h, i: (i, off // w + h))
    const2 = lambda h, i: (0, 0)
    return pl.pallas_call(
        functools.partial(_hgrn2_kernel, chunk=chunk, levels=tuple(levels), ncast=len(cast_weights)),
        grid=(nblk, t // tb),
        in_specs=[
            pl.BlockSpec((tb, w), col(0)),
            pl.BlockSpec((tb, w), col(0)),
            pl.BlockSpec((tb, w), col(0)),
            pl.BlockSpec((tb, w), col(gate_off)),
            pl.BlockSpec((1, HG_DV), const2),
            pl.BlockSpec(pm.shape, const2),
            pl.BlockSpec(mk.shape, lambda h, i: (0, 0, 0)),
        ] + w_specs,
        out_specs=[pl.BlockSpec((tb, w), lambda h, i: (i, h))] + w_specs,
        out_shape=[jax.ShapeDtypeStruct((t, n_heads * HG_DV), BF16)]
        + [jax.ShapeDtypeStruct(wgt.shape, BF16) for wgt in cast_weights],
        scratch_shapes=[pltpu.VMEM((hps, HG_DV, HG_DK), F32),
                        pltpu.VMEM((pm.shape[0], w), F32)],
        compiler_params=_params(("parallel", "arbitrary")),
        name="hgrn2_scan",
    )(qh, lf, v, gate, norm_w, pm, mk, *cast_weights)


def _ssd_expand_mat(hpg):
    ea = np.zeros((128, hpg * SSM_HEADDIM), np.float32)
    for h in range(hpg):
        ea[h, h * SSM_HEADDIM:(h + 1) * SSM_HEADDIM] = 1.0
    return ea


def _ssd_kernel(z_ref, x_ref, b_ref, c_ref, dt_ref, dtb_ref, alog_ref, dskip_ref, nw_ref,
                triu_ref, ea_ref, o_ref, s_ref):
    @pl.when(pl.program_id(1) == 0)
    def _():
        s_ref[...] = jnp.zeros_like(s_ref)

    gw = s_ref.shape[2]
    n = s_ref.shape[1]
    q = triu_ref.shape[0]
    for r0 in range(0, x_ref.shape[0], q):
        rows = slice(r0, r0 + q)
        for gi in range(s_ref.shape[0]):
            cg = slice(gi * gw, (gi + 1) * gw)
            cn = slice(gi * n, (gi + 1) * n)
            _ssd_group(z_ref.at[rows, cg], x_ref.at[rows, cg], b_ref.at[rows, cn], c_ref.at[rows, cn],
                       dt_ref.at[gi, :, rows], dtb_ref.at[gi], alog_ref.at[gi], dskip_ref.at[gi],
                       nw_ref.at[gi], triu_ref, ea_ref, o_ref.at[rows, cg], s_ref.at[gi])


def _ssd_group(z_ref, x_ref, b_ref, c_ref, dt_ref, dtb_ref, alog_ref, dskip_ref, nw_ref,
               triu_ref, ea_ref, o_ref, s_ref):
    q = x_ref.shape[0]
    hpg = dt_ref.shape[0]
    bm16 = b_ref[...]
    cm16 = c_ref[...]

    dt_r = jax.nn.softplus(dt_ref[...] + dtb_ref[...])
    da_r = dt_r * (-jnp.exp(alog_ref[...]) * LOG2E)
    da_hi, da_lo = _split_bf16(da_r)
    triu = triu_ref[...]
    acum_r = _dot(da_hi, triu) + _dot(da_lo, triu)
    pad = jnp.zeros((128 - 2 * hpg, q), F32)
    cols = jnp.concatenate([acum_r, dt_r, pad], axis=0).T

    lane = lax.broadcasted_iota(jnp.int32, (q, 128), 1)
    acol = jnp.where(lane < hpg, cols, 0.0)
    a_last = acol[q - 1:q, :]
    dtwd = jnp.exp2(a_last - acol) * pltpu.roll(cols, 128 - hpg, axis=1)
    fx = _dot(jnp.concatenate([jnp.exp2(acol).astype(BF16), dtwd.astype(BF16)], axis=0), ea_ref[...])
    ea_x = fx[0:q]
    dtwd_x = fx[q:2 * q]
    sd_x = _dot(jnp.broadcast_to(jnp.exp2(a_last), (8, 128)).astype(BF16), ea_ref[...])[0:1, :]

    xs16 = x_ref[...]
    bmt = bm16.T
    cb16 = _dot(cm16, bmt).astype(BF16)
    cs = _dot(cm16, s_ref[...].astype(BF16))
    row = lax.broadcasted_iota(jnp.int32, (q, q), 0)
    colid = lax.broadcasted_iota(jnp.int32, (q, q), 1)
    causal = row >= colid
    lo_half = lane < SSM_HEADDIM

    y_parts = []
    for p in range(hpg // 2):
        ms = []
        for h in (2 * p, 2 * p + 1):
            dm = cols[:, h:h + 1] - acum_r[h:h + 1, :]
            lm16 = jnp.exp2(jnp.where(causal, dm, -1e30)).astype(BF16)
            ms.append(cb16 * lm16 * dt_r[h:h + 1, :].astype(BF16))
        yy = _dot(jnp.concatenate(ms, axis=0), xs16[:, p * 128:(p + 1) * 128])
        y_parts.append(jnp.where(lo_half, yy[0:q], yy[q:2 * q]))

    y = jnp.concatenate(y_parts, axis=1) + cs * ea_x
    s_ref[...] = s_ref[...] * sd_x + _dot(bmt, (x_ref[...].astype(F32) * dtwd_x).astype(BF16))

    y = y + dskip_ref[...] * x_ref[...].astype(F32)
    y = y * z_ref[...].astype(F32)
    o_ref[...] = _rms(y, nw_ref[...]).astype(o_ref.dtype)


def _ssd(zs, xbc, dt_hm, dt_bias, a_log, d_skip, norm_w, d_inner):
    t = xbc.shape[0]
    z_off = xbc_off = 0
    q = SSD_CHUNK
    g = SSM_GROUPS
    n = SSM_DSTATE
    gw = d_inner // g
    hpg = gw // SSM_HEADDIM
    triu = jnp.asarray(np.triu(np.ones((q, q), np.float32)), BF16)
    ea = jnp.asarray(_ssd_expand_mat(hpg), BF16)
    c2 = lambda gi, i: (0, 0)
    gs = SSD_GROUPS_PER_STEP
    assert g % gs == 0 and d_inner % (gs * n) == 0, "B / C column blocks must be block-aligned"
    tb = q * SSD_CHUNKS_PER_STEP
    return pl.pallas_call(
        _ssd_kernel,
        grid=(g // gs, t // tb),
        in_specs=[
            pl.BlockSpec((tb, gs * gw), lambda gi, i: (i, z_off // (gs * gw) + gi)),
            pl.BlockSpec((tb, gs * gw), lambda gi, i: (i, xbc_off // (gs * gw) + gi)),
            pl.BlockSpec((tb, gs * n), lambda gi, i: (i, (xbc_off + d_inner) // (gs * n) + gi)),
            pl.BlockSpec((tb, gs * n), lambda gi, i: (i, (xbc_off + d_inner + g * n) // (gs * n) + gi)),
            pl.BlockSpec((gs, hpg, tb), lambda gi, i: (gi, 0, i)),
            pl.BlockSpec((gs, hpg, 1), lambda gi, i: (gi, 0, 0)),
            pl.BlockSpec((gs, hpg, 1), lambda gi, i: (gi, 0, 0)),
            pl.BlockSpec((gs, 1, gw), lambda gi, i: (gi, 0, 0)),
            pl.BlockSpec((gs, 1, gw), lambda gi, i: (gi, 0, 0)),
            pl.BlockSpec((q, q), c2),
            pl.BlockSpec(ea.shape, c2),
        ],
        out_specs=pl.BlockSpec((tb, gs * gw), lambda gi, i: (i, gi)),
        out_shape=jax.ShapeDtypeStruct((t, d_inner), BF16),
        scratch_shapes=[pltpu.VMEM((gs, n, gw), F32)],
        compiler_params=_params(("parallel", "arbitrary")),
        name="ssd_scan",
    )(zs, xbc, xbc, xbc, dt_hm, dt_bias, a_log, d_skip, norm_w, triu, ea)


def _merge_kernel(yh_ref, ys_ref, wh_ref, ws_ref, gh_ref, gs_ref, o_ref):
    a = _dot(yh_ref[...], wh_ref[...])
    b = _dot(ys_ref[...], ws_ref[...])
    gh = _sigmoid(gh_ref[...].astype(F32))
    gs = _sigmoid(gs_ref[...].astype(F32))
    o_ref[...] = (gh * a + gs * b).astype(o_ref.dtype)


def _merge(y_hg, y_ssm, w_hg, w_ssm, gates, tm, tn):
    t = y_hg.shape[0]
    d = w_hg.shape[1]
    return pl.pallas_call(
        _merge_kernel,
        grid=(t // tm, d // tn),
        in_specs=[
            pl.BlockSpec((tm, y_hg.shape[1]), lambda m, j: (m, 0)),
            pl.BlockSpec((tm, y_ssm.shape[1]), lambda m, j: (m, 0)),
            pl.BlockSpec((w_hg.shape[0], tn), lambda m, j: (0, j)),
            pl.BlockSpec((w_ssm.shape[0], tn), lambda m, j: (0, j)),
            pl.BlockSpec((tm, tn), lambda m, j: (m, j)),
            pl.BlockSpec((tm, tn), lambda m, j: (m, d // tn + j)),
        ],
        out_specs=pl.BlockSpec((tm, tn), lambda m, j: (m, j)),
        out_shape=jax.ShapeDtypeStruct((t, d), BF16),
        compiler_params=_params(("parallel", "arbitrary")),
        name="branch_merge",
    )(y_hg, y_ssm, w_hg, w_ssm, gates, gates)


def _out_kernel(a_ref, w_ref, x_ref, post_ref, pre_ref, x1_ref, h_ref):
    w = w_ref[...]
    for r0 in range(0, x_ref.shape[0], CONV_ROWS):
        rows = slice(r0, r0 + CONV_ROWS)
        x1 = x_ref[rows, :] + _rms(_dot(a_ref[rows, :], w), post_ref[...])
        x1_ref[rows, :] = x1
        h_ref[rows, :] = _rms(x1, pre_ref[...]).astype(BF16)


def _out_proj(mixed, w_out, x, post_w, pre_w, tm):
    t, d = x.shape
    c2 = lambda m: (0, 0)
    row = lambda m: (m, 0)
    return pl.pallas_call(
        _out_kernel,
        grid=(t // tm,),
        in_specs=[
            pl.BlockSpec((tm, d), row),
            pl.BlockSpec((d, d), c2),
            pl.BlockSpec((tm, d), row),
            pl.BlockSpec((1, d), c2),
            pl.BlockSpec((1, d), c2),
        ],
        out_specs=[pl.BlockSpec((tm, d), row), pl.BlockSpec((tm, d), row)],
        out_shape=[jax.ShapeDtypeStruct((t, d), F32), jax.ShapeDtypeStruct((t, d), BF16)],
        compiler_params=_params(("parallel",)),
        name="out_proj",
    )(mixed, w_out, x, post_w, pre_w)


def _ffn_up_kernel(h_ref, wg_ref, wu_ref, cw_ref, cb_ref, o_ref, ext_ref, wgb_ref, wub_ref):
    @pl.when(pl.program_id(1) == 0)
    def _():
        wgb_ref[...] = wg_ref[...].astype(BF16)
        wub_ref[...] = wu_ref[...].astype(BF16)
        ext_ref[0:CARRY, :] = jnp.zeros((CARRY, ext_ref.shape[1]), F32)

    h = h_ref[...]
    gate = _dot(h, wgb_ref[...])
    up = _dot(h, wub_ref[...])
    conv = _conv_rows(ext_ref, gate, 0, cw_ref[...], cb_ref[...], FFN_CONV)
    _conv_keep_tail(ext_ref, o_ref.shape[0])
    o_ref[...] = (_gelu_tanh(conv) * up).astype(o_ref.dtype)


def _ffn_up(h, w_up, conv_w, conv_b, d_ff, tm, tn):
    t, d = h.shape
    k = conv_w.shape[0]
    return pl.pallas_call(
        _ffn_up_kernel,
        grid=(d_ff // tn, t // tm),
        in_specs=[
            pl.BlockSpec((tm, d), lambda j, m: (m, 0)),
            pl.BlockSpec((d, tn), lambda j, m: (0, j)),
            pl.BlockSpec((d, tn), lambda j, m: (0, d_ff // tn + j)),
            pl.BlockSpec((k, tn), lambda j, m: (0, j)),
            pl.BlockSpec((1, tn), lambda j, m: (0, j)),
        ],
        out_specs=pl.BlockSpec((tm, tn), lambda j, m: (m, j)),
        out_shape=jax.ShapeDtypeStruct((t, d_ff), BF16),
        scratch_shapes=[pltpu.VMEM((CARRY + tm, tn), F32),
                        pltpu.VMEM((d, tn), BF16), pltpu.VMEM((d, tn), BF16)],
        compiler_params=_params(("parallel", "arbitrary")),
        name="ffn_up",
    )(h, w_up, w_up, conv_w, conv_b)


def _ffn_down_kernel(a_ref, w_ref, x_ref, post_ref, o_ref):
    d = o_ref.shape[1]
    slabs = [slice(c, c + FFN_DOWN_SLAB) for c in range(0, d, FFN_DOWN_SLAB)]
    a = a_ref[...]
    for cs in slabs:
        o_ref[:, cs] = _dot(a, w_ref[:, cs])
    ssq = sum(jnp.sum(o_ref[:, cs] * o_ref[:, cs], axis=-1, keepdims=True) for cs in slabs)
    inv = lax.rsqrt(ssq / d + EPS)
    for cs in slabs:
        o_ref[:, cs] = x_ref[:, cs] + o_ref[:, cs] * inv * post_ref[:, cs]


def _ffn_down(act, w_down, x1, post_w, tm):
    t, d = x1.shape
    d_ff = act.shape[1]
    return pl.pallas_call(
        _ffn_down_kernel,
        grid=(t // tm,),
        in_specs=[
            pl.BlockSpec((tm, d_ff), lambda m: (m, 0)),
            pl.BlockSpec((d_ff, d), lambda m: (0, 0), pipeline_mode=pl.Buffered(1)),
            pl.BlockSpec((tm, d), lambda m: (m, 0)),
            pl.BlockSpec((1, d), lambda m: (0, 0)),
        ],
        out_specs=pl.BlockSpec((tm, d), lambda m: (m, 0)),
        out_shape=jax.ShapeDtypeStruct((t, d), F32),
        compiler_params=_params(("parallel",), VMEM_LIMIT_LARGE),
        name="ffn_down",
    )(act, w_down, x1, post_w)


def _layer(x, w_in, mix_pre, mix_post, lb_table, hg_norm, conv_w, conv_b, dt_bias, a_log, d_skip,
           ssm_norm, w_hg, w_ssm, w_out, ffn_pre, ffn_post, w_up, ffn_cw, ffn_cb, w_down):
    t, d = x.shape
    hg_v = w_hg.shape[0]
    hg_heads = hg_v // HG_DV
    d_inner = w_ssm.shape[0]
    ssm_heads = a_log.shape[0]
    hpg = ssm_heads // SSM_GROUPS
    gw = d_inner // SSM_GROUPS
    d_ff = w_down.shape[0]
    conv_dim = conv_w.shape[1]

    main_n = 4 * hg_v + d_inner + conv_dim
    dt_off = main_n
    gate_off = dt_off + ssm_heads
    z_off = 4 * hg_v
    xbc_off = z_off + d_inner

    w_t = w_in.T
    h, dt_t = _norm_dt(x, mix_pre.reshape(1, d), w_t[dt_off:gate_off], tm=NORM_ROWS)
    tm, tn = INPROJ_ROWS, INPROJ_TILE
    qh = _seg_proj(h, w_t, 0, hg_v, "qscale", (), tm, tn, "inproj_q")
    lf = _seg_proj(h, w_t, hg_v, hg_v, "logf", (lb_table,), tm, tn, "inproj_f")
    v = _proj_nt(h, w_t, 2 * hg_v, hg_v, tm, tn, "inproj_i")
    go = _seg_proj(h, w_t, 3 * hg_v, hg_v, "silu", (), tm, tn, "inproj_g")
    zs = _seg_proj(h, w_t, z_off, d_inner, "silu", (), tm, tn, "inproj_z")
    xbc = _seg_proj(h, w_t, xbc_off, conv_dim, "conv", (conv_w, conv_b.reshape(1, conv_dim)), tm, tn,
                    "inproj_xbc")
    gates = _proj_nt(h, w_t, gate_off, 2 * d, tm, tn, "inproj_gates")

    y_hg, w_hg16, w_ssm16, w_out16, w_down16 = _hgrn2(
        qh, lf, v, go, hg_norm.reshape(1, HG_DV), hg_heads, HG_BLOCK, HG_CHUNK, HG_HEADS_PER_STEP,
        cast_weights=(w_hg, w_ssm, w_out, w_down))

    dt_hm = dt_t.reshape(SSM_GROUPS, hpg, t)
    y_ssm = _ssd(
        zs, xbc, dt_hm,
        dt_bias.reshape(SSM_GROUPS, hpg, 1), a_log.reshape(SSM_GROUPS, hpg, 1),
        jnp.repeat(d_skip, SSM_HEADDIM).reshape(SSM_GROUPS, 1, gw),
        ssm_norm.reshape(SSM_GROUPS, 1, gw), d_inner)

    mixed = _merge(y_hg, y_ssm, w_hg16, w_ssm16, gates, tm=MERGE_ROWS, tn=MERGE_COLS)
    x1, h2 = _out_proj(mixed, w_out16, x, mix_post.reshape(1, d), ffn_pre.reshape(1, d), tm=OUT_ROWS)
    act = _ffn_up(h2, w_up, ffn_cw, ffn_cb.reshape(1, d_ff), d_ff, tm=FFN_UP_ROWS, tn=FFN_UP_COLS)
    return _ffn_down(act, w_down16, x1, ffn_post.reshape(1, d), tm=FFN_DOWN_ROWS)


def kernel(x, w_in, mix_pre_norm, mix_post_norm, hg_lb_table, hg_out_norm, ssm_conv_w, ssm_conv_b,
           ssm_dt_bias, ssm_A_log, ssm_D, ssm_out_norm, w_branch_hg, w_branch_ssm, w_out,
           ffn_pre_norm, ffn_post_norm, ffn_w_up, ffn_conv_w, ffn_conv_b, ffn_w_down):
    bsz, t, d = x.shape
    depth = w_in.shape[0]
    assert depth == 1 and hg_lb_table.shape[0] == 2, "forget-gate lower bound is computed for one layer"
    outs = []
    for b in range(bsz):
        xb = x[b]
        for l in range(depth):
            xb = _layer(xb, w_in[l], mix_pre_norm[l], mix_post_norm[l], hg_lb_table, hg_out_norm[l],
                        ssm_conv_w[l], ssm_conv_b[l], ssm_dt_bias[l], ssm_A_log[l], ssm_D[l],
                        ssm_out_norm[l], w_branch_hg[l], w_branch_ssm[l], w_out[l],
                        ffn_pre_norm[l], ffn_post_norm[l], ffn_w_up[l], ffn_conv_w[l], ffn_conv_b[l],
                        ffn_w_down[l])
        outs.append(xb)
    return jnp.stack(outs, axis=0)
```

```python
import functools

import numpy as np
import jax
import jax.numpy as jnp
from jax import lax
from jax.experimental import pallas as pl
from jax.experimental.pallas import tpu as pltpu

F32 = jnp.float32
BF16 = jnp.bfloat16
EPS = 1e-6

HG_DK = 128
HG_DV = 128
SSM_HEADDIM = 64
SSM_GROUPS = 8
SSM_DSTATE = 128
SSM_CONV = 4
FFN_CONV = 3

NORM_ROWS = 512
INPROJ_TILE = 1024
INPROJ_ROWS = 1024
HG_CHUNK = 128
HG_BLOCK = 512
HG_HEADS_PER_STEP = 16
HG_VPU_LEVEL_MIN = 8
SSD_CHUNK = 256
SSD_CHUNKS_PER_STEP = 2
SSD_GROUPS_PER_STEP = 8
MERGE_ROWS, MERGE_COLS = 1024, 512
OUT_ROWS = 512
FFN_UP_ROWS, FFN_UP_COLS = 1024, 512
FFN_DOWN_ROWS = 512
FFN_DOWN_SLAB = 512
CARRY = 8
CONV_ROWS = 256

VMEM_LIMIT = 48 * 1024 * 1024
VMEM_LIMIT_LARGE = 56 * 1024 * 1024
LOG2E = 1.4426950408889634


def _params(sem, vmem=VMEM_LIMIT):
    return pltpu.CompilerParams(dimension_semantics=sem, vmem_limit_bytes=vmem)


def _dot(a, b):
    return jnp.dot(a, b, preferred_element_type=F32)


def _dot_nt(a, b):
    return lax.dot_general(a, b, (((1,), (1,)), ((), ())), preferred_element_type=F32)


def _dot_tn(a, b):
    return lax.dot_general(a, b, (((0,), (0,)), ((), ())), preferred_element_type=F32)


def _split_bf16(a):
    hi = a.astype(BF16)
    lo = (a - hi.astype(F32)).astype(BF16)
    return hi, lo


def _sigmoid(a):
    return 0.5 * jnp.tanh(0.5 * a) + 0.5


def _silu(a):
    u = 0.5 * a
    return u * (jnp.tanh(u) + 1.0)


def _gelu_tanh(a):
    c0 = float(np.sqrt(2.0 / np.pi))
    return a * (0.5 * jnp.tanh(a * (c0 + (c0 * 0.044715) * (a * a))) + 0.5)


def _rms(x, w):
    return x * lax.rsqrt(jnp.mean(x * x, axis=-1, keepdims=True) + EPS) * w


def _norm_dt_kernel(x_ref, nw_ref, wdt_ref, h_ref, dt_ref):
    h = _rms(x_ref[...], nw_ref[...]).astype(BF16)
    h_ref[...] = h
    dt_ref[...] = _dot_nt(wdt_ref[...].astype(BF16), h)


def _norm_dt(x, nw, wdt_t, tm):
    t, d = x.shape
    nh = wdt_t.shape[0]
    return pl.pallas_call(
        _norm_dt_kernel,
        grid=(t // tm,),
        in_specs=[
            pl.BlockSpec((tm, d), lambda m: (m, 0)),
            pl.BlockSpec((1, d), lambda m: (0, 0)),
            pl.BlockSpec((nh, d), lambda m: (0, 0)),
        ],
        out_specs=[pl.BlockSpec((tm, d), lambda m: (m, 0)),
                   pl.BlockSpec((nh, tm), lambda m: (0, m))],
        out_shape=[jax.ShapeDtypeStruct((t, d), BF16), jax.ShapeDtypeStruct((nh, t), F32)],
        compiler_params=_params(("parallel",)),
        name="norm_dt",
    )(x, nw, wdt_t)


def _proj_nt_kernel(h_ref, w_ref, o_ref, wb_ref):
    @pl.when(pl.program_id(1) == 0)
    def _():
        wb_ref[...] = w_ref[...].astype(BF16)

    o_ref[...] = _dot_nt(h_ref[...], wb_ref[...]).astype(o_ref.dtype)


def _proj_nt(h, w_t, row_off, n, tm, tn, name):
    t, d = h.shape
    if row_off % tn == 0:
        w_spec = pl.BlockSpec((tn, d), lambda j, m: (row_off // tn + j, 0))
    else:
        assert row_off % 8 == 0 and tn % 8 == 0
        w_spec = pl.BlockSpec((pl.Element(tn), pl.Element(d)),
                              lambda j, m: (pl.multiple_of(row_off + j * tn, 8), 0))
    return pl.pallas_call(
        _proj_nt_kernel,
        grid=(n // tn, t // tm),
        in_specs=[pl.BlockSpec((tm, d), lambda j, m: (m, 0)), w_spec],
        out_specs=pl.BlockSpec((tm, tn), lambda j, m: (m, j)),
        out_shape=jax.ShapeDtypeStruct((t, n), BF16),
        scratch_shapes=[pltpu.VMEM((tn, d), BF16)],
        compiler_params=_params(("parallel", "arbitrary")),
        name=name,
    )(h, w_t)


def _conv_rows(ext_ref, cur, r0, w, b, ksize):
    rows = cur.shape[0]
    ext_ref[CARRY + r0:CARRY + r0 + rows, :] = cur
    acc = cur * w[ksize - 1:ksize, :] + b
    for j in range(1, ksize):
        acc = acc + ext_ref[CARRY + r0 - j:CARRY + r0 - j + rows, :] * w[ksize - 1 - j:ksize - j, :]
    return acc


def _conv_keep_tail(ext_ref, q):
    ext_ref[0:CARRY, :] = ext_ref[q:q + CARRY, :]


_STAGE_EXTRAS = {"qscale": 0, "silu": 0, "logf": 1, "conv": 2}


def _seg_proj_kernel(h_ref, w_ref, *rest, kind):
    nx = _STAGE_EXTRAS[kind]
    extra = rest[:nx]
    o_ref, wb_ref = rest[nx:nx + 2]
    first = pl.program_id(1) == 0

    @pl.when(first)
    def _():
        wb_ref[...] = w_ref[...].astype(BF16)

    if kind == "conv":
        ext_ref = rest[nx + 2]

        @pl.when(first)
        def _():
            ext_ref[0:CARRY, :] = jnp.zeros((CARRY, ext_ref.shape[1]), F32)

    if kind == "logf":
        tab = extra[0][...]
        te = jnp.exp(tab - jnp.max(tab, axis=0, keepdims=True))
        lb = te[0:1, :] / jnp.sum(te, axis=0, keepdims=True)
        f_amp = 0.5 * (1.0 - lb)
        f_mid = lb + f_amp
    elif kind == "conv":
        w = 0.5 * extra[0][...]
        bias = 0.5 * extra[1][...]
    wb = wb_ref[...]
    step = CONV_ROWS if kind == "conv" else o_ref.shape[0]
    for r0 in range(0, o_ref.shape[0], step):
        acc = _dot_nt(h_ref[r0:r0 + step, :], wb)
        if kind == "qscale":
            out = _silu(acc) * HG_DK ** -0.5
        elif kind == "silu":
            out = _silu(acc)
        elif kind == "logf":
            out = jnp.log(f_mid + f_amp * jnp.tanh(0.5 * acc)) * LOG2E
        else:
            u = _conv_rows(ext_ref, acc, r0, w, bias, SSM_CONV)
            out = u * (jnp.tanh(u) + 1.0)
        o_ref[r0:r0 + step, :] = out.astype(o_ref.dtype)
    if kind == "conv":
        _conv_keep_tail(ext_ref, o_ref.shape[0])


def _seg_proj(h, w_t, row_off, n, kind, extras, tm, tn, name):
    t, d = h.shape
    assert row_off % tn == 0 and len(extras) == _STAGE_EXTRAS[kind]
    scratch = [pltpu.VMEM((tn, d), BF16)]
    if kind == "conv":
        scratch.append(pltpu.VMEM((CARRY + tm, tn), F32))
    return pl.pallas_call(
        functools.partial(_seg_proj_kernel, kind=kind),
        grid=(n // tn, t // tm),
        in_specs=[
            pl.BlockSpec((tm, d), lambda j, m: (m, 0)),
            pl.BlockSpec((tn, d), lambda j, m: (row_off // tn + j, 0)),
        ] + [pl.BlockSpec((e.shape[0], tn), lambda j, m: (0, j)) for e in extras],
        out_specs=pl.BlockSpec((tm, tn), lambda j, m: (m, j)),
        out_shape=jax.ShapeDtypeStruct((t, n), BF16),
        scratch_shapes=scratch,
        compiler_params=_params(("parallel", "arbitrary")),
        name=name,
    )(h, w_t, *extras)


def _hg_constants(c):
    levels = []
    h = c // 2
    while h >= 1:
        levels.append(h)
        h //= 2
    t = np.arange(c)[:, None]
    u = np.arange(c)[None, :]
    mats = [(u <= t)]
    masks = []
    for h in levels:
        blk = 2 * h
        mid = (t // blk) * blk + h
        second = (t % blk) >= h
        if h < HG_VPU_LEVEL_MIN:
            mats.append(np.where(second, (u >= mid) & (u <= t), (u > t) & (u < mid)))
        s = u
        masks.append(((t // blk) == (s // blk)) & second & ((s % blk) < h))
    masks.append(t == u)
    pm = np.concatenate(mats, axis=0).astype(np.float32)
    mk = np.stack(masks, axis=0).astype(np.float32)
    return levels, pm, mk


def _level_exponents(b, h):
    c = b.shape[0]
    parts = []
    for start in range(0, c, 2 * h):
        mid = start + h
        ref = b[mid - 1:mid, :]
        parts.append(ref - b[start:mid, :])
        parts.append(b[mid:mid + h, :] - ref)
    return jnp.concatenate(parts, axis=0)


def _blockdiag(a, b):
    top = jnp.concatenate([a, jnp.zeros((a.shape[0], b.shape[1]), a.dtype)], axis=1)
    bot = jnp.concatenate([jnp.zeros((b.shape[0], a.shape[1]), b.dtype), b], axis=1)
    return jnp.concatenate([top, bot], axis=0)


def _blockdiag_t(kpair):
    kt = kpair.T
    half = kt.shape[0] // 2
    return _blockdiag(kt[0:half], kt[half:])


def _hgrn2_kernel(q_ref, lf_ref, i_ref, g_ref, nw_ref, pm_ref, mk_ref, *rest, chunk, levels, ncast):
    c = chunk
    w_in_refs = rest[:ncast]
    o_ref = rest[ncast]
    w_out_refs = rest[ncast + 1:2 * ncast + 1]
    st_ref, ex_ref = rest[2 * ncast + 1:]
    for src, dst in zip(w_in_refs, w_out_refs):
        dst[...] = src[...].astype(BF16)

    @pl.when(pl.program_id(1) == 0)
    def _():
        st_ref[...] = jnp.zeros_like(st_ref)

    nw = nw_ref[...]
    pm = pm_ref[...]
    nh = st_ref.shape[0]
    dk = HG_DK

    def body(ci, carry):
        r0 = pl.multiple_of(ci * c, c)
        rows = pl.ds(r0, c)
        for p in range(nh // 2):
            ha, hb = 2 * p, 2 * p + 1
            sa = slice(ha * dk, (ha + 1) * dk)
            sb = slice(hb * dk, (hb + 1) * dk)
            ps = slice(ha * dk, (hb + 1) * dk)
            lo, hi = slice(0, dk), slice(dk, 2 * dk)
            qp = q_ref[rows, ps]
            lfp = lf_ref[rows, ps]
            kp = (1.0 - jnp.exp2(lfp.astype(F32))).astype(BF16)
            ex_ref[:, ps] = _dot(pm, lfp)
            b = ex_ref[0:c, ps]
            b_last = b[c - 1:c, :]
            st_a = st_ref[ha]
            st_b = st_ref[hb]
            o = _dot_nt(qp * jnp.exp2(b).astype(BF16),
                        _blockdiag(st_a.astype(BF16), st_b.astype(BF16)))
            sc = mk_ref[len(levels)] * _dot(qp, _blockdiag_t(kp)).astype(BF16)
            n_vpu = sum(h >= HG_VPU_LEVEL_MIN for h in levels)
            for l, h in enumerate(levels):
                if h >= HG_VPU_LEVEL_MIN:
                    ex = _level_exponents(ex_ref.at[0:c, ps], h)
                else:
                    ex = ex_ref[(l - n_vpu + 1) * c:(l - n_vpu + 2) * c, ps]
                e = jnp.exp2(ex).astype(BF16)
                ke = kp * e
                s = _dot(qp * e, _blockdiag_t(ke))
                sc = sc + mk_ref[l] * s.astype(BF16)
            o = o + _dot(sc, _blockdiag(i_ref[rows, sa], i_ref[rows, sb]))
            kdec = kp * jnp.exp2(b_last - ex_ref[0:c, ps]).astype(BF16)
            sdec = jnp.exp2(b_last)
            st_ref[ha] = st_a * sdec[:, lo] + _dot_tn(i_ref[rows, sa], kdec[:, lo])
            st_ref[hb] = st_b * sdec[:, hi] + _dot_tn(i_ref[rows, sb], kdec[:, hi])
            o_ref[rows, sa] = (_rms(o[:, lo], nw) * g_ref[rows, sa].astype(F32)).astype(o_ref.dtype)
            o_ref[rows, sb] = (_rms(o[:, hi], nw) * g_ref[rows, sb].astype(F32)).astype(o_ref.dtype)
        return carry

    lax.fori_loop(0, q_ref.shape[0] // c, body, 0)


def _hgrn2(qh, lf, v, gate, norm_w, n_heads, tb, chunk, hps, cast_weights):
    t = qh.shape[0]
    gate_off = 0
    n_steps = (n_heads // hps) * (t // tb)
    for wgt in cast_weights:
        assert wgt.shape[0] % (16 * n_steps) == 0, "weight rows must split into bf16-tile-aligned slices"
    w_rows = [wgt.shape[0] // n_steps for wgt in cast_weights]
    w_specs = [pl.BlockSpec((r, wgt.shape[1]), lambda h, i: (h * (t // tb) + i, 0))
               for r, wgt in zip(w_rows, cast_weights)]
    levels, pm, mk = _hg_constants(chunk)
    pm = jnp.asarray(pm, BF16)
    mk = jnp.asarray(np.concatenate([mk, mk], axis=2), BF16)
    w = hps * HG_DK
    nblk = n_heads // hps
    col = lambda off: (lambda h, i: (i, off // w + h))
    const2 = lambda h, i: (0, 0)
    return pl.pallas_call(
        functools.partial(_hgrn2_kernel, chunk=chunk, levels=tuple(levels), ncast=len(cast_weights)),
        grid=(nblk, t // tb),
        in_specs=[
            pl.BlockSpec((tb, w), col(0)),
            pl.BlockSpec((tb, w), col(0)),
            pl.BlockSpec((tb, w), col(0)),
            pl.BlockSpec((tb, w), col(gate_off)),
            pl.BlockSpec((1, HG_DV), const2),
            pl.BlockSpec(pm.shape, const2),
            pl.BlockSpec(mk.shape, lambda h, i: (0, 0, 0)),
        ] + w_specs,
        out_specs=[pl.BlockSpec((tb, w), lambda h, i: (i, h))] + w_specs,
        out_shape=[jax.ShapeDtypeStruct((t, n_heads * HG_DV), BF16)]
        + [jax.ShapeDtypeStruct(wgt.shape, BF16) for wgt in cast_weights],
        scratch_shapes=[pltpu.VMEM((hps, HG_DV, HG_DK), F32),
                        pltpu.VMEM((pm.shape[0], w), F32)],
        compiler_params=_params(("parallel", "arbitrary")),
        name="hgrn2_scan",
    )(qh, lf, v, gate, norm_w, pm, mk, *cast_weights)


def _ssd_expand_mat(hpg):
    ea = np.zeros((128, hpg * SSM_HEADDIM), np.float32)
    for h in range(hpg):
        ea[h, h * SSM_HEADDIM:(h + 1) * SSM_HEADDIM] = 1.0
    return ea


def _ssd_kernel(z_ref, x_ref, b_ref, c_ref, dt_ref, dtb_ref, alog_ref, dskip_ref, nw_ref,
                triu_ref, ea_ref, o_ref, s_ref):
    @pl.when(pl.program_id(1) == 0)
    def _():
        s_ref[...] = jnp.zeros_like(s_ref)

    gw = s_ref.shape[2]
    n = s_ref.shape[1]
    q = triu_ref.shape[0]
    for r0 in range(0, x_ref.shape[0], q):
        rows = slice(r0, r0 + q)
        for gi in range(s_ref.shape[0]):
            cg = slice(gi * gw, (gi + 1) * gw)
            cn = slice(gi * n, (gi + 1) * n)
            _ssd_group(z_ref.at[rows, cg], x_ref.at[rows, cg], b_ref.at[rows, cn], c_ref.at[rows, cn],
                       dt_ref.at[gi, :, rows], dtb_ref.at[gi], alog_ref.at[gi], dskip_ref.at[gi],
                       nw_ref.at[gi], triu_ref, ea_ref, o_ref.at[rows, cg], s_ref.at[gi])


def _ssd_group(z_ref, x_ref, b_ref, c_ref, dt_ref, dtb_ref, alog_ref, dskip_ref, nw_ref,
               triu_ref, ea_ref, o_ref, s_ref):
    q = x_ref.shape[0]
    hpg = dt_ref.shape[0]
    bm16 = b_ref[...]
    cm16 = c_ref[...]

    dt_r = jax.nn.softplus(dt_ref[...] + dtb_ref[...])
    da_r = dt_r * (-jnp.exp(alog_ref[...]) * LOG2E)
    da_hi, da_lo = _split_bf16(da_r)
    triu = triu_ref[...]
    acum_r = _dot(da_hi, triu) + _dot(da_lo, triu)
    pad = jnp.zeros((128 - 2 * hpg, q), F32)
    cols = jnp.concatenate([acum_r, dt_r, pad], axis=0).T

    lane = lax.broadcasted_iota(jnp.int32, (q, 128), 1)
    acol = jnp.where(lane < hpg, cols, 0.0)
    a_last = acol[q - 1:q, :]
    dtwd = jnp.exp2(a_last - acol) * pltpu.roll(cols, 128 - hpg, axis=1)
    fx = _dot(jnp.concatenate([jnp.exp2(acol).astype(BF16), dtwd.astype(BF16)], axis=0), ea_ref[...])
    ea_x = fx[0:q]
    dtwd_x = fx[q:2 * q]
    sd_x = _dot(jnp.broadcast_to(jnp.exp2(a_last), (8, 128)).astype(BF16), ea_ref[...])[0:1, :]

    xs16 = x_ref[...]
    bmt = bm16.T
    cb16 = _dot(cm16, bmt).astype(BF16)
    cs = _dot(cm16, s_ref[...].astype(BF16))
    row = lax.broadcasted_iota(jnp.int32, (q, q), 0)
    colid = lax.broadcasted_iota(jnp.int32, (q, q), 1)
    causal = row >= colid
    lo_half = lane < SSM_HEADDIM

    y_parts = []
    for p in range(hpg // 2):
        ms = []
        for h in (2 * p, 2 * p + 1):
            dm = cols[:, h:h + 1] - acum_r[h:h + 1, :]
            lm16 = jnp.exp2(jnp.where(causal, dm, -1e30)).astype(BF16)
            ms.append(cb16 * lm16 * dt_r[h:h + 1, :].astype(BF16))
        yy = _dot(jnp.concatenate(ms, axis=0), xs16[:, p * 128:(p + 1) * 128])
        y_parts.append(jnp.where(lo_half, yy[0:q], yy[q:2 * q]))

    y = jnp.concatenate(y_parts, axis=1) + cs * ea_x
    s_ref[...] = s_ref[...] * sd_x + _dot(bmt, (x_ref[...].astype(F32) * dtwd_x).astype(BF16))

    y = y + dskip_ref[...] * x_ref[...].astype(F32)
    y = y * z_ref[...].astype(F32)
    o_ref[...] = _rms(y, nw_ref[...]).astype(o_ref.dtype)


def _ssd(zs, xbc, dt_hm, dt_bias, a_log, d_skip, norm_w, d_inner):
    t = xbc.shape[0]
    z_off = xbc_off = 0
    q = SSD_CHUNK
    g = SSM_GROUPS
    n = SSM_DSTATE
    gw = d_inner // g
    hpg = gw // SSM_HEADDIM
    triu = jnp.asarray(np.triu(np.ones((q, q), np.float32)), BF16)
    ea = jnp.asarray(_ssd_expand_mat(hpg), BF16)
    c2 = lambda gi, i: (0, 0)
    gs = SSD_GROUPS_PER_STEP
    assert g % gs == 0 and d_inner % (gs * n) == 0, "B / C column blocks must be block-aligned"
    tb = q * SSD_CHUNKS_PER_STEP
    return pl.pallas_call(
        _ssd_kernel,
        grid=(g // gs, t // tb),
        in_specs=[
            pl.BlockSpec((tb, gs * gw), lambda gi, i: (i, z_off // (gs * gw) + gi)),
            pl.BlockSpec((tb, gs * gw), lambda gi, i: (i, xbc_off // (gs * gw) + gi)),
            pl.BlockSpec((tb, gs * n), lambda gi, i: (i, (xbc_off + d_inner) // (gs * n) + gi)),
            pl.BlockSpec((tb, gs * n), lambda gi, i: (i, (xbc_off + d_inner + g * n) // (gs * n) + gi)),
            pl.BlockSpec((gs, hpg, tb), lambda gi, i: (gi, 0, i)),
            pl.BlockSpec((gs, hpg, 1), lambda gi, i: (gi, 0, 0)),
            pl.BlockSpec((gs, hpg, 1), lambda gi, i: (gi, 0, 0)),
            pl.BlockSpec((gs, 1, gw), lambda gi, i: (gi, 0, 0)),
            pl.BlockSpec((gs, 1, gw), lambda gi, i: (gi, 0, 0)),
            pl.BlockSpec((q, q), c2),
            pl.BlockSpec(ea.shape, c2),
        ],
        out_specs=pl.BlockSpec((tb, gs * gw), lambda gi, i: (i, gi)),
        out_shape=jax.ShapeDtypeStruct((t, d_inner), BF16),
        scratch_shapes=[pltpu.VMEM((gs, n, gw), F32)],
        compiler_params=_params(("parallel", "arbitrary")),
        name="ssd_scan",
    )(zs, xbc, xbc, xbc, dt_hm, dt_bias, a_log, d_skip, norm_w, triu, ea)


def _merge_kernel(yh_ref, ys_ref, wh_ref, ws_ref, gh_ref, gs_ref, o_ref):
    a = _dot(yh_ref[...], wh_ref[...])
    b = _dot(ys_ref[...], ws_ref[...])
    gh = _sigmoid(gh_ref[...].astype(F32))
    gs = _sigmoid(gs_ref[...].astype(F32))
    o_ref[...] = (gh * a + gs * b).astype(o_ref.dtype)


def _merge(y_hg, y_ssm, w_hg, w_ssm, gates, tm, tn):
    t = y_hg.shape[0]
    d = w_hg.shape[1]
    return pl.pallas_call(
        _merge_kernel,
        grid=(t // tm, d // tn),
        in_specs=[
            pl.BlockSpec((tm, y_hg.shape[1]), lambda m, j: (m, 0)),
            pl.BlockSpec((tm, y_ssm.shape[1]), lambda m, j: (m, 0)),
            pl.BlockSpec((w_hg.shape[0], tn), lambda m, j: (0, j)),
            pl.BlockSpec((w_ssm.shape[0], tn), lambda m, j: (0, j)),
            pl.BlockSpec((tm, tn), lambda m, j: (m, j)),
            pl.BlockSpec((tm, tn), lambda m, j: (m, d // tn + j)),
        ],
        out_specs=pl.BlockSpec((tm, tn), lambda m, j: (m, j)),
        out_shape=jax.ShapeDtypeStruct((t, d), BF16),
        compiler_params=_params(("parallel", "arbitrary")),
        name="branch_merge",
    )(y_hg, y_ssm, w_hg, w_ssm, gates, gates)


def _out_kernel(a_ref, w_ref, x_ref, post_ref, pre_ref, x1_ref, h_ref):
    w = w_ref[...]
    for r0 in range(0, x_ref.shape[0], CONV_ROWS):
        rows = slice(r0, r0 + CONV_ROWS)
        x1 = x_ref[rows, :] + _rms(_dot(a_ref[rows, :], w), post_ref[...])
        x1_ref[rows, :] = x1
        h_ref[rows, :] = _rms(x1, pre_ref[...]).astype(BF16)


def _out_proj(mixed, w_out, x, post_w, pre_w, tm):
    t, d = x.shape
    c2 = lambda m: (0, 0)
    row = lambda m: (m, 0)
    return pl.pallas_call(
        _out_kernel,
        grid=(t // tm,),
        in_specs=[
            pl.BlockSpec((tm, d), row),
            pl.BlockSpec((d, d), c2),
            pl.BlockSpec((tm, d), row),
            pl.BlockSpec((1, d), c2),
            pl.BlockSpec((1, d), c2),
        ],
        out_specs=[pl.BlockSpec((tm, d), row), pl.BlockSpec((tm, d), row)],
        out_shape=[jax.ShapeDtypeStruct((t, d), F32), jax.ShapeDtypeStruct((t, d), BF16)],
        compiler_params=_params(("parallel",)),
        name="out_proj",
    )(mixed, w_out, x, post_w, pre_w)


def _ffn_up_kernel(h_ref, wg_ref, wu_ref, cw_ref, cb_ref, o_ref, ext_ref, wgb_ref, wub_ref):
    @pl.when(pl.program_id(1) == 0)
    def _():
        wgb_ref[...] = wg_ref[...].astype(BF16)
        wub_ref[...] = wu_ref[...].astype(BF16)
        ext_ref[0:CARRY, :] = jnp.zeros((CARRY, ext_ref.shape[1]), F32)

    h = h_ref[...]
    gate = _dot(h, wgb_ref[...])
    up = _dot(h, wub_ref[...])
    conv = _conv_rows(ext_ref, gate, 0, cw_ref[...], cb_ref[...], FFN_CONV)
    _conv_keep_tail(ext_ref, o_ref.shape[0])
    o_ref[...] = (_gelu_tanh(conv) * up).astype(o_ref.dtype)


def _ffn_up(h, w_up, conv_w, conv_b, d_ff, tm, tn):
    t, d = h.shape
    k = conv_w.shape[0]
    return pl.pallas_call(
        _ffn_up_kernel,
        grid=(d_ff // tn, t // tm),
        in_specs=[
            pl.BlockSpec((tm, d), lambda j, m: (m, 0)),
            pl.BlockSpec((d, tn), lambda j, m: (0, j)),
            pl.BlockSpec((d, tn), lambda j, m: (0, d_ff // tn + j)),
            pl.BlockSpec((k, tn), lambda j, m: (0, j)),
            pl.BlockSpec((1, tn), lambda j, m: (0, j)),
        ],
        out_specs=pl.BlockSpec((tm, tn), lambda j, m: (m, j)),
        out_shape=jax.ShapeDtypeStruct((t, d_ff), BF16),
        scratch_shapes=[pltpu.VMEM((CARRY + tm, tn), F32),
                        pltpu.VMEM((d, tn), BF16), pltpu.VMEM((d, tn), BF16)],
        compiler_params=_params(("parallel", "arbitrary")),
        name="ffn_up",
    )(h, w_up, w_up, conv_w, conv_b)


def _ffn_down_kernel(a_ref, w_ref, x_ref, post_ref, o_ref):
    d = o_ref.shape[1]
    slabs = [slice(c, c + FFN_DOWN_SLAB) for c in range(0, d, FFN_DOWN_SLAB)]
    a = a_ref[...]
    for cs in slabs:
        o_ref[:, cs] = _dot(a, w_ref[:, cs])
    ssq = sum(jnp.sum(o_ref[:, cs] * o_ref[:, cs], axis=-1, keepdims=True) for cs in slabs)
    inv = lax.rsqrt(ssq / d + EPS)
    for cs in slabs:
        o_ref[:, cs] = x_ref[:, cs] + o_ref[:, cs] * inv * post_ref[:, cs]


def _ffn_down(act, w_down, x1, post_w, tm):
    t, d = x1.shape
    d_ff = act.shape[1]
    return pl.pallas_call(
        _ffn_down_kernel,
        grid=(t // tm,),
        in_specs=[
            pl.BlockSpec((tm, d_ff), lambda m: (m, 0)),
            pl.BlockSpec((d_ff, d), lambda m: (0, 0), pipeline_mode=pl.Buffered(1)),
            pl.BlockSpec((tm, d), lambda m: (m, 0)),
            pl.BlockSpec((1, d), lambda m: (0, 0)),
        ],
        out_specs=pl.BlockSpec((tm, d), lambda m: (m, 0)),
        out_shape=jax.ShapeDtypeStruct((t, d), F32),
        compiler_params=_params(("parallel",), VMEM_LIMIT_LARGE),
        name="ffn_down",
    )(act, w_down, x1, post_w)


def _layer(x, w_in, mix_pre, mix_post, lb_table, hg_norm, conv_w, conv_b, dt_bias, a_log, d_skip,
           ssm_norm, w_hg, w_ssm, w_out, ffn_pre, ffn_post, w_up, ffn_cw, ffn_cb, w_down):
    t, d = x.shape
    hg_v = w_hg.shape[0]
    hg_heads = hg_v // HG_DV
    d_inner = w_ssm.shape[0]
    ssm_heads = a_log.shape[0]
    hpg = ssm_heads // SSM_GROUPS
    gw = d_inner // SSM_GROUPS
    d_ff = w_down.shape[0]
    conv_dim = conv_w.shape[1]

    main_n = 4 * hg_v + d_inner + conv_dim
    dt_off = main_n
    gate_off = dt_off + ssm_heads
    z_off = 4 * hg_v
    xbc_off = z_off + d_inner

    w_t = w_in.T
    h, dt_t = _norm_dt(x, mix_pre.reshape(1, d), w_t[dt_off:gate_off], tm=NORM_ROWS)
    tm, tn = INPROJ_ROWS, INPROJ_TILE
    qh = _seg_proj(h, w_t, 0, hg_v, "qscale", (), tm, tn, "inproj_q")
    lf = _seg_proj(h, w_t, hg_v, hg_v, "logf", (lb_table,), tm, tn, "inproj_f")
    v = _proj_nt(h, w_t, 2 * hg_v, hg_v, tm, tn, "inproj_i")
    go = _seg_proj(h, w_t, 3 * hg_v, hg_v, "silu", (), tm, tn, "inproj_g")
    zs = _seg_proj(h, w_t, z_off, d_inner, "silu", (), tm, tn, "inproj_z")
    xbc = _seg_proj(h, w_t, xbc_off, conv_dim, "conv", (conv_w, conv_b.reshape(1, conv_dim)), tm, tn,
                    "inproj_xbc")
    gates = _proj_nt(h, w_t, gate_off, 2 * d, tm, tn, "inproj_gates")

    y_hg, w_hg16, w_ssm16, w_out16, w_down16 = _hgrn2(
        qh, lf, v, go, hg_norm.reshape(1, HG_DV), hg_heads, HG_BLOCK, HG_CHUNK, HG_HEADS_PER_STEP,
        cast_weights=(w_hg, w_ssm, w_out, w_down))

    dt_hm = dt_t.reshape(SSM_GROUPS, hpg, t)
    y_ssm = _ssd(
        zs, xbc, dt_hm,
        dt_bias.reshape(SSM_GROUPS, hpg, 1), a_log.reshape(SSM_GROUPS, hpg, 1),
        jnp.repeat(d_skip, SSM_HEADDIM).reshape(SSM_GROUPS, 1, gw),
        ssm_norm.reshape(SSM_GROUPS, 1, gw), d_inner)

    mixed = _merge(y_hg, y_ssm, w_hg16, w_ssm16, gates, tm=MERGE_ROWS, tn=MERGE_COLS)
    x1, h2 = _out_proj(mixed, w_out16, x, mix_post.reshape(1, d), ffn_pre.reshape(1, d), tm=OUT_ROWS)
    act = _ffn_up(h2, w_up, ffn_cw, ffn_cb.reshape(1, d_ff), d_ff, tm=FFN_UP_ROWS, tn=FFN_UP_COLS)
    return _ffn_down(act, w_down16, x1, ffn_post.reshape(1, d), tm=FFN_DOWN_ROWS)


def kernel(x, w_in, mix_pre_norm, mix_post_norm, hg_lb_table, hg_out_norm, ssm_conv_w, ssm_conv_b,
           ssm_dt_bias, ssm_A_log, ssm_D, ssm_out_norm, w_branch_hg, w_branch_ssm, w_out,
           ffn_pre_norm, ffn_post_norm, ffn_w_up, ffn_conv_w, ffn_conv_b, ffn_w_down):
    bsz, t, d = x.shape
    depth = w_in.shape[0]
    assert depth == 1 and hg_lb_table.shape[0] == 2, "forget-gate lower bound is computed for one layer"
    outs = []
    for b in range(bsz):
        xb = x[b]
        for l in range(depth):
            xb = _layer(xb, w_in[l], mix_pre_norm[l], mix_post_norm[l], hg_lb_table, hg_out_norm[l],
                        ssm_conv_w[l], ssm_conv_b[l], ssm_dt_bias[l], ssm_A_log[l], ssm_D[l],
                        ssm_out_norm[l], w_branch_hg[l], w_branch_ssm[l], w_out[l],
                        ffn_pre_norm[l], ffn_post_norm[l], ffn_w_up[l], ffn_conv_w[l], ffn_conv_b[l],
                        ffn_w_down[l])
        outs.append(xb)
    return jnp.stack(outs, axis=0)
```
